```python
import math
import jax, jax.numpy as jnp
from jax import lax
import numpy as np

D_MODEL = 2048
BATCH = 8
SEQ = 2048
DEPTH = 2

GRID_W = 64
CTX_LEN = 256
HEAD_DIM = 128
ATT_HEADS = 6
ATT_KV_HEADS = 2
RET_HEADS = 4
RET_DK = 128
RET_DV = 128
SSD_HEADS = 12
SSD_HEAD_DIM = 64
SSD_GROUPS = 2
SSD_STATE = 128
SSD_CONV = 5
ATT_W = ATT_HEADS * HEAD_DIM
KV_W = ATT_KV_HEADS * HEAD_DIM
RET_QK_W = RET_HEADS * RET_DK
RET_W = RET_HEADS * RET_DV
SSD_W = SSD_HEADS * SSD_HEAD_DIM
SSD_BC = SSD_GROUPS * SSD_STATE
CONV_CH = SSD_W + 2 * SSD_BC
D_MIX = ATT_W + RET_W + SSD_W
IN_WIDTHS = (ATT_W, KV_W, KV_W, RET_QK_W, RET_QK_W, RET_W, RET_W, SSD_W, CONV_CH, SSD_HEADS, SSD_HEADS)
D_IN = sum(IN_WIDTHS)
D_FF = -(-8 * D_MODEL // (3 * 256)) * 256
CHUNK = 128
Q_BLOCK = 128
ROPE_THETA = 10000.0
EPS = 1e-6

kernel_name = "hybrid_attn_retention_ssd_dit_block"


def rmsnorm(x, g=None):
    xf = x.astype(jnp.float32)
    y = xf * lax.rsqrt(jnp.mean(xf * xf, axis=-1, keepdims=True) + EPS)
    if g is not None:
        y = y * g.astype(jnp.float32)
    return y.astype(x.dtype)


def axial_rope(rows):
    row = jnp.repeat(jnp.arange(rows, dtype=jnp.float32), GRID_W)
    col = jnp.tile(jnp.arange(GRID_W, dtype=jnp.float32), rows)
    n_freq = HEAD_DIM // 4
    inv = ROPE_THETA ** (-jnp.arange(n_freq, dtype=jnp.float32) / n_freq)
    ang = jnp.concatenate([row[:, None] * inv, col[:, None] * inv], axis=-1)
    return jnp.cos(ang), jnp.sin(ang)


def apply_rope(x, cos, sin):
    xf = x.astype(jnp.float32).reshape(*x.shape[:-1], -1, 2)
    x1, x2 = xf[..., 0], xf[..., 1]
    cs, sn = cos[None, :, None, :], sin[None, :, None, :]
    out = jnp.stack([x1 * cs - x2 * sn, x1 * sn + x2 * cs], axis=-1).reshape(x.shape)
    return out.astype(x.dtype)


def dwconv_centred(u, w, b):
    k = w.shape[0]
    out = lax.conv_general_dilated(u, w[:, None, :].astype(u.dtype), window_strides=(1,),
                                   padding=[(k // 2, k // 2)], dimension_numbers=('NWC', 'WIO', 'NWC'),
                                   feature_group_count=u.shape[-1])
    return out + b


def chunked_scan(q, k, v, log_a, s0):
    bsz, L, H, N = q.shape
    P = v.shape[-1]
    nc = L // CHUNK
    f32 = jnp.float32
    qc = q.reshape(bsz, nc, CHUNK, H, N).astype(f32)
    kc = k.reshape(bsz, nc, CHUNK, H, N).astype(f32)
    vc = v.reshape(bsz, nc, CHUNK, H, P).astype(f32)
    acs = jnp.cumsum(log_a.astype(f32).reshape(bsz, nc, CHUNK, H), axis=2)
    tri = jnp.tril(jnp.ones((CHUNK, CHUNK), dtype=bool))
    seg = acs[:, :, :, None, :] - acs[:, :, None, :, :]
    dmat = jnp.exp(jnp.where(tri[None, None, :, :, None], seg, -jnp.inf))
    scores = jnp.einsum('bcihn,bcjhn->bcijh', qc, kc) * dmat
    y_intra = jnp.einsum('bcijh,bcjhp->bcihp', scores, vc)
    decay_end = jnp.exp(acs[:, :, -1:, :] - acs)
    states = jnp.einsum('bcjhn,bcjh,bcjhp->cbhnp', kc, decay_end, vc)
    chunk_decay = jnp.exp(acs[:, :, -1, :]).transpose(1, 0, 2)

    def step(s, inp):
        st, dec = inp
        return s * dec[..., None, None] + st, s

    s_final, s_enter = lax.scan(step, s0.astype(f32), (states, chunk_decay))
    y_inter = jnp.einsum('bcihn,cbhnp,bcih->bcihp', qc, s_enter, jnp.exp(acs))
    return (y_intra + y_inter).reshape(bsz, L, H, P).astype(v.dtype), s_final


def bidir_scan(q, k_f, k_b, v, la_f, la_b, s0_f, s0_b):
    y_f, s_f = chunked_scan(q, k_f, v, la_f, s0_f)
    fl = lambda t: jnp.flip(t, axis=1)
    y_b, s_b = chunked_scan(fl(q), fl(k_b), fl(v), fl(la_b), s0_b)
    return y_f + fl(y_b), s_f, s_b


def block_attention(q, k, v):
    bsz, L, hq, dh = q.shape
    rep = hq // ATT_KV_HEADS
    nb = L // Q_BLOCK
    qb = q.reshape(bsz, nb, Q_BLOCK, ATT_KV_HEADS, rep, dh).transpose(1, 0, 2, 3, 4, 5)
    scale = HEAD_DIM ** -0.5

    def one_block(qi):
        s = jnp.einsum('bqgrd,bkgd->bgrqk', qi, k).astype(jnp.float32) * scale
        pr = jax.nn.softmax(s, axis=-1).astype(v.dtype)
        return jnp.einsum('bgrqk,bkgd->bqgrd', pr, v)

    out = lax.map(one_block, qb)
    return out.transpose(1, 0, 2, 3, 4, 5).reshape(bsz, L, hq * dh)


def token_tensors(h, p, rope):
    bsz, L, _ = h.shape
    splits = [int(s) for s in np.cumsum(IN_WIDTHS)[:-1]]
    aq, ak, av, rq, rk, rv, rg, z, xbc, dtf, dtb = jnp.split(h @ p['w_in'], splits, axis=-1)
    aq = rmsnorm(aq.reshape(bsz, L, ATT_HEADS, HEAD_DIM), p['q_norm_g'])
    ak = rmsnorm(ak.reshape(bsz, L, ATT_KV_HEADS, HEAD_DIM), p['k_norm_g'])
    av = av.reshape(bsz, L, ATT_KV_HEADS, HEAD_DIM)
    rq = rq.reshape(bsz, L, RET_HEADS, RET_DK)
    rk = rk.reshape(bsz, L, RET_HEADS, RET_DK) * (RET_DK ** -0.5)
    rv = rv.reshape(bsz, L, RET_HEADS, RET_DV)
    if rope is not None:
        cos, sin = rope
        aq, ak, rq, rk = (apply_rope(t, cos, sin) for t in (aq, ak, rq, rk))
    xbc = jax.nn.silu(dwconv_centred(xbc, p['conv_w'], p['conv_b']))
    xs, bs, cs = jnp.split(xbc, [SSD_W, SSD_W + SSD_BC], axis=-1)
    hpg = SSD_HEADS // SSD_GROUPS
    xs = xs.reshape(bsz, L, SSD_HEADS, SSD_HEAD_DIM)
    bs = jnp.repeat(bs.reshape(bsz, L, SSD_GROUPS, SSD_STATE), hpg, axis=2)
    cs = jnp.repeat(cs.reshape(bsz, L, SSD_GROUPS, SSD_STATE), hpg, axis=2)
    dt_f = jax.nn.softplus(dtf.astype(jnp.float32) + p['dt_bias_f'].astype(jnp.float32))
    dt_b = jax.nn.softplus(dtb.astype(jnp.float32) + p['dt_bias_b'].astype(jnp.float32))
    la_f = dt_f * -jnp.exp(p['a_log_f'].astype(jnp.float32))
    la_b = dt_b * -jnp.exp(p['a_log_b'].astype(jnp.float32))
    ret_lf = jnp.broadcast_to(jnp.log1p(-jnp.exp2(p['ret_decay_f'].astype(jnp.float32))), (bsz, L, RET_HEADS))
    ret_lb = jnp.broadcast_to(jnp.log1p(-jnp.exp2(p['ret_decay_b'].astype(jnp.float32))), (bsz, L, RET_HEADS))
    return dict(aq=aq, ak=ak, av=av, rq=rq, rk=rk, rv=rv, rg=rg, ret_lf=ret_lf, ret_lb=ret_lb,
                z=z, xs=xs, cs=cs, k_f=bs * dt_f[..., None].astype(bs.dtype),
                k_b=bs * dt_b[..., None].astype(bs.dtype), la_f=la_f, la_b=la_b)


def mixer_output(att, ret, ssd, t, p):
    bsz, L = att.shape[:2]
    ret = rmsnorm(ret).reshape(bsz, L, RET_W) * jax.nn.silu(t['rg'])
    ssd = (ssd + p['d_skip'][:, None] * t['xs']).reshape(bsz, L, SSD_W)
    ssd = rmsnorm(ssd * jax.nn.silu(t['z']), p['ssd_norm_g'])
    return jnp.concatenate([att, ret, ssd], axis=-1) @ p['w_out']


def swiglu(h, p):
    return (jax.nn.silu(h @ p['w_gate']) * (h @ p['w_up'])) @ p['w_down']


def hybrid_layer(x, xc, c, c_ctx, p, rope, last):
    bsz = x.shape[0]
    mod = jax.nn.silu(c) @ p['w_mod'] + p['b_mod']
    mod_c = jax.nn.silu(c_ctx) @ p['w_mod'] + p['b_mod']
    sh1, sc1, g1, sh2, sc2, g2 = [m[:, None, :] for m in jnp.split(mod, 6, axis=-1)]
    shc1, scc1, gc1, shc2, scc2, gc2 = jnp.split(mod_c, 6, axis=-1)

    h = rmsnorm(x, p['pre_mix_g']) * (1 + sc1) + sh1
    hc = rmsnorm(xc, p['pre_mix_g']) * (1 + scc1) + shc1
    t = token_tensors(h, p, rope)
    tc = token_tensors(hc, p, None)

    zr = jnp.zeros((bsz, RET_HEADS, RET_DK, RET_DV), jnp.float32)
    zs = jnp.zeros((bsz, SSD_HEADS, SSD_STATE, SSD_HEAD_DIM), jnp.float32)
    ret_c, rs_f, rs_b = bidir_scan(tc['rq'], tc['rk'], tc['rk'], tc['rv'], tc['ret_lf'], tc['ret_lb'], zr, zr)
    ssd_c, ss_f, ss_b = bidir_scan(tc['cs'], tc['k_f'], tc['k_b'], tc['xs'], tc['la_f'], tc['la_b'], zs, zs)

    k_all = jnp.concatenate([tc['ak'], t['ak']], axis=1)
    v_all = jnp.concatenate([tc['av'], t['av']], axis=1)
    att = block_attention(t['aq'], k_all, v_all)
    ret, _, _ = bidir_scan(t['rq'], t['rk'], t['rk'], t['rv'], t['ret_lf'], t['ret_lb'], rs_f, rs_b)
    ssd, _, _ = bidir_scan(t['cs'], t['k_f'], t['k_b'], t['xs'], t['la_f'], t['la_b'], ss_f, ss_b)
    m = mixer_output(att, ret, ssd, t, p)
    x = x + g1 * rmsnorm(m, p['post_mix_g'])
    f = swiglu(rmsnorm(x, p['pre_ffn_g']) * (1 + sc2) + sh2, p)
    x = x + g2 * rmsnorm(f, p['post_ffn_g'])

    if not last:
        att_c = block_attention(tc['aq'], tc['ak'], tc['av'])
        mc = mixer_output(att_c, ret_c, ssd_c, tc, p)
        xc = xc + gc1 * rmsnorm(mc, p['post_mix_g'])
        fc = swiglu(rmsnorm(xc, p['pre_ffn_g']) * (1 + scc2) + shc2, p)
        xc = xc + gc2 * rmsnorm(fc, p['post_ffn_g'])
    return x, xc


def setup_inputs(seed: int = 0) -> dict:
    key = jax.random.key(seed)
    ks = jax.random.split(key, 32)
    f32 = jnp.float32
    nrm = lambda k, shape, s: jax.random.normal(k, shape, f32) * s
    L = DEPTH
    lo, hi = math.log(1e-3), math.log(1e-1)
    dt_f = jnp.exp(jax.random.uniform(ks[17], (L, SSD_HEADS), f32) * (hi - lo) + lo)
    dt_b = jnp.exp(jax.random.uniform(ks[18], (L, SSD_HEADS), f32) * (hi - lo) + lo)
    base_decay = -5.0 - jnp.arange(RET_HEADS, dtype=f32)
    return {
        "x": nrm(ks[0], (BATCH, SEQ, D_MODEL), 1.0),
        "c": nrm(ks[1], (BATCH, D_MODEL), 1.0),
        "ctx": nrm(ks[2], (BATCH, CTX_LEN, D_MODEL), 1.0),
        "c_ctx": nrm(ks[3], (D_MODEL,), 1.0),
        "w_mod": nrm(ks[4], (L, D_MODEL, 6 * D_MODEL), 0.5 * D_MODEL ** -0.5),
        "b_mod": nrm(ks[5], (L, 6 * D_MODEL), 0.02),
        "pre_mix_g": 1.0 + nrm(ks[6], (L, D_MODEL), 0.02),
        "post_mix_g": 1.0 + nrm(ks[7], (L, D_MODEL), 0.02),
        "pre_ffn_g": 1.0 + nrm(ks[8], (L, D_MODEL), 0.02),
        "post_ffn_g": 1.0 + nrm(ks[9], (L, D_MODEL), 0.02),
        "w_in": nrm(ks[10], (L, D_MODEL, D_IN), D_MODEL ** -0.5),
        "q_norm_g": 1.0 + nrm(ks[11], (L, HEAD_DIM), 0.02),
        "k_norm_g": 1.0 + nrm(ks[12], (L, HEAD_DIM), 0.02),
        "ret_decay_f": base_decay + nrm(ks[13], (L, RET_HEADS), 0.1),
        "ret_decay_b": base_decay + nrm(ks[14], (L, RET_HEADS), 0.1),
        "conv_w": nrm(ks[15], (L, SSD_CONV, CONV_CH), SSD_CONV ** -0.5),
        "conv_b": nrm(ks[16], (L, CONV_CH), 0.02),
        "dt_bias_f": dt_f + jnp.log(-jnp.expm1(-dt_f)),
        "dt_bias_b": dt_b + jnp.log(-jnp.expm1(-dt_b)),
        "a_log_f": jnp.log(jax.random.uniform(ks[19], (L, SSD_HEADS), f32, 1.0, 16.0)),
        "a_log_b": jnp.log(jax.random.uniform(ks[20], (L, SSD_HEADS), f32, 1.0, 16.0)),
        "d_skip": 1.0 + nrm(ks[21], (L, SSD_HEADS), 0.02),
        "ssd_norm_g": 1.0 + nrm(ks[22], (L, SSD_W), 0.02),
        "w_out": nrm(ks[23], (L, D_MIX, D_MODEL), D_MIX ** -0.5),
        "w_gate": nrm(ks[24], (L, D_MODEL, D_FF), D_MODEL ** -0.5),
        "w_up": nrm(ks[25], (L, D_MODEL, D_FF), D_MODEL ** -0.5),
        "w_down": nrm(ks[26], (L, D_FF, D_MODEL), D_FF ** -0.5),
    }


def reference(x, c, ctx, c_ctx, w_mod, b_mod, pre_mix_g, post_mix_g, pre_ffn_g, post_ffn_g, w_in,
              q_norm_g, k_norm_g, ret_decay_f, ret_decay_b, conv_w, conv_b, dt_bias_f, dt_bias_b,
              a_log_f, a_log_b, d_skip, ssd_norm_g, w_out, w_gate, w_up, w_down):
    n_lat = x.shape[1]
    ROWS = n_lat // GRID_W
    rope = axial_rope(ROWS)
    xc = ctx
    for i in range(DEPTH):
        p = dict(w_mod=w_mod[i], b_mod=b_mod[i], pre_mix_g=pre_mix_g[i], post_mix_g=post_mix_g[i],
                 pre_ffn_g=pre_ffn_g[i], post_ffn_g=post_ffn_g[i], w_in=w_in[i], q_norm_g=q_norm_g[i],
                 k_norm_g=k_norm_g[i], ret_decay_f=ret_decay_f[i], ret_decay_b=ret_decay_b[i],
                 conv_w=conv_w[i], conv_b=conv_b[i], dt_bias_f=dt_bias_f[i], dt_bias_b=dt_bias_b[i],
                 a_log_f=a_log_f[i], a_log_b=a_log_b[i], d_skip=d_skip[i], ssd_norm_g=ssd_norm_g[i],
                 w_out=w_out[i], w_gate=w_gate[i], w_up=w_up[i], w_down=w_down[i])
        x, xc = hybrid_layer(x, xc, c, c_ctx, p, rope, last=(i == DEPTH - 1))
    return x
```

```python
import functools
import math

import numpy as np
import jax
import jax.numpy as jnp
from jax import lax
from jax.experimental import pallas as pl
from jax.experimental.pallas import tpu as pltpu

F32 = jnp.float32
BF16 = jnp.bfloat16

GRID_W = 64
HEAD_DIM = 128
ATT_HEADS = 6
ATT_KV_HEADS = 2
ATT_REP = ATT_HEADS // ATT_KV_HEADS
RET_HEADS = 4
RET_DK = 128
RET_DV = 128
SSD_HEADS = 12
SSD_HEAD_DIM = 64
SSD_GROUPS = 2
SSD_STATE = 128
SSD_CONV = 5
ATT_W = ATT_HEADS * HEAD_DIM
KV_W = ATT_KV_HEADS * HEAD_DIM
RET_QK_W = RET_HEADS * RET_DK
RET_W = RET_HEADS * RET_DV
SSD_W = SSD_HEADS * SSD_HEAD_DIM
SSD_BC = SSD_GROUPS * SSD_STATE
CONV_CH = SSD_W + 2 * SSD_BC
GROUP_W = SSD_W // SSD_GROUPS
HEADS_PER_GROUP = SSD_HEADS // SSD_GROUPS
CHUNK = 128
ROPE_THETA = 10000.0
EPS = 1e-6
LANE = 128

OFF_AQ = 0
OFF_AK = OFF_AQ + ATT_W
OFF_AV = OFF_AK + KV_W
OFF_RQ = OFF_AV + KV_W
OFF_RK = OFF_RQ + RET_QK_W
OFF_RV = OFF_RK + RET_QK_W
OFF_RG = OFF_RV + RET_W
OFF_Z = OFF_RG + RET_W
OFF_XBC = OFF_Z + SSD_W
OFF_DT = OFF_XBC + CONV_CH
D_IN = OFF_DT + 2 * SSD_HEADS
D_IN_PAD = OFF_DT + LANE

VMEM_LIMIT = 56 * 1024 * 1024
N_MOD_ROWS = 16


def _cparams(sem):
    return pltpu.CompilerParams(dimension_semantics=sem, vmem_limit_bytes=VMEM_LIMIT)


def _silu(v):
    return v * jax.nn.sigmoid(v)


def _rms(v):
    return v * lax.rsqrt(jnp.mean(v * v, axis=-1, keepdims=True) + EPS)


def _dot(a, b):
    return jnp.dot(a, b, preferred_element_type=F32)


def _dot_nt(a, b):
    return lax.dot_general(a, b, (((1,), (1,)), ((), ())), preferred_element_type=F32)


def _dot_tn(a, b):
    return lax.dot_general(a, b, (((0,), (0,)), ((), ())), preferred_element_type=F32)


def _mod_kernel(c_ref, w_ref, b_ref, o_ref):
    a = _silu(c_ref[...]).astype(BF16)
    o_ref[...] = _dot(a, w_ref[...].astype(BF16)) + b_ref[...]


def _modulation(c16, w_mod, b_mod):
    depth, d, n = w_mod.shape
    tn = _pick_tile(n, 1024)
    return pl.pallas_call(
        _mod_kernel,
        grid=(depth, n // tn),
        in_specs=[
            pl.BlockSpec((N_MOD_ROWS, d), lambda l, j: (0, 0)),
            pl.BlockSpec((None, d, tn), lambda l, j: (l, 0, j)),
            pl.BlockSpec((None, 1, tn), lambda l, j: (l, 0, j)),
        ],
        out_specs=pl.BlockSpec((None, N_MOD_ROWS, tn), lambda l, j: (l, 0, j)),
        out_shape=jax.ShapeDtypeStruct((depth, N_MOD_ROWS, n), F32),
        compiler_params=_cparams(("parallel", "parallel")),
        name="adaln_modulation",
    )(c16, w_mod, b_mod.reshape(depth, 1, n))


def _in_proj_kernel(use_rope, *refs):
    if use_rope:
        (x_ref, sc_ref, sh_ref, g_ref, w_ref, qg_ref, kg_ref, cos_ref, sin_ref,
         aq_ref, ak_ref, av_ref, rq_ref, rk_ref, rv_ref, rg_ref, z_ref, xbc_ref, dt_ref) = refs
        cos = cos_ref[...]
        sin = sin_ref[...]
    else:
        (x_ref, sc_ref, sh_ref, g_ref, w_ref, qg_ref, kg_ref,
         aq_ref, ak_ref, av_ref, rq_ref, rk_ref, rv_ref, rg_ref, z_ref, xbc_ref, dt_ref) = refs

    x = x_ref[...]
    h = (_rms(x) * g_ref[...] * (1.0 + sc_ref[...]) + sh_ref[...]).astype(BF16)

    def proj(lo, width):
        return _dot(h, w_ref[:, lo:lo + width])

    def rope(t):
        if not use_rope:
            return t
        return t * cos + pltpu.roll(t, HEAD_DIM // 2, 1) * sin

    att_scale = HEAD_DIM ** -0.5
    ret_scale = RET_DK ** -0.5

    acc = proj(OFF_AQ, ATT_W)
    for hd in range(ATT_HEADS):
        t = acc[:, hd * HEAD_DIM:(hd + 1) * HEAD_DIM]
        t = rope(_rms(t) * qg_ref[...]) * att_scale
        aq_ref[:, hd * HEAD_DIM:(hd + 1) * HEAD_DIM] = t.astype(BF16)

    acc = proj(OFF_AK, KV_W)
    for hd in range(ATT_KV_HEADS):
        t = acc[:, hd * HEAD_DIM:(hd + 1) * HEAD_DIM]
        t = rope(_rms(t) * kg_ref[...])
        ak_ref[:, hd * HEAD_DIM:(hd + 1) * HEAD_DIM] = t.astype(BF16)

    av_ref[...] = proj(OFF_AV, KV_W).astype(BF16)

    acc = proj(OFF_RQ, RET_QK_W)
    for hd in range(RET_HEADS):
        t = rope(acc[:, hd * RET_DK:(hd + 1) * RET_DK])
        rq_ref[:, hd * RET_DK:(hd + 1) * RET_DK] = t.astype(BF16)

    acc = proj(OFF_RK, RET_QK_W)
    for hd in range(RET_HEADS):
        t = rope(acc[:, hd * RET_DK:(hd + 1) * RET_DK] * ret_scale)
        rk_ref[:, hd * RET_DK:(hd + 1) * RET_DK] = t.astype(BF16)

    rv_ref[...] = proj(OFF_RV, RET_W).astype(BF16)
    rg_ref[...] = proj(OFF_RG, RET_W).astype(BF16)
    z_ref[...] = proj(OFF_Z, SSD_W).astype(BF16)
    xbc_ref[...] = proj(OFF_XBC, CONV_CH).astype(BF16)
    dt_ref[...] = proj(OFF_DT, LANE)


def _in_proj(x2d, mod4, group_of_tile, pre_g, w_in_p, qg, kg, rope_tabs, tm):
    rows, d = x2d.shape
    nt = rows // tm
    use_rope = rope_tabs is not None
    row_spec = lambda w: pl.BlockSpec((tm, w), lambda i: (i, 0))
    mod_spec = lambda k: pl.BlockSpec((None, None, 1, d), lambda i: (group_of_tile(i), k, 0, 0))
    vec_spec = lambda w: pl.BlockSpec((1, w), lambda i: (0, 0))
    in_specs = [
        row_spec(d), mod_spec(1), mod_spec(0), vec_spec(d),
        pl.BlockSpec((d, D_IN_PAD), lambda i: (0, 0), pipeline_mode=pl.Buffered(1)),
        vec_spec(HEAD_DIM), vec_spec(HEAD_DIM),
    ]
    args = [x2d, mod4, mod4, pre_g, w_in_p, qg, kg]
    if use_rope:
        cos_t, sin_t = rope_tabs
        tiles_per_seq = cos_t.shape[0] // tm
        tab_spec = pl.BlockSpec((tm, HEAD_DIM), lambda i: (i % tiles_per_seq, 0))
        in_specs += [tab_spec, tab_spec]
        args += [cos_t, sin_t]
    widths = (ATT_W, KV_W, KV_W, RET_QK_W, RET_QK_W, RET_W, RET_W, SSD_W, CONV_CH)
    out_specs = [row_spec(w) for w in widths] + [row_spec(LANE)]
    out_shape = [jax.ShapeDtypeStruct((rows, w), BF16) for w in widths]
    out_shape.append(jax.ShapeDtypeStruct((rows, LANE), F32))
    outs = pl.pallas_call(
        functools.partial(_in_proj_kernel, use_rope),
        grid=(nt,),
        in_specs=in_specs,
        out_specs=out_specs,
        out_shape=out_shape,
        compiler_params=_cparams(("parallel",)),
        name="in_proj_rope" if use_rope else "in_proj",
    )(*args)
    names = ("aq", "ak", "av", "rq", "rk", "rv", "rg", "z", "xbc", "dt")
    return dict(zip(names, outs))


def _attn_kernel(n_kv, tq, *refs):
    q_ref = refs[0]
    kv_refs = refs[1:1 + 2 * n_kv]
    o_ref = refs[1 + 2 * n_kv]
    q = q_ref[...]
    qs = jnp.concatenate([q[:, r * HEAD_DIM:(r + 1) * HEAD_DIM] for r in range(ATT_REP)], axis=0)
    scores = [_dot_nt(qs, kv_refs[2 * s][...]) for s in range(n_kv)]
    m = functools.reduce(jnp.maximum, [jnp.max(s, axis=-1, keepdims=True) for s in scores])
    probs = [jnp.exp(s - m) for s in scores]
    denom = functools.reduce(jnp.add, [jnp.sum(p, axis=-1, keepdims=True) for p in probs])
    acc = functools.reduce(jnp.add, [_dot(p.astype(BF16), kv_refs[2 * s + 1][...]) for s, p in enumerate(probs)])
    out = acc / denom
    for r in range(ATT_REP):
        o_ref[:, r * HEAD_DIM:(r + 1) * HEAD_DIM] = out[r * tq:(r + 1) * tq].astype(BF16)


def _attention(q2d, kv_list, batch, tq):
    rows = q2d.shape[0]
    lq = rows // batch
    nq = lq // tq
    gw = ATT_REP * HEAD_DIM
    in_specs = [pl.BlockSpec((tq, gw), lambda b, g, i: (b * nq + i, g))]
    args = [q2d]
    for k2d, v2d in kv_list:
        lk = k2d.shape[0] // batch
        spec = pl.BlockSpec((lk, HEAD_DIM), lambda b, g, i: (b, g))
        in_specs += [spec, spec]
        args += [k2d, v2d]
    return pl.pallas_call(
        functools.partial(_attn_kernel, len(kv_list), tq),
        grid=(batch, ATT_KV_HEADS, nq),
        in_specs=in_specs,
        out_specs=pl.BlockSpec((tq, gw), lambda b, g, i: (b * nq + i, g)),
        out_shape=jax.ShapeDtypeStruct((rows, ATT_W), BF16),
        compiler_params=_cparams(("parallel", "parallel", "parallel")),
        name="attention",
    )(*args)


def _ret_kernel(n_ctx_chunks, n_lat_chunks, ctx_out, *refs):
    (qc_ref, kc_ref, vc_ref, gc_ref, ql_ref, kl_ref, vl_ref, gl_ref, df_ref, db_ref) = refs[:10]
    if ctx_out:
        oc_ref, ol_ref, yc_s, yl_s, sf_s, sb_s = refs[10:]
    else:
        ol_ref, yc_s, yl_s, sf_s, sb_s = refs[10:]
        oc_ref = None

    lam_f = jnp.log1p(-jnp.exp2(df_ref[...]))
    lam_b = jnp.log1p(-jnp.exp2(db_ref[...]))
    ii = lax.broadcasted_iota(jnp.int32, (CHUNK, CHUNK), 0)
    jj = lax.broadcasted_iota(jnp.int32, (CHUNK, CHUNK), 1)
    dist = (ii - jj).astype(F32)
    rowi = ii.astype(F32)
    w_intra = (jnp.where(jj <= ii, jnp.exp(dist * lam_f), 0.0)
               + jnp.where(jj >= ii, jnp.exp(-dist * lam_b), 0.0))
    inter_f = jnp.exp((rowi + 1.0) * lam_f)
    state_f = jnp.exp((CHUNK - 1.0 - rowi) * lam_f)
    inter_b = jnp.exp((CHUNK - rowi) * lam_b)
    state_b = jnp.exp(rowi * lam_b)
    dec_f = jnp.exp(CHUNK * lam_f)
    dec_b = jnp.exp(CHUNK * lam_b)

    sf_s[...] = jnp.zeros_like(sf_s)
    sb_s[...] = jnp.zeros_like(sb_s)

    def fwd_chunk(q_ref, k_ref, v_ref, y_ref, r0):
        q = q_ref[pl.ds(r0, CHUNK), :]
        k = k_ref[pl.ds(r0, CHUNK), :]
        v = v_ref[pl.ds(r0, CHUNK), :]
        s = _dot_nt(q, k) * w_intra
        sf = sf_s[...]
        y = _dot(s.astype(BF16), v) + inter_f * _dot(q, sf.astype(BF16))
        y_ref[pl.ds(r0, CHUNK), :] = y
        vw = (v.astype(F32) * state_f).astype(BF16)
        sf_s[...] = sf * dec_f + _dot_tn(k, vw)

    def bwd_chunk(q_ref, k_ref, v_ref, g_ref, y_ref, o_ref, r0):
        q = q_ref[pl.ds(r0, CHUNK), :]
        k = k_ref[pl.ds(r0, CHUNK), :]
        v = v_ref[pl.ds(r0, CHUNK), :]
        sb = sb_s[...]
        if o_ref is not None:
            y = y_ref[pl.ds(r0, CHUNK), :] + inter_b * _dot(q, sb.astype(BF16))
            gate = _silu(g_ref[pl.ds(r0, CHUNK), :].astype(F32))
            o_ref[pl.ds(r0, CHUNK), :] = (_rms(y) * gate).astype(BF16)
        vw = (v.astype(F32) * state_b).astype(BF16)
        sb_s[...] = sb * dec_b + _dot_tn(k, vw)

    for c in range(n_ctx_chunks):
        fwd_chunk(qc_ref, kc_ref, vc_ref, yc_s, c * CHUNK)

    def lat_fwd(c, carry):
        fwd_chunk(ql_ref, kl_ref, vl_ref, yl_s, pl.multiple_of(c * CHUNK, CHUNK))
        return carry

    lax.fori_loop(0, n_lat_chunks, lat_fwd, 0)

    for c in reversed(range(n_ctx_chunks)):
        bwd_chunk(qc_ref, kc_ref, vc_ref, gc_ref, yc_s, oc_ref, c * CHUNK)

    def lat_bwd(t, carry):
        c = n_lat_chunks - 1 - t
        bwd_chunk(ql_ref, kl_ref, vl_ref, gl_ref, yl_s, ol_ref, pl.multiple_of(c * CHUNK, CHUNK))
        return carry

    lax.fori_loop(0, n_lat_chunks, lat_bwd, 0)


def _retention(tc, tl, decay_f, decay_b, batch, ctx_out):
    lc = tc["rq"].shape[0] // batch
    ll = tl["rq"].shape[0] // batch
    cspec = pl.BlockSpec((lc, RET_DK), lambda b, h: (b, h))
    lspec = pl.BlockSpec((ll, RET_DK), lambda b, h: (b, h))
    pspec = pl.BlockSpec((None, 1, LANE), lambda b, h: (h, 0, 0))
    out_specs = [lspec]
    out_shape = [jax.ShapeDtypeStruct((batch * ll, RET_W), BF16)]
    if ctx_out:
        out_specs = [cspec] + out_specs
        out_shape = [jax.ShapeDtypeStruct((batch * lc, RET_W), BF16)] + out_shape
    bcast = lambda p: jnp.broadcast_to(p.astype(F32)[:, None, None], (RET_HEADS, 1, LANE))
    outs = pl.pallas_call(
        functools.partial(_ret_kernel, lc // CHUNK, ll // CHUNK, ctx_out),
        grid=(batch, RET_HEADS),
        in_specs=[cspec] * 4 + [lspec] * 4 + [pspec, pspec],
        out_specs=out_specs,
        out_shape=out_shape,
        scratch_shapes=[
            pltpu.VMEM((lc, RET_DV), F32), pltpu.VMEM((ll, RET_DV), F32),
            pltpu.VMEM((RET_DK, RET_DV), F32), pltpu.VMEM((RET_DK, RET_DV), F32),
        ],
        compiler_params=_cparams(("parallel", "parallel")),
        name="retention",
    )(tc["rq"], tc["rk"], tc["rv"], tc["rg"], tl["rq"], tl["rk"], tl["rv"], tl["rg"],
      bcast(decay_f), bcast(decay_b))
    return (outs[0], outs[1]) if ctx_out else (None, outs[0])


def _expand_heads(v, e_ref):
    hi = v.astype(BF16)
    lo = (v - hi.astype(F32)).astype(BF16)
    e = e_ref[...]
    return _dot(hi, e) + _dot(lo, e)


def _cumsum_rows(v):
    rows = lax.broadcasted_iota(jnp.int32, v.shape, 0)
    s = 1
    while s < v.shape[0]:
        v = v + jnp.where(rows >= s, pltpu.roll(v, s, 0), 0.0)
        s *= 2
    return v


def _ssd_kernel(n_ctx_chunks, n_lat_chunks, ctx_out, *refs):
    (xc_ref, dtc_ref, zc_ref, xl_ref, dtl_ref, zl_ref,
     cw_ref, cb_ref, bias_ref, alog_ref, skip_ref, ng_ref, ef_ref, eb_ref) = refs[:14]
    if ctx_out:
        oc_ref, ol_ref, tok_s, y_s, sf_s, sb_s = refs[14:]
    else:
        ol_ref, tok_s, y_s, sf_s, sb_s = refs[14:]
        oc_ref = None
    lc = n_ctx_chunks * CHUNK

    ii = lax.broadcasted_iota(jnp.int32, (CHUNK, CHUNK), 0)
    jj = lax.broadcasted_iota(jnp.int32, (CHUNK, CHUNK), 1)
    lower = jj < ii
    upper = jj > ii
    neg_a = -jnp.exp(alog_ref[...])
    halo = 8

    def conv_silu(x_ref, c, n_chunks):
        if isinstance(c, int):
            r0 = c * CHUNK
            prev0 = max(r0 - halo, 0)
            next0 = min(r0 + CHUNK, (n_chunks - 1) * CHUNK)
            first = c == 0
            last = c == n_chunks - 1
        else:
            r0 = pl.multiple_of(c * CHUNK, CHUNK)
            prev0 = pl.multiple_of(jnp.maximum(r0 - halo, 0), halo)
            next0 = pl.multiple_of(jnp.minimum(r0 + CHUNK, (n_chunks - 1) * CHUNK), halo)
            first = c == 0
            last = c == n_chunks - 1
        centre = x_ref[pl.ds(r0, CHUNK), :].astype(F32)
        prev = x_ref[pl.ds(prev0, halo), :].astype(F32)
        nxt = x_ref[pl.ds(next0, halo), :].astype(F32)
        prev = jnp.where(first, 0.0, prev)
        nxt = jnp.where(last, 0.0, nxt)
        ext = jnp.concatenate([prev, centre, nxt], axis=0)
        acc = cb_ref[...] + cw_ref[0:1, :] * ext[halo - 2:halo - 2 + CHUNK]
        for k in range(1, SSD_CONV):
            acc = acc + cw_ref[k:k + 1, :] * ext[halo - 2 + k:halo - 2 + k + CHUNK]
        return _silu(acc)

    def decays(dt_ref, r0):
        dt = jax.nn.softplus(dt_ref[pl.ds(r0, CHUNK), :] + bias_ref[...])
        la = dt * neg_a
        a_inc = _cumsum_rows(la)
        total = a_inc[CHUNK - 1:CHUNK, :]
        return dt, la, a_inc, total

    def decay_row(total, e_ref):
        return _expand_heads(jnp.broadcast_to(jnp.exp(total), (8, LANE)), e_ref)[0:1, :]

    def fwd_chunk(x_ref, dt_ref, c, n_chunks, base):
        if isinstance(c, int):
            r0 = c * CHUNK
            t0 = base + r0
        else:
            r0 = pl.multiple_of(c * CHUNK, CHUNK)
            t0 = pl.multiple_of(base + r0, CHUNK)
        tok = conv_silu(x_ref, c, n_chunks)
        tok_b = tok.astype(BF16)
        tok_s[pl.ds(t0, CHUNK), :] = tok_b
        xs = tok_b[:, :SSD_W]
        dt, la, a_inc, total = decays(dt_ref, r0)
        a_exc = a_inc - la
        a_inc_t = a_inc.T
        a_exc_t = a_exc.T
        dt_t = dt.T
        ys = []
        inter = []
        for g in range(SSD_GROUPS):
            bs = tok_b[:, SSD_W + g * SSD_STATE:SSD_W + (g + 1) * SSD_STATE]
            cs = tok_b[:, SSD_W + SSD_BC + g * SSD_STATE:SSD_W + SSD_BC + (g + 1) * SSD_STATE]
            gmat = _dot_nt(cs, bs)
            for pair in range(HEADS_PER_GROUP // 2):
                ms = []
                for sub in range(2):
                    hd = g * HEADS_PER_GROUP + 2 * pair + sub
                    hb = SSD_HEADS + hd
                    expo = jnp.where(lower, a_inc[:, hd:hd + 1] - a_inc_t[hd:hd + 1, :],
                                     jnp.where(upper, a_exc_t[hb:hb + 1, :] - a_exc[:, hb:hb + 1], 0.0))
                    dsel = jnp.where(lower, dt_t[hd:hd + 1, :],
                                     jnp.where(upper, dt_t[hb:hb + 1, :], dt_t[hd:hd + 1, :] + dt_t[hb:hb + 1, :]))
                    ms.append((gmat * jnp.exp(expo) * dsel).astype(BF16))
                col = (g * HEADS_PER_GROUP + 2 * pair) * SSD_HEAD_DIM
                xpair = xs[:, col:col + 2 * SSD_HEAD_DIM]
                lane = lax.broadcasted_iota(jnp.int32, xpair.shape, 1)
                zero = jnp.zeros_like(xpair)
                rhs = jnp.concatenate([jnp.where(lane < SSD_HEAD_DIM, xpair, zero),
                                       jnp.where(lane < SSD_HEAD_DIM, zero, xpair)], axis=0)
                ys.append(_dot(jnp.concatenate(ms, axis=1), rhs))
            inter.append(_dot(cs, sf_s[:, g * GROUP_W:(g + 1) * GROUP_W].astype(BF16)))
        y = jnp.concatenate(ys, axis=1) + _expand_heads(jnp.exp(a_inc), ef_ref) * jnp.concatenate(inter, axis=1)
        y_s[pl.ds(t0, CHUNK), :] = y
        xw = (xs.astype(F32) * _expand_heads(jnp.exp(total - a_inc) * dt, ef_ref)).astype(BF16)
        dec = decay_row(total, ef_ref)
        for g in range(SSD_GROUPS):
            bs = tok_b[:, SSD_W + g * SSD_STATE:SSD_W + (g + 1) * SSD_STATE]
            sl = slice(g * GROUP_W, (g + 1) * GROUP_W)
            sf_s[:, sl] = sf_s[:, sl] * dec[:, sl] + _dot_tn(bs, xw[:, sl])

    def bwd_chunk(dt_ref, z_ref, o_ref, c, base):
        if isinstance(c, int):
            r0 = c * CHUNK
            t0 = base + r0
        else:
            r0 = pl.multiple_of(c * CHUNK, CHUNK)
            t0 = pl.multiple_of(base + r0, CHUNK)
        tok_b = tok_s[pl.ds(t0, CHUNK), :]
        xs = tok_b[:, :SSD_W].astype(F32)
        dt, la, a_inc, total = decays(dt_ref, r0)
        a_exc = a_inc - la
        if o_ref is not None:
            inter = []
            for g in range(SSD_GROUPS):
                cs = tok_b[:, SSD_W + SSD_BC + g * SSD_STATE:SSD_W + SSD_BC + (g + 1) * SSD_STATE]
                inter.append(_dot(cs, sb_s[:, g * GROUP_W:(g + 1) * GROUP_W].astype(BF16)))
            y = y_s[pl.ds(t0, CHUNK), :] + _expand_heads(jnp.exp(total - a_exc), eb_ref) * jnp.concatenate(inter, axis=1)
            y = (y + skip_ref[...] * xs) * _silu(z_ref[pl.ds(r0, CHUNK), :].astype(F32))
            o_ref[pl.ds(r0, CHUNK), :] = (_rms(y) * ng_ref[...]).astype(BF16)
        xw = (xs * _expand_heads(jnp.exp(a_exc) * dt, eb_ref)).astype(BF16)
        dec = decay_row(total, eb_ref)
        for g in range(SSD_GROUPS):
            bs = tok_b[:, SSD_W + g * SSD_STATE:SSD_W + (g + 1) * SSD_STATE]
            sl = slice(g * GROUP_W, (g + 1) * GROUP_W)
            sb_s[:, sl] = sb_s[:, sl] * dec[:, sl] + _dot_tn(bs, xw[:, sl])

    sf_s[...] = jnp.zeros_like(sf_s)
    sb_s[...] = jnp.zeros_like(sb_s)

    for c in range(n_ctx_chunks):
        fwd_chunk(xc_ref, dtc_ref, c, n_ctx_chunks, 0)

    def lat_fwd(c, carry):
        fwd_chunk(xl_ref, dtl_ref, c, n_lat_chunks, lc)
        return carry

    lax.fori_loop(0, n_lat_chunks, lat_fwd, 0)

    for c in reversed(range(n_ctx_chunks)):
        bwd_chunk(dtc_ref, zc_ref, oc_ref, c, 0)

    def lat_bwd(t, carry):
        bwd_chunk(dtl_ref, zl_ref, ol_ref, n_lat_chunks - 1 - t, lc)
        return carry

    lax.fori_loop(0, n_lat_chunks, lat_bwd, 0)


def _head_lane_vec(f, b):
    v = jnp.zeros((1, LANE), F32)
    v = v.at[0, :SSD_HEADS].set(f.astype(F32))
    return v.at[0, SSD_HEADS:2 * SSD_HEADS].set(b.astype(F32))


def _ssd(tc, tl, p, batch, ctx_out):
    lc = tc["xbc"].shape[0] // batch
    ll = tl["xbc"].shape[0] // batch
    rows = lambda n, w: pl.BlockSpec((n, w), lambda b: (b, 0))
    vec = lambda w: pl.BlockSpec((1, w), lambda b: (0, 0))
    whole = lambda a: pl.BlockSpec(a.shape, lambda b: (0,) * a.ndim)
    head_of_col = np.arange(SSD_W) // SSD_HEAD_DIM
    ef = jnp.asarray(np.arange(LANE)[:, None] == head_of_col[None, :], BF16)
    eb = jnp.asarray(np.arange(LANE)[:, None] == head_of_col[None, :] + SSD_HEADS, BF16)
    conv_w = p["conv_w"].astype(F32)
    conv_b = p["conv_b"].astype(F32).reshape(1, CONV_CH)
    bias = _head_lane_vec(p["dt_bias_f"], p["dt_bias_b"])
    alog = _head_lane_vec(p["a_log_f"], p["a_log_b"])
    skip = jnp.repeat(p["d_skip"].astype(F32), SSD_HEAD_DIM).reshape(1, SSD_W)
    norm_g = p["ssd_norm_g"].astype(F32).reshape(1, SSD_W)
    out_specs = [rows(ll, SSD_W)]
    out_shape = [jax.ShapeDtypeStruct((batch * ll, SSD_W), BF16)]
    if ctx_out:
        out_specs = [rows(lc, SSD_W)] + out_specs
        out_shape = [jax.ShapeDtypeStruct((batch * lc, SSD_W), BF16)] + out_shape
    outs = pl.pallas_call(
        functools.partial(_ssd_kernel, lc // CHUNK, ll // CHUNK, ctx_out),
        grid=(batch,),
        in_specs=[rows(lc, CONV_CH), rows(lc, LANE), rows(lc, SSD_W),
                  rows(ll, CONV_CH), rows(ll, LANE), rows(ll, SSD_W),
                  whole(conv_w), vec(CONV_CH), vec(LANE), vec(LANE), vec(SSD_W), vec(SSD_W),
                  whole(ef), whole(eb)],
        out_specs=out_specs,
        out_shape=out_shape,
        scratch_shapes=[
            pltpu.VMEM((lc + ll, CONV_CH), BF16), pltpu.VMEM((lc + ll, SSD_W), F32),
            pltpu.VMEM((SSD_STATE, SSD_W), F32), pltpu.VMEM((SSD_STATE, SSD_W), F32),
        ],
        compiler_params=_cparams(("parallel",)),
        name="ssd_scan",
    )(tc["xbc"], tc["dt"], tc["z"], tl["xbc"], tl["dt"], tl["z"],
      conv_w, conv_b, bias, alog, skip, norm_g, ef, eb)
    return (outs[0], outs[1]) if ctx_out else (None, outs[0])


def _out_proj_kernel(att_ref, ret_ref, ssd_ref, w_ref, x_ref, g1_ref, pmg_ref, pfg_ref, sc_ref, sh_ref,
                     x1_ref, hf_ref):
    m = (_dot(att_ref[...], w_ref[0:ATT_W, :])
         + _dot(ret_ref[...], w_ref[ATT_W:ATT_W + RET_W, :])
         + _dot(ssd_ref[...], w_ref[ATT_W + RET_W:, :]))
    x1 = x_ref[...] + g1_ref[...] * (_rms(m) * pmg_ref[...])
    x1_ref[...] = x1
    hf_ref[...] = (_rms(x1) * pfg_ref[...] * (1.0 + sc_ref[...]) + sh_ref[...]).astype(BF16)


def _out_proj(att, ret, ssd, w_out_b, x2d, mod4, group_of_tile, post_mix_g, pre_ffn_g, tm):
    rows, d = x2d.shape
    row_spec = lambda w: pl.BlockSpec((tm, w), lambda i: (i, 0))
    mod_spec = lambda k: pl.BlockSpec((None, None, 1, d), lambda i: (group_of_tile(i), k, 0, 0))
    vec_spec = pl.BlockSpec((1, d), lambda i: (0, 0))
    return pl.pallas_call(
        _out_proj_kernel,
        grid=(rows // tm,),
        in_specs=[row_spec(ATT_W), row_spec(RET_W), row_spec(SSD_W),
                  pl.BlockSpec(w_out_b.shape, lambda i: (0, 0)),
                  row_spec(d), mod_spec(2), vec_spec, vec_spec, mod_spec(4), mod_spec(3)],
        out_specs=[row_spec(d), row_spec(d)],
        out_shape=[jax.ShapeDtypeStruct((rows, d), F32), jax.ShapeDtypeStruct((rows, d), BF16)],
        compiler_params=_cparams(("parallel",)),
        name="out_proj",
    )(att, ret, ssd, w_out_b, x2d, mod4, post_mix_g, pre_ffn_g, mod4, mod4)


def _ffn_kernel(hf_ref, wg_ref, wu_ref, wd_ref, x1_ref, g2_ref, pg_ref, o_ref, acc_ref):
    j = pl.program_id(1)

    @pl.when(j == 0)
    def _():
        acc_ref[...] = jnp.zeros_like(acc_ref)

    hf = hf_ref[...]
    a = _silu(_dot(hf, wg_ref[...])) * _dot(hf, wu_ref[...])
    acc_ref[...] += _dot(a.astype(BF16), wd_ref[...])

    @pl.when(j == pl.num_programs(1) - 1)
    def _():
        o_ref[...] = x1_ref[...] + g2_ref[...] * (_rms(acc_ref[...]) * pg_ref[...])


def _ffn(hf, wg, wu, wd, x1, mod4, group_of_tile, post_ffn_g, tm, tf):
    rows, d = x1.shape
    dff = wg.shape[1]
    return pl.pallas_call(
        _ffn_kernel,
        grid=(rows // tm, dff // tf),
        in_specs=[
            pl.BlockSpec((tm, d), lambda i, j: (i, 0)),
            pl.BlockSpec((d, tf), lambda i, j: (0, j)),
            pl.BlockSpec((d, tf), lambda i, j: (0, j)),
            pl.BlockSpec((tf, d), lambda i, j: (j, 0)),
            pl.BlockSpec((tm, d), lambda i, j: (i, 0)),
            pl.BlockSpec((None, None, 1, d), lambda i, j: (group_of_tile(i), 5, 0, 0)),
            pl.BlockSpec((1, d), lambda i, j: (0, 0)),
        ],
        out_specs=pl.BlockSpec((tm, d), lambda i, j: (i, 0)),
        out_shape=jax.ShapeDtypeStruct((rows, d), F32),
        scratch_shapes=[pltpu.VMEM((tm, d), F32)],
        compiler_params=_cparams(("parallel", "arbitrary")),
        name="swiglu_ffn",
    )(hf, wg, wu, wd, x1, mod4, post_ffn_g)


def _rope_tables(seq_len):
    rows = seq_len // GRID_W
    row = jnp.repeat(jnp.arange(rows, dtype=F32), GRID_W)
    col = jnp.tile(jnp.arange(GRID_W, dtype=F32), rows)
    n_freq = HEAD_DIM // 4
    inv = ROPE_THETA ** (-jnp.arange(n_freq, dtype=F32) / n_freq)
    ang = jnp.concatenate([row[:, None] * inv, col[:, None] * inv], axis=-1)
    cos, sin = jnp.cos(ang), jnp.sin(ang)
    return jnp.concatenate([cos, cos], axis=-1), jnp.concatenate([-sin, sin], axis=-1)


def _pick_tile(n, target):
    t = min(n, target)
    while n % t:
        t //= 2
    return t


def _layer_weights(w_in, q_norm_g, k_norm_g, w_out, w_gate, w_up, w_down):
    perm = np.concatenate([np.arange(0, HEAD_DIM, 2), np.arange(1, HEAD_DIM, 2)])
    cols = np.arange(D_IN)
    for off, heads in ((OFF_AQ, ATT_HEADS), (OFF_AK, ATT_KV_HEADS), (OFF_RQ, RET_HEADS), (OFF_RK, RET_HEADS)):
        for hd in range(heads):
            cols[off + hd * HEAD_DIM:off + (hd + 1) * HEAD_DIM] = off + hd * HEAD_DIM + perm
    w_in_p = jnp.pad(w_in[:, cols].astype(BF16), ((0, 0), (0, D_IN_PAD - D_IN)))
    qg = q_norm_g.astype(F32)[perm].reshape(1, HEAD_DIM)
    kg = k_norm_g.astype(F32)[perm].reshape(1, HEAD_DIM)
    return w_in_p, qg, kg, w_out.astype(BF16), w_gate.astype(BF16), w_up.astype(BF16), w_down.astype(BF16)


def kernel(x, c, ctx, c_ctx, w_mod, b_mod, pre_mix_g, post_mix_g, pre_ffn_g, post_ffn_g, w_in, q_norm_g, k_norm_g, ret_decay_f, ret_decay_b, conv_w, conv_b, dt_bias_f, dt_bias_b, a_log_f, a_log_b, d_skip, ssd_norm_g, w_out, w_gate, w_up, w_down):
    batch, seq, d = x.shape
    lc = ctx.shape[1]
    depth = w_mod.shape[0]
    assert batch < N_MOD_ROWS and seq % CHUNK == 0 and lc % CHUNK == 0 and seq % GRID_W == 0

    c16 = jnp.zeros((N_MOD_ROWS, d), F32).at[:batch].set(c.astype(F32)).at[batch].set(c_ctx.astype(F32))
    mod_all = _modulation(c16, w_mod.astype(F32), b_mod.astype(F32))
    rope_tabs = _rope_tables(seq)

    tm_lat = _pick_tile(seq, 512)
    tm_ctx = _pick_tile(batch * lc, 512)
    tiles_per_batch = seq // tm_lat
    lat_group = lambda i: i // tiles_per_batch
    ctx_group = lambda i: batch
    tq_lat = _pick_tile(seq, 256)
    tq_ctx = _pick_tile(lc, 256)
    tf = _pick_tile(w_gate.shape[2], 512)

    xl = x.reshape(batch * seq, d).astype(F32)
    xc = ctx.reshape(batch * lc, d).astype(F32)
    vec = lambda v: v.astype(F32).reshape(1, d)

    for l in range(depth):
        last = l == depth - 1
        w_in_p, qg, kg, w_out_b, wg, wu, wd = _layer_weights(
            w_in[l], q_norm_g[l], k_norm_g[l], w_out[l], w_gate[l], w_up[l], w_down[l])
        mod4 = mod_all[l].reshape(N_MOD_ROWS, 6, 1, d)
        p = dict(conv_w=conv_w[l], conv_b=conv_b[l], dt_bias_f=dt_bias_f[l], dt_bias_b=dt_bias_b[l],
                 a_log_f=a_log_f[l], a_log_b=a_log_b[l], d_skip=d_skip[l], ssd_norm_g=ssd_norm_g[l])

        tl = _in_proj(xl, mod4, lat_group, vec(pre_mix_g[l]), w_in_p, qg, kg, rope_tabs, tm_lat)
        tc = _in_proj(xc, mod4, ctx_group, vec(pre_mix_g[l]), w_in_p, qg, kg, None, tm_ctx)

        att_l = _attention(tl["aq"], [(tc["ak"], tc["av"]), (tl["ak"], tl["av"])], batch, tq_lat)
        ret_c, ret_l = _retention(tc, tl, ret_decay_f[l], ret_decay_b[l], batch, not last)
        ssd_c, ssd_l = _ssd(tc, tl, p, batch, not last)

        x1, hf = _out_proj(att_l, ret_l, ssd_l, w_out_b, xl, mod4, lat_group,
                           vec(post_mix_g[l]), vec(pre_ffn_g[l]), tm_lat)
        xl = _ffn(hf, wg, wu, wd, x1, mod4, lat_group, vec(post_ffn_g[l]), tm_lat, tf)

        if not last:
            att_c = _attention(tc["aq"], [(tc["ak"], tc["av"])], batch, tq_ctx)
            x1c, hfc = _out_proj(att_c, ret_c, ssd_c, w_out_b, xc, mod4, ctx_group,
                                 vec(post_mix_g[l]), vec(pre_ffn_g[l]), tm_ctx)
            xc = _ffn(hfc, wg, wu, wd, x1c, mod4, ctx_group, vec(post_ffn_g[l]), tm_ctx, tf)

    return xl.reshape(batch, seq, d).astype(x.dtype)
```

```python
import functools
import math

import numpy as np
import jax
import jax.numpy as jnp
from jax import lax
from jax.experimental import pallas as pl
from jax.experimental.pallas import tpu as pltpu

F32 = jnp.float32
BF16 = jnp.bfloat16

GRID_W = 64
HEAD_DIM = 128
ATT_HEADS = 6
ATT_KV_HEADS = 2
ATT_REP = ATT_HEADS // ATT_KV_HEADS
RET_HEADS = 4
RET_DK = 128
RET_DV = 128
SSD_HEADS = 12
SSD_HEAD_DIM = 64
SSD_GROUPS = 2
SSD_STATE = 128
SSD_CONV = 5
ATT_W = ATT_HEADS * HEAD_DIM
KV_W = ATT_KV_HEADS * HEAD_DIM
RET_QK_W = RET_HEADS * RET_DK
RET_W = RET_HEADS * RET_DV
SSD_W = SSD_HEADS * SSD_HEAD_DIM
SSD_BC = SSD_GROUPS * SSD_STATE
CONV_CH = SSD_W + 2 * SSD_BC
GROUP_W = SSD_W // SSD_GROUPS
HEADS_PER_GROUP = SSD_HEADS // SSD_GROUPS
CHUNK = 128
ROPE_THETA = 10000.0
EPS = 1e-6
LANE = 128
SCAN_UNROLL = 2
CONV_HALO = 16
CONV_EXT = 256

OFF_AQ = 0
OFF_AK = OFF_AQ + ATT_W
OFF_AV = OFF_AK + KV_W
OFF_RQ = OFF_AV + KV_W
OFF_RK = OFF_RQ + RET_QK_W
OFF_RV = OFF_RK + RET_QK_W
OFF_RG = OFF_RV + RET_W
OFF_Z = OFF_RG + RET_W
OFF_XBC = OFF_Z + SSD_W
OFF_DT = OFF_XBC + CONV_CH
D_IN = OFF_DT + 2 * SSD_HEADS
D_IN_PAD = OFF_DT + LANE

VMEM_LIMIT = 56 * 1024 * 1024
N_MOD_ROWS = 16


def _cparams(sem):
    return pltpu.CompilerParams(dimension_semantics=sem, vmem_limit_bytes=VMEM_LIMIT)


def _silu(v):
    return v * jax.nn.sigmoid(v)


def _rms(v):
    return v * lax.rsqrt(jnp.mean(v * v, axis=-1, keepdims=True) + EPS)


def _dot(a, b):
    return jnp.dot(a, b, preferred_element_type=F32)


def _dot_nt(a, b):
    return lax.dot_general(a, b, (((1,), (1,)), ((), ())), preferred_element_type=F32)


def _dot_tn(a, b):
    return lax.dot_general(a, b, (((0,), (0,)), ((), ())), preferred_element_type=F32)


def _mod_kernel(c_ref, w_ref, b_ref, o_ref):
    a = _silu(c_ref[...]).astype(BF16)
    o_ref[...] = _dot(a, w_ref[...].astype(BF16)) + b_ref[...]


def _modulation(c16, w_mod, b_mod):
    depth, d, n = w_mod.shape
    tn = _pick_tile(n, 1024)
    return pl.pallas_call(
        _mod_kernel,
        grid=(depth, n // tn),
        in_specs=[
            pl.BlockSpec((N_MOD_ROWS, d), lambda l, j: (0, 0)),
            pl.BlockSpec((None, d, tn), lambda l, j: (l, 0, j)),
            pl.BlockSpec((None, 1, tn), lambda l, j: (l, 0, j)),
        ],
        out_specs=pl.BlockSpec((None, N_MOD_ROWS, tn), lambda l, j: (l, 0, j)),
        out_shape=jax.ShapeDtypeStruct((depth, N_MOD_ROWS, n), F32),
        compiler_params=_cparams(("parallel", "parallel")),
        name="adaln_modulation",
    )(c16, w_mod, b_mod.reshape(depth, 1, n))


def _in_proj_kernel(use_rope, *refs):
    if use_rope:
        (x_ref, sc_ref, sh_ref, g_ref, w_ref, qg_ref, kg_ref, cos_ref, sin_ref,
         aq_ref, ak_ref, av_ref, rq_ref, rk_ref, rv_ref, rg_ref, z_ref, xbc_ref, dt_ref) = refs
        cos = cos_ref[...]
        sin = sin_ref[...]
    else:
        (x_ref, sc_ref, sh_ref, g_ref, w_ref, qg_ref, kg_ref,
         aq_ref, ak_ref, av_ref, rq_ref, rk_ref, rv_ref, rg_ref, z_ref, xbc_ref, dt_ref) = refs

    x = x_ref[...]
    h = (_rms(x) * g_ref[...] * (1.0 + sc_ref[...]) + sh_ref[...]).astype(BF16)

    def proj(lo, width):
        return _dot(h, w_ref[:, lo:lo + width])

    def rope(t):
        if not use_rope:
            return t
        return t * cos + pltpu.roll(t, HEAD_DIM // 2, 1) * sin

    att_scale = HEAD_DIM ** -0.5 * math.log2(math.e)
    ret_scale = RET_DK ** -0.5

    acc = proj(OFF_AQ, ATT_W)
    for hd in range(ATT_HEADS):
        t = acc[:, hd * HEAD_DIM:(hd + 1) * HEAD_DIM]
        t = rope(_rms(t) * qg_ref[...]) * att_scale
        aq_ref[:, hd * HEAD_DIM:(hd + 1) * HEAD_DIM] = t.astype(BF16)

    acc = proj(OFF_AK, KV_W)
    for hd in range(ATT_KV_HEADS):
        t = acc[:, hd * HEAD_DIM:(hd + 1) * HEAD_DIM]
        t = rope(_rms(t) * kg_ref[...])
        ak_ref[:, hd * HEAD_DIM:(hd + 1) * HEAD_DIM] = t.astype(BF16)

    av_ref[...] = proj(OFF_AV, KV_W).astype(BF16)

    acc = proj(OFF_RQ, RET_QK_W)
    for hd in range(RET_HEADS):
        t = rope(acc[:, hd * RET_DK:(hd + 1) * RET_DK])
        rq_ref[:, hd * RET_DK:(hd + 1) * RET_DK] = t.astype(BF16)

    acc = proj(OFF_RK, RET_QK_W)
    for hd in range(RET_HEADS):
        t = rope(acc[:, hd * RET_DK:(hd + 1) * RET_DK] * ret_scale)
        rk_ref[:, hd * RET_DK:(hd + 1) * RET_DK] = t.astype(BF16)

    rv_ref[...] = proj(OFF_RV, RET_W).astype(BF16)
    rg_ref[...] = proj(OFF_RG, RET_W).astype(BF16)
    z_ref[...] = proj(OFF_Z, SSD_W).astype(BF16)
    xbc_ref[...] = proj(OFF_XBC, CONV_CH).astype(BF16)
    dt_ref[...] = proj(OFF_DT, LANE)


def _in_proj(x2d, mod4, group_of_tile, pre_g, w_in_p, qg, kg, rope_tabs, tm):
    rows, d = x2d.shape
    nt = rows // tm
    use_rope = rope_tabs is not None
    row_spec = lambda w: pl.BlockSpec((tm, w), lambda i: (i, 0))
    mod_spec = lambda k: pl.BlockSpec((None, None, 1, d), lambda i: (group_of_tile(i), k, 0, 0))
    vec_spec = lambda w: pl.BlockSpec((1, w), lambda i: (0, 0))
    in_specs = [
        row_spec(d), mod_spec(1), mod_spec(0), vec_spec(d),
        pl.BlockSpec((d, D_IN_PAD), lambda i: (0, 0), pipeline_mode=pl.Buffered(1)),
        vec_spec(HEAD_DIM), vec_spec(HEAD_DIM),
    ]
    args = [x2d, mod4, mod4, pre_g, w_in_p, qg, kg]
    if use_rope:
        cos_t, sin_t = rope_tabs
        tiles_per_seq = cos_t.shape[0] // tm
        tab_spec = pl.BlockSpec((tm, HEAD_DIM), lambda i: (i % tiles_per_seq, 0))
        in_specs += [tab_spec, tab_spec]
        args += [cos_t, sin_t]
    widths = (ATT_W, KV_W, KV_W, RET_QK_W, RET_QK_W, RET_W, RET_W, SSD_W, CONV_CH)
    out_specs = [row_spec(w) for w in widths] + [row_spec(LANE)]
    out_shape = [jax.ShapeDtypeStruct((rows, w), BF16) for w in widths]
    out_shape.append(jax.ShapeDtypeStruct((rows, LANE), F32))
    outs = pl.pallas_call(
        functools.partial(_in_proj_kernel, use_rope),
        grid=(nt,),
        in_specs=in_specs,
        out_specs=out_specs,
        out_shape=out_shape,
        compiler_params=_cparams(("parallel",)),
        name="in_proj_rope" if use_rope else "in_proj",
    )(*args)
    names = ("aq", "ak", "av", "rq", "rk", "rv", "rg", "z", "xbc", "dt")
    return dict(zip(names, outs))


def _attn_kernel(n_kv, tq, n_sub, *refs):
    q_ref = refs[0]
    kv_refs = refs[1:1 + 2 * n_kv]
    o_ref = refs[1 + 2 * n_kv]
    ts = tq // n_sub
    for c in range(n_sub):
        q = q_ref[c * ts:(c + 1) * ts, :]
        qs = jnp.concatenate([q[:, r * HEAD_DIM:(r + 1) * HEAD_DIM] for r in range(ATT_REP)], axis=0)
        scores = [_dot_nt(qs, kv_refs[2 * s][...]) for s in range(n_kv)]
        m = functools.reduce(jnp.maximum, [jnp.max(s, axis=-1, keepdims=True) for s in scores])
        probs = [jnp.exp2(s - m) for s in scores]
        denom = functools.reduce(jnp.add, [jnp.sum(p, axis=-1, keepdims=True) for p in probs])
        acc = functools.reduce(jnp.add, [_dot(p.astype(BF16), kv_refs[2 * s + 1][...]) for s, p in enumerate(probs)])
        out = acc / denom
        for r in range(ATT_REP):
            o_ref[c * ts:(c + 1) * ts, r * HEAD_DIM:(r + 1) * HEAD_DIM] = out[r * ts:(r + 1) * ts].astype(BF16)


def _attention(q2d, kv_list, batch, tq, n_sub):
    rows = q2d.shape[0]
    lq = rows // batch
    nq = lq // tq
    gw = ATT_REP * HEAD_DIM
    in_specs = [pl.BlockSpec((tq, gw), lambda b, g, i: (b * nq + i, g))]
    args = [q2d]
    for k2d, v2d in kv_list:
        lk = k2d.shape[0] // batch
        spec = pl.BlockSpec((lk, HEAD_DIM), lambda b, g, i: (b, g))
        in_specs += [spec, spec]
        args += [k2d, v2d]
    return pl.pallas_call(
        functools.partial(_attn_kernel, len(kv_list), tq, n_sub),
        grid=(batch, ATT_KV_HEADS, nq),
        in_specs=in_specs,
        out_specs=pl.BlockSpec((tq, gw), lambda b, g, i: (b * nq + i, g)),
        out_shape=jax.ShapeDtypeStruct((rows, ATT_W), BF16),
        compiler_params=_cparams(("parallel", "parallel", "parallel")),
        name="attention",
    )(*args)


_RT_INTRA, _RT_INTER_F, _RT_STATE_F, _RT_DEC_F, _RT_INTER_B, _RT_STATE_B, _RT_DEC_B = range(7)


def _ret_kernel(n_ctx_chunks, n_lat_chunks, ctx_out, *refs):
    (qc_ref, kc_ref, vc_ref, gc_ref, ql_ref, kl_ref, vl_ref, gl_ref, df_ref, db_ref) = refs[:10]
    if ctx_out:
        oc_ref, ol_ref, yc_s, yl_s, sf_s, sb_s, tab_s = refs[10:]
    else:
        ol_ref, yc_s, yl_s, sf_s, sb_s, tab_s = refs[10:]
        oc_ref = None

    ii = lax.broadcasted_iota(jnp.int32, (CHUNK, CHUNK), 0)
    jj = lax.broadcasted_iota(jnp.int32, (CHUNK, CHUNK), 1)
    dist = (ii - jj).astype(F32)
    rowi = ii.astype(F32)
    ones = jnp.ones((CHUNK, CHUNK), F32)
    for hd in range(RET_HEADS):
        lam_f = jnp.log1p(-jnp.exp2(df_ref[hd]))
        lam_b = jnp.log1p(-jnp.exp2(db_ref[hd]))
        tab_s[hd, _RT_INTRA] = (jnp.where(jj <= ii, jnp.exp(dist * lam_f), 0.0)
                                + jnp.where(jj >= ii, jnp.exp(-dist * lam_b), 0.0))
        tab_s[hd, _RT_INTER_F] = jnp.exp((rowi + 1.0) * lam_f)
        tab_s[hd, _RT_STATE_F] = jnp.exp((CHUNK - 1.0 - rowi) * lam_f)
        tab_s[hd, _RT_DEC_F] = ones * jnp.exp(CHUNK * lam_f)
        tab_s[hd, _RT_INTER_B] = jnp.exp((CHUNK - rowi) * lam_b)
        tab_s[hd, _RT_STATE_B] = jnp.exp(rowi * lam_b)
        tab_s[hd, _RT_DEC_B] = ones * jnp.exp(CHUNK * lam_b)

    sf_s[...] = jnp.zeros_like(sf_s)
    sb_s[...] = jnp.zeros_like(sb_s)

    def fwd_chunk(q_ref, k_ref, v_ref, y_ref, r0):
        for hd in range(RET_HEADS):
            cols = slice(hd * RET_DK, (hd + 1) * RET_DK)
            q = q_ref[pl.ds(r0, CHUNK), cols]
            k = k_ref[pl.ds(r0, CHUNK), cols]
            v = v_ref[pl.ds(r0, CHUNK), cols]
            s = _dot_nt(q, k) * tab_s[hd, _RT_INTRA]
            sf = sf_s[hd]
            y = _dot(s.astype(BF16), v) + tab_s[hd, _RT_INTER_F] * _dot(q, sf.astype(BF16))
            y_ref[pl.ds(r0, CHUNK), cols] = y
            vw = (v.astype(F32) * tab_s[hd, _RT_STATE_F]).astype(BF16)
            sf_s[hd] = sf * tab_s[hd, _RT_DEC_F] + _dot_tn(k, vw)

    def bwd_chunk(q_ref, k_ref, v_ref, g_ref, y_ref, o_ref, r0):
        for hd in range(RET_HEADS):
            cols = slice(hd * RET_DK, (hd + 1) * RET_DK)
            k = k_ref[pl.ds(r0, CHUNK), cols]
            v = v_ref[pl.ds(r0, CHUNK), cols]
            sb = sb_s[hd]
            if o_ref is not None:
                q = q_ref[pl.ds(r0, CHUNK), cols]
                y = y_ref[pl.ds(r0, CHUNK), cols] + tab_s[hd, _RT_INTER_B] * _dot(q, sb.astype(BF16))
                gate = _silu(g_ref[pl.ds(r0, CHUNK), cols].astype(F32))
                o_ref[pl.ds(r0, CHUNK), cols] = (_rms(y) * gate).astype(BF16)
            vw = (v.astype(F32) * tab_s[hd, _RT_STATE_B]).astype(BF16)
            sb_s[hd] = sb * tab_s[hd, _RT_DEC_B] + _dot_tn(k, vw)

    for c in range(n_ctx_chunks):
        fwd_chunk(qc_ref, kc_ref, vc_ref, yc_s, c * CHUNK)

    def lat_fwd(c, carry):
        fwd_chunk(ql_ref, kl_ref, vl_ref, yl_s, pl.multiple_of(c * CHUNK, CHUNK))
        return carry

    lax.fori_loop(0, n_lat_chunks, lat_fwd, 0, unroll=SCAN_UNROLL)

    for c in reversed(range(n_ctx_chunks)):
        bwd_chunk(qc_ref, kc_ref, vc_ref, gc_ref, yc_s, oc_ref, c * CHUNK)

    def lat_bwd(t, carry):
        c = n_lat_chunks - 1 - t
        bwd_chunk(ql_ref, kl_ref, vl_ref, gl_ref, yl_s, ol_ref, pl.multiple_of(c * CHUNK, CHUNK))
        return carry

    lax.fori_loop(0, n_lat_chunks, lat_bwd, 0, unroll=SCAN_UNROLL)


def _retention(tc, tl, decay_f, decay_b, batch, ctx_out):
    lc = tc["rq"].shape[0] // batch
    ll = tl["rq"].shape[0] // batch
    cspec = pl.BlockSpec((lc, RET_W), lambda b: (b, 0))
    lspec = pl.BlockSpec((ll, RET_W), lambda b: (b, 0))
    pspec = pl.BlockSpec((RET_HEADS, 1, LANE), lambda b: (0, 0, 0))
    out_specs = [lspec]
    out_shape = [jax.ShapeDtypeStruct((batch * ll, RET_W), BF16)]
    if ctx_out:
        out_specs = [cspec] + out_specs
        out_shape = [jax.ShapeDtypeStruct((batch * lc, RET_W), BF16)] + out_shape
    bcast = lambda p: jnp.broadcast_to(p.astype(F32)[:, None, None], (RET_HEADS, 1, LANE))
    outs = pl.pallas_call(
        functools.partial(_ret_kernel, lc // CHUNK, ll // CHUNK, ctx_out),
        grid=(batch,),
        in_specs=[cspec] * 4 + [lspec] * 4 + [pspec, pspec],
        out_specs=out_specs,
        out_shape=out_shape,
        scratch_shapes=[
            pltpu.VMEM((lc, RET_W), F32), pltpu.VMEM((ll, RET_W), F32),
            pltpu.VMEM((RET_HEADS, RET_DK, RET_DV), F32), pltpu.VMEM((RET_HEADS, RET_DK, RET_DV), F32),
            pltpu.VMEM((RET_HEADS, 7, CHUNK, CHUNK), F32),
        ],
        compiler_params=_cparams(("parallel",)),
        name="retention",
    )(tc["rq"], tc["rk"], tc["rv"], tc["rg"], tl["rq"], tl["rk"], tl["rv"], tl["rg"],
      bcast(decay_f), bcast(decay_b))
    return (outs[0], outs[1]) if ctx_out else (None, outs[0])


def _expand_heads(v, e_ref):
    hi = v.astype(BF16)
    lo = (v - hi.astype(F32)).astype(BF16)
    e = e_ref[...]
    return _dot(hi, e) + _dot(lo, e)


def _cumsum_rows(v):
    rows = lax.broadcasted_iota(jnp.int32, v.shape, 0)
    s = 1
    while s < v.shape[0]:
        v = v + jnp.where(rows >= s, pltpu.roll(v, s, 0), 0.0)
        s *= 2
    return v


def _ssd_kernel(n_ctx_chunks, n_lat_chunks, ctx_out, *refs):
    (xc_ref, dtc_ref, zc_ref, xl_ref, dtl_ref, zl_ref,
     cw_ref, cb_ref, bias_ref, alog_ref, skip_ref, ng_ref, ef_ref, eb_ref, shift_ref) = refs[:15]
    if ctx_out:
        oc_ref, ol_ref, tok_s, y_s, sf_s, sb_s = refs[15:]
    else:
        ol_ref, tok_s, y_s, sf_s, sb_s = refs[15:]
        oc_ref = None
    lc = n_ctx_chunks * CHUNK

    ii = lax.broadcasted_iota(jnp.int32, (CHUNK, CHUNK), 0)
    jj = lax.broadcasted_iota(jnp.int32, (CHUNK, CHUNK), 1)
    lower = jj < ii
    upper = jj > ii
    neg_a = -jnp.exp(alog_ref[...])
    halo = CONV_HALO

    def conv_silu(x_ref, c, n_chunks):
        if isinstance(c, int):
            r0 = c * CHUNK
            prev0 = max(r0 - halo, 0)
            next0 = min(r0 + CHUNK, (n_chunks - 1) * CHUNK)
        else:
            r0 = pl.multiple_of(c * CHUNK, CHUNK)
            prev0 = pl.multiple_of(jnp.maximum(r0 - halo, 0), halo)
            next0 = pl.multiple_of(jnp.minimum(r0 + CHUNK, (n_chunks - 1) * CHUNK), halo)
        first = c == 0
        last = c == n_chunks - 1
        centre = x_ref[pl.ds(r0, CHUNK), :]
        prev = x_ref[pl.ds(prev0, halo), :]
        nxt = x_ref[pl.ds(next0, halo), :]
        prev = jnp.where(first, jnp.zeros_like(prev), prev)
        nxt = jnp.where(last, jnp.zeros_like(nxt), nxt)
        pad = jnp.zeros((CONV_EXT - CHUNK - 2 * halo, CONV_CH), BF16)
        ext = jnp.concatenate([centre, prev, nxt, pad], axis=0)
        acc = cb_ref[...] + cw_ref[SSD_CONV // 2:SSD_CONV // 2 + 1, :] * centre.astype(F32)
        taps = [k for k in range(SSD_CONV) if k != SSD_CONV // 2]
        for n, k in enumerate(taps):
            acc = acc + cw_ref[k:k + 1, :] * _dot(shift_ref[n], ext)
        return _silu(acc)

    def decays(dt_ref, r0):
        dt = jax.nn.softplus(dt_ref[pl.ds(r0, CHUNK), :] + bias_ref[...])
        la = dt * neg_a
        a_inc = _cumsum_rows(la)
        total = a_inc[CHUNK - 1:CHUNK, :]
        return dt, la, a_inc, total

    def decay_row(total, e_ref):
        return _expand_heads(jnp.broadcast_to(jnp.exp(total), (8, LANE)), e_ref)[0:1, :]

    def fwd_chunk(x_ref, dt_ref, c, n_chunks, base):
        if isinstance(c, int):
            r0 = c * CHUNK
            t0 = base + r0
        else:
            r0 = pl.multiple_of(c * CHUNK, CHUNK)
            t0 = pl.multiple_of(base + r0, CHUNK)
        tok = conv_silu(x_ref, c, n_chunks)
        tok_b = tok.astype(BF16)
        tok_s[pl.ds(t0, CHUNK), :] = tok_b
        xs = tok_b[:, :SSD_W]
        dt, la, a_inc, total = decays(dt_ref, r0)
        a_exc = a_inc - la
        log2e = math.log2(math.e)
        ldt = jnp.log2(dt)
        col_term = a_inc * log2e
        col_term_b = a_exc * log2e
        row_t = (jnp.where(lax.broadcasted_iota(jnp.int32, (CHUNK, LANE), 1) < SSD_HEADS,
                           ldt - col_term, ldt + col_term_b)).T
        dt_t = dt.T
        ys = []
        inter = []
        for g in range(SSD_GROUPS):
            bs = tok_b[:, SSD_W + g * SSD_STATE:SSD_W + (g + 1) * SSD_STATE]
            cs = tok_b[:, SSD_W + SSD_BC + g * SSD_STATE:SSD_W + SSD_BC + (g + 1) * SSD_STATE]
            gmat = _dot_nt(cs, bs)
            for pair in range(HEADS_PER_GROUP // 2):
                ms = []
                for sub in range(2):
                    hd = g * HEADS_PER_GROUP + 2 * pair + sub
                    hb = SSD_HEADS + hd
                    diag = jnp.log2(dt_t[hd:hd + 1, :] + dt_t[hb:hb + 1, :])
                    expo = jnp.where(lower, col_term[:, hd:hd + 1] + row_t[hd:hd + 1, :],
                                     jnp.where(upper, row_t[hb:hb + 1, :] - col_term_b[:, hb:hb + 1], diag))
                    ms.append((gmat * jnp.exp2(expo)).astype(BF16))
                col = (g * HEADS_PER_GROUP + 2 * pair) * SSD_HEAD_DIM
                xpair = xs[:, col:col + 2 * SSD_HEAD_DIM]
                lane = lax.broadcasted_iota(jnp.int32, xpair.shape, 1)
                zero = jnp.zeros_like(xpair)
                rhs = jnp.concatenate([jnp.where(lane < SSD_HEAD_DIM, xpair, zero),
                                       jnp.where(lane < SSD_HEAD_DIM, zero, xpair)], axis=0)
                ys.append(_dot(jnp.concatenate(ms, axis=1), rhs))
            inter.append(_dot(cs, sf_s[:, g * GROUP_W:(g + 1) * GROUP_W].astype(BF16)))
        y = jnp.concatenate(ys, axis=1) + _expand_heads(jnp.exp(a_inc), ef_ref) * jnp.concatenate(inter, axis=1)
        y_s[pl.ds(t0, CHUNK), :] = y
        xw = (xs.astype(F32) * _expand_heads(jnp.exp(total - a_inc) * dt, ef_ref)).astype(BF16)
        dec = decay_row(total, ef_ref)
        for g in range(SSD_GROUPS):
            bs = tok_b[:, SSD_W + g * SSD_STATE:SSD_W + (g + 1) * SSD_STATE]
            sl = slice(g * GROUP_W, (g + 1) * GROUP_W)
            sf_s[:, sl] = sf_s[:, sl] * dec[:, sl] + _dot_tn(bs, xw[:, sl])

    def bwd_chunk(dt_ref, z_ref, o_ref, c, base):
        if isinstance(c, int):
            r0 = c * CHUNK
            t0 = base + r0
        else:
            r0 = pl.multiple_of(c * CHUNK, CHUNK)
            t0 = pl.multiple_of(base + r0, CHUNK)
        tok_b = tok_s[pl.ds(t0, CHUNK), :]
        xs = tok_b[:, :SSD_W].astype(F32)
        dt, la, a_inc, total = decays(dt_ref, r0)
        a_exc = a_inc - la
        if o_ref is not None:
            inter = []
            for g in range(SSD_GROUPS):
                cs = tok_b[:, SSD_W + SSD_BC + g * SSD_STATE:SSD_W + SSD_BC + (g + 1) * SSD_STATE]
                inter.append(_dot(cs, sb_s[:, g * GROUP_W:(g + 1) * GROUP_W].astype(BF16)))
            y = y_s[pl.ds(t0, CHUNK), :] + _expand_heads(jnp.exp(total - a_exc), eb_ref) * jnp.concatenate(inter, axis=1)
            y = (y + skip_ref[...] * xs) * _silu(z_ref[pl.ds(r0, CHUNK), :].astype(F32))
            o_ref[pl.ds(r0, CHUNK), :] = (_rms(y) * ng_ref[...]).astype(BF16)
        xw = (xs * _expand_heads(jnp.exp(a_exc) * dt, eb_ref)).astype(BF16)
        dec = decay_row(total, eb_ref)
        for g in range(SSD_GROUPS):
            bs = tok_b[:, SSD_W + g * SSD_STATE:SSD_W + (g + 1) * SSD_STATE]
            sl = slice(g * GROUP_W, (g + 1) * GROUP_W)
            sb_s[:, sl] = sb_s[:, sl] * dec[:, sl] + _dot_tn(bs, xw[:, sl])

    sf_s[...] = jnp.zeros_like(sf_s)
    sb_s[...] = jnp.zeros_like(sb_s)

    for c in range(n_ctx_chunks):
        fwd_chunk(xc_ref, dtc_ref, c, n_ctx_chunks, 0)

    def lat_fwd(c, carry):
        fwd_chunk(xl_ref, dtl_ref, c, n_lat_chunks, lc)
        return carry

    lax.fori_loop(0, n_lat_chunks, lat_fwd, 0, unroll=SCAN_UNROLL)

    for c in reversed(range(n_ctx_chunks)):
        bwd_chunk(dtc_ref, zc_ref, oc_ref, c, 0)

    def lat_bwd(t, carry):
        bwd_chunk(dtl_ref, zl_ref, ol_ref, n_lat_chunks - 1 - t, lc)
        return carry

    lax.fori_loop(0, n_lat_chunks, lat_bwd, 0, unroll=SCAN_UNROLL)


def _conv_shift_matrices():
    taps = [k for k in range(SSD_CONV) if k != SSD_CONV // 2]
    mats = np.zeros((len(taps), CHUNK, CONV_EXT), np.float32)
    for n, k in enumerate(taps):
        for i in range(CHUNK):
            src = i + k - SSD_CONV // 2
            if src < 0:
                col = CHUNK + CONV_HALO + src
            elif src >= CHUNK:
                col = CHUNK + CONV_HALO + (src - CHUNK)
            else:
                col = src
            mats[n, i, col] = 1.0
    return mats


def _head_lane_vec(f, b):
    v = jnp.zeros((1, LANE), F32)
    v = v.at[0, :SSD_HEADS].set(f.astype(F32))
    return v.at[0, SSD_HEADS:2 * SSD_HEADS].set(b.astype(F32))


def _ssd(tc, tl, p, batch, ctx_out):
    lc = tc["xbc"].shape[0] // batch
    ll = tl["xbc"].shape[0] // batch
    rows = lambda n, w: pl.BlockSpec((n, w), lambda b: (b, 0))
    vec = lambda w: pl.BlockSpec((1, w), lambda b: (0, 0))
    whole = lambda a: pl.BlockSpec(a.shape, lambda b: (0,) * a.ndim)
    head_of_col = np.arange(SSD_W) // SSD_HEAD_DIM
    ef = jnp.asarray(np.arange(LANE)[:, None] == head_of_col[None, :], BF16)
    eb = jnp.asarray(np.arange(LANE)[:, None] == head_of_col[None, :] + SSD_HEADS, BF16)
    shift = jnp.asarray(_conv_shift_matrices(), BF16)
    conv_w = p["conv_w"].astype(F32)
    conv_b = p["conv_b"].astype(F32).reshape(1, CONV_CH)
    bias = _head_lane_vec(p["dt_bias_f"], p["dt_bias_b"])
    alog = _head_lane_vec(p["a_log_f"], p["a_log_b"])
    skip = jnp.repeat(p["d_skip"].astype(F32), SSD_HEAD_DIM).reshape(1, SSD_W)
    norm_g = p["ssd_norm_g"].astype(F32).reshape(1, SSD_W)
    out_specs = [rows(ll, SSD_W)]
    out_shape = [jax.ShapeDtypeStruct((batch * ll, SSD_W), BF16)]
    if ctx_out:
        out_specs = [rows(lc, SSD_W)] + out_specs
        out_shape = [jax.ShapeDtypeStruct((batch * lc, SSD_W), BF16)] + out_shape
    outs = pl.pallas_call(
        functools.partial(_ssd_kernel, lc // CHUNK, ll // CHUNK, ctx_out),
        grid=(batch,),
        in_specs=[rows(lc, CONV_CH), rows(lc, LANE), rows(lc, SSD_W),
                  rows(ll, CONV_CH), rows(ll, LANE), rows(ll, SSD_W),
                  whole(conv_w), vec(CONV_CH), vec(LANE), vec(LANE), vec(SSD_W), vec(SSD_W),
                  whole(ef), whole(eb), whole(shift)],
        out_specs=out_specs,
        out_shape=out_shape,
        scratch_shapes=[
            pltpu.VMEM((lc + ll, CONV_CH), BF16), pltpu.VMEM((lc + ll, SSD_W), F32),
            pltpu.VMEM((SSD_STATE, SSD_W), F32), pltpu.VMEM((SSD_STATE, SSD_W), F32),
        ],
        compiler_params=_cparams(("parallel",)),
        name="ssd_scan",
    )(tc["xbc"], tc["dt"], tc["z"], tl["xbc"], tl["dt"], tl["z"],
      conv_w, conv_b, bias, alog, skip, norm_g, ef, eb, shift)
    return (outs[0], outs[1]) if ctx_out else (None, outs[0])


def _out_proj_kernel(att_ref, ret_ref, ssd_ref, w_ref, x_ref, g1_ref, pmg_ref, pfg_ref, sc_ref, sh_ref,
                     x1_ref, hf_ref):
    m = (_dot(att_ref[...], w_ref[0:ATT_W, :])
         + _dot(ret_ref[...], w_ref[ATT_W:ATT_W + RET_W, :])
         + _dot(ssd_ref[...], w_ref[ATT_W + RET_W:, :]))
    x1 = x_ref[...] + g1_ref[...] * (_rms(m) * pmg_ref[...])
    x1_ref[...] = x1
    hf_ref[...] = (_rms(x1) * pfg_ref[...] * (1.0 + sc_ref[...]) + sh_ref[...]).astype(BF16)


def _out_proj(att, ret, ssd, w_out_b, x2d, mod4, group_of_tile, post_mix_g, pre_ffn_g, tm):
    rows, d = x2d.shape
    row_spec = lambda w: pl.BlockSpec((tm, w), lambda i: (i, 0))
    mod_spec = lambda k: pl.BlockSpec((None, None, 1, d), lambda i: (group_of_tile(i), k, 0, 0))
    vec_spec = pl.BlockSpec((1, d), lambda i: (0, 0))
    return pl.pallas_call(
        _out_proj_kernel,
        grid=(rows // tm,),
        in_specs=[row_spec(ATT_W), row_spec(RET_W), row_spec(SSD_W),
                  pl.BlockSpec(w_out_b.shape, lambda i: (0, 0)),
                  row_spec(d), mod_spec(2), vec_spec, vec_spec, mod_spec(4), mod_spec(3)],
        out_specs=[row_spec(d), row_spec(d)],
        out_shape=[jax.ShapeDtypeStruct((rows, d), F32), jax.ShapeDtypeStruct((rows, d), BF16)],
        compiler_params=_cparams(("parallel",)),
        name="out_proj",
    )(att, ret, ssd, w_out_b, x2d, mod4, post_mix_g, pre_ffn_g, mod4, mod4)


def _ffn_kernel(hf_ref, wg_ref, wu_ref, wd_ref, x1_ref, g2_ref, pg_ref, o_ref, acc_ref):
    j = pl.program_id(1)

    @pl.when(j == 0)
    def _():
        acc_ref[...] = jnp.zeros_like(acc_ref)

    hf = hf_ref[...]
    a = _silu(_dot(hf, wg_ref[...])) * _dot(hf, wu_ref[...])
    acc_ref[...] += _dot(a.astype(BF16), wd_ref[...])

    @pl.when(j == pl.num_programs(1) - 1)
    def _():
        o_ref[...] = x1_ref[...] + g2_ref[...] * (_rms(acc_ref[...]) * pg_ref[...])


def _ffn(hf, wg, wu, wd, x1, mod4, group_of_tile, post_ffn_g, tm, tf):
    rows, d = x1.shape
    dff = wg.shape[1]
    return pl.pallas_call(
        _ffn_kernel,
        grid=(rows // tm, dff // tf),
        in_specs=[
            pl.BlockSpec((tm, d), lambda i, j: (i, 0)),
            pl.BlockSpec((d, tf), lambda i, j: (0, j)),
            pl.BlockSpec((d, tf), lambda i, j: (0, j)),
            pl.BlockSpec((tf, d), lambda i, j: (j, 0)),
            pl.BlockSpec((tm, d), lambda i, j: (i, 0)),
            pl.BlockSpec((None, None, 1, d), lambda i, j: (group_of_tile(i), 5, 0, 0)),
            pl.BlockSpec((1, d), lambda i, j: (0, 0)),
        ],
        out_specs=pl.BlockSpec((tm, d), lambda i, j: (i, 0)),
        out_shape=jax.ShapeDtypeStruct((rows, d), F32),
        scratch_shapes=[pltpu.VMEM((tm, d), F32)],
        compiler_params=_cparams(("parallel", "arbitrary")),
        name="swiglu_ffn",
    )(hf, wg, wu, wd, x1, mod4, post_ffn_g)


def _rope_tables(seq_len):
    rows = seq_len // GRID_W
    row = jnp.repeat(jnp.arange(rows, dtype=F32), GRID_W)
    col = jnp.tile(jnp.arange(GRID_W, dtype=F32), rows)
    n_freq = HEAD_DIM // 4
    inv = ROPE_THETA ** (-jnp.arange(n_freq, dtype=F32) / n_freq)
    ang = jnp.concatenate([row[:, None] * inv, col[:, None] * inv], axis=-1)
    cos, sin = jnp.cos(ang), jnp.sin(ang)
    return jnp.concatenate([cos, cos], axis=-1), jnp.concatenate([-sin, sin], axis=-1)


def _pick_tile(n, target):
    t = min(n, target)
    while n % t:
        t //= 2
    return t


def _layer_weights(w_in, q_norm_g, k_norm_g, w_out, w_gate, w_up, w_down):
    perm = np.concatenate([np.arange(0, HEAD_DIM, 2), np.arange(1, HEAD_DIM, 2)])
    cols = np.arange(D_IN)
    for off, heads in ((OFF_AQ, ATT_HEADS), (OFF_AK, ATT_KV_HEADS), (OFF_RQ, RET_HEADS), (OFF_RK, RET_HEADS)):
        for hd in range(heads):
            cols[off + hd * HEAD_DIM:off + (hd + 1) * HEAD_DIM] = off + hd * HEAD_DIM + perm
    w_in_p = jnp.pad(w_in[:, cols].astype(BF16), ((0, 0), (0, D_IN_PAD - D_IN)))
    qg = q_norm_g.astype(F32)[perm].reshape(1, HEAD_DIM)
    kg = k_norm_g.astype(F32)[perm].reshape(1, HEAD_DIM)
    return w_in_p, qg, kg, w_out.astype(BF16), w_gate.astype(BF16), w_up.astype(BF16), w_down.astype(BF16)


def kernel(x, c, ctx, c_ctx, w_mod, b_mod, pre_mix_g, post_mix_g, pre_ffn_g, post_ffn_g, w_in, q_norm_g, k_norm_g, ret_decay_f, ret_decay_b, conv_w, conv_b, dt_bias_f, dt_bias_b, a_log_f, a_log_b, d_skip, ssd_norm_g, w_out, w_gate, w_up, w_down):
    batch, seq, d = x.shape
    lc = ctx.shape[1]
    depth = w_mod.shape[0]
    assert batch < N_MOD_ROWS and seq % CHUNK == 0 and lc % CHUNK == 0 and seq % GRID_W == 0

    c16 = jnp.zeros((N_MOD_ROWS, d), F32).at[:batch].set(c.astype(F32)).at[batch].set(c_ctx.astype(F32))
    mod_all = _modulation(c16, w_mod.astype(F32), b_mod.astype(F32))
    rope_tabs = _rope_tables(seq)

    tm_lat = _pick_tile(seq, 512)
    tm_ctx = _pick_tile(batch * lc, 512)
    tiles_per_batch = seq // tm_lat
    lat_group = lambda i: i // tiles_per_batch
    ctx_group = lambda i: batch
    tq_lat = _pick_tile(seq, 512)
    tq_ctx = _pick_tile(lc, 256)
    sub_lat = 4 if tq_lat % 64 == 0 else 1
    sub_ctx = 2 if tq_ctx % 32 == 0 else 1
    tf = _pick_tile(w_gate.shape[2], 512)

    xl = x.reshape(batch * seq, d).astype(F32)
    xc = ctx.reshape(batch * lc, d).astype(F32)
    vec = lambda v: v.astype(F32).reshape(1, d)

    for l in range(depth):
        last = l == depth - 1
        w_in_p, qg, kg, w_out_b, wg, wu, wd = _layer_weights(
            w_in[l], q_norm_g[l], k_norm_g[l], w_out[l], w_gate[l], w_up[l], w_down[l])
        mod4 = mod_all[l].reshape(N_MOD_ROWS, 6, 1, d)
        p = dict(conv_w=conv_w[l], conv_b=conv_b[l], dt_bias_f=dt_bias_f[l], dt_bias_b=dt_bias_b[l],
                 a_log_f=a_log_f[l], a_log_b=a_log_b[l], d_skip=d_skip[l], ssd_norm_g=ssd_norm_g[l])

        tl = _in_proj(xl, mod4, lat_group, vec(pre_mix_g[l]), w_in_p, qg, kg, rope_tabs, tm_lat)
        tc = _in_proj(xc, mod4, ctx_group, vec(pre_mix_g[l]), w_in_p, qg, kg, None, tm_ctx)

        att_l = _attention(tl["aq"], [(tc["ak"], tc["av"]), (tl["ak"], tl["av"])], batch, tq_lat, sub_lat)
        ret_c, ret_l = _retention(tc, tl, ret_decay_f[l], ret_decay_b[l], batch, not last)
        ssd_c, ssd_l = _ssd(tc, tl, p, batch, not last)

        x1, hf = _out_proj(att_l, ret_l, ssd_l, w_out_b, xl, mod4, lat_group,
                           vec(post_mix_g[l]), vec(pre_ffn_g[l]), tm_lat)
        xl = _ffn(hf, wg, wu, wd, x1, mod4, lat_group, vec(post_ffn_g[l]), tm_lat, tf)

        if not last:
            att_c = _attention(tc["aq"], [(tc["ak"], tc["av"])], batch, tq_ctx, sub_ctx)
            x1c, hfc = _out_proj(att_c, ret_c, ssd_c, w_out_b, xc, mod4, ctx_group,
                                 vec(post_mix_g[l]), vec(pre_ffn_g[l]), tm_ctx)
            xc = _ffn(hfc, wg, wu, wd, x1c, mod4, ctx_group, vec(post_ffn_g[l]), tm_ctx, tf)

    return xl.reshape(batch, seq, d).astype(x.dtype)
```

```python
import functools
import math

import numpy as np
import jax
import jax.numpy as jnp
from jax import lax
from jax.experimental import pallas as pl
from jax.experimental.pallas import tpu as pltpu

F32 = jnp.float32
BF16 = jnp.bfloat16

GRID_W = 64
HEAD_DIM = 128
ATT_HEADS = 6
ATT_KV_HEADS = 2
ATT_REP = ATT_HEADS // ATT_KV_HEADS
RET_HEADS = 4
RET_DK = 128
RET_DV = 128
SSD_HEADS = 12
SSD_HEAD_DIM = 64
SSD_GROUPS = 2
SSD_STATE = 128
SSD_CONV = 5
ATT_W = ATT_HEADS * HEAD_DIM
KV_W = ATT_KV_HEADS * HEAD_DIM
RET_QK_W = RET_HEADS * RET_DK
RET_W = RET_HEADS * RET_DV
SSD_W = SSD_HEADS * SSD_HEAD_DIM
SSD_BC = SSD_GROUPS * SSD_STATE
CONV_CH = SSD_W + 2 * SSD_BC
GROUP_W = SSD_W // SSD_GROUPS
HEADS_PER_GROUP = SSD_HEADS // SSD_GROUPS
CHUNK = 128
ROPE_THETA = 10000.0
EPS = 1e-6
LANE = 128
FFN_DOWN_COLS = 512
FFN_EPILOGUE_ROWS = 128
SCAN_UNROLL = 2
CONV_HALO = 16
CONV_EXT = 256

OFF_AQ = 0
OFF_AK = OFF_AQ + ATT_W
OFF_AV = OFF_AK + KV_W
OFF_RQ = OFF_AV + KV_W
OFF_RK = OFF_RQ + RET_QK_W
OFF_RV = OFF_RK + RET_QK_W
OFF_RG = OFF_RV + RET_W
OFF_Z = OFF_RG + RET_W
OFF_XBC = OFF_Z + SSD_W
OFF_DT = OFF_XBC + CONV_CH
D_IN = OFF_DT + 2 * SSD_HEADS
D_IN_PAD = OFF_DT + LANE

VMEM_LIMIT = 60 * 1024 * 1024
N_MOD_ROWS = 16


def _cparams(sem):
    return pltpu.CompilerParams(dimension_semantics=sem, vmem_limit_bytes=VMEM_LIMIT)


def _silu(v):
    return v * jax.nn.sigmoid(v)


def _rms(v):
    return v * lax.rsqrt(jnp.mean(v * v, axis=-1, keepdims=True) + EPS)


def _dot(a, b):
    return jnp.dot(a, b, preferred_element_type=F32)


def _dot_nt(a, b):
    return lax.dot_general(a, b, (((1,), (1,)), ((), ())), preferred_element_type=F32)


def _dot_tn(a, b):
    return lax.dot_general(a, b, (((0,), (0,)), ((), ())), preferred_element_type=F32)


def _mod_kernel(c_ref, w_ref, b_ref, o_ref):
    a = _silu(c_ref[...]).astype(BF16)
    o_ref[...] = _dot(a, w_ref[...].astype(BF16)) + b_ref[...]


def _modulation(c16, w_mod, b_mod):
    depth, d, n = w_mod.shape
    tn = _pick_tile(n, 1024)
    return pl.pallas_call(
        _mod_kernel,
        grid=(depth, n // tn),
        in_specs=[
            pl.BlockSpec((N_MOD_ROWS, d), lambda l, j: (0, 0)),
            pl.BlockSpec((None, d, tn), lambda l, j: (l, 0, j)),
            pl.BlockSpec((None, 1, tn), lambda l, j: (l, 0, j)),
        ],
        out_specs=pl.BlockSpec((None, N_MOD_ROWS, tn), lambda l, j: (l, 0, j)),
        out_shape=jax.ShapeDtypeStruct((depth, N_MOD_ROWS, n), F32),
        compiler_params=_cparams(("parallel", "parallel")),
        name="adaln_modulation",
    )(c16, w_mod, b_mod.reshape(depth, 1, n))


def _in_proj_kernel(use_rope, *refs):
    if use_rope:
        (x_ref, sc_ref, sh_ref, g_ref, w_ref, wdt_ref, qg_ref, kg_ref, cos_ref, sin_ref,
         aq_ref, ak_ref, av_ref, rq_ref, rk_ref, rv_ref, rg_ref, z_ref, xbc_ref, dt_ref) = refs
        cos = cos_ref[...]
        sin = sin_ref[...]
        even_lane = (lax.broadcasted_iota(jnp.int32, cos.shape, 1) & 1) == 0
    else:
        (x_ref, sc_ref, sh_ref, g_ref, w_ref, wdt_ref, qg_ref, kg_ref,
         aq_ref, ak_ref, av_ref, rq_ref, rk_ref, rv_ref, rg_ref, z_ref, xbc_ref, dt_ref) = refs

    x = x_ref[...]
    h = (_rms(x) * g_ref[...] * (1.0 + sc_ref[...]) + sh_ref[...]).astype(BF16)

    def proj(lo, width):
        return _dot(h, w_ref[:, lo:lo + width])

    def rope(t):
        if not use_rope:
            return t
        partner = jnp.where(even_lane, pltpu.roll(t, HEAD_DIM - 1, 1), pltpu.roll(t, 1, 1))
        return t * cos + partner * sin

    att_scale = HEAD_DIM ** -0.5 * math.log2(math.e)
    ret_scale = RET_DK ** -0.5

    acc = proj(OFF_AQ, ATT_W)
    for hd in range(ATT_HEADS):
        t = acc[:, hd * HEAD_DIM:(hd + 1) * HEAD_DIM]
        t = rope(_rms(t) * qg_ref[...]) * att_scale
        aq_ref[:, hd * HEAD_DIM:(hd + 1) * HEAD_DIM] = t.astype(BF16)

    acc = proj(OFF_AK, KV_W)
    for hd in range(ATT_KV_HEADS):
        t = acc[:, hd * HEAD_DIM:(hd + 1) * HEAD_DIM]
        t = rope(_rms(t) * kg_ref[...])
        ak_ref[:, hd * HEAD_DIM:(hd + 1) * HEAD_DIM] = t.astype(BF16)

    av_ref[...] = proj(OFF_AV, KV_W).astype(BF16)

    acc = proj(OFF_RQ, RET_QK_W)
    for hd in range(RET_HEADS):
        t = rope(acc[:, hd * RET_DK:(hd + 1) * RET_DK])
        rq_ref[:, hd * RET_DK:(hd + 1) * RET_DK] = t.astype(BF16)

    acc = proj(OFF_RK, RET_QK_W)
    for hd in range(RET_HEADS):
        t = rope(acc[:, hd * RET_DK:(hd + 1) * RET_DK] * ret_scale)
        rk_ref[:, hd * RET_DK:(hd + 1) * RET_DK] = t.astype(BF16)

    rv_ref[...] = proj(OFF_RV, RET_W).astype(BF16)
    rg_ref[...] = proj(OFF_RG, RET_W).astype(BF16)
    z_ref[...] = proj(OFF_Z, SSD_W).astype(BF16)
    xbc_ref[...] = proj(OFF_XBC, CONV_CH).astype(BF16)
    dt_ref[...] = _dot(h, wdt_ref[...])


def _in_proj(x2d, mod4, group_of_tile, pre_g, w_in_b, w_dt_b, qg, kg, rope_tabs, tm):
    rows, d = x2d.shape
    nt = rows // tm
    use_rope = rope_tabs is not None
    row_spec = lambda w: pl.BlockSpec((tm, w), lambda i: (i, 0))
    mod_spec = lambda k: pl.BlockSpec((None, None, 1, d), lambda i: (group_of_tile(i), k, 0, 0))
    vec_spec = lambda w: pl.BlockSpec((1, w), lambda i: (0, 0))
    in_specs = [
        row_spec(d), mod_spec(1), mod_spec(0), vec_spec(d),
        pl.BlockSpec((d, OFF_DT), lambda i: (0, 0), pipeline_mode=pl.Buffered(1)),
        pl.BlockSpec((d, LANE), lambda i: (0, 0)),
        vec_spec(HEAD_DIM), vec_spec(HEAD_DIM),
    ]
    args = [x2d, mod4, mod4, pre_g, w_in_b, w_dt_b, qg, kg]
    if use_rope:
        cos_t, sin_t = rope_tabs
        tiles_per_seq = cos_t.shape[0] // tm
        tab_spec = pl.BlockSpec((tm, HEAD_DIM), lambda i: (i % tiles_per_seq, 0))
        in_specs += [tab_spec, tab_spec]
        args += [cos_t, sin_t]
    widths = (ATT_W, KV_W, KV_W, RET_QK_W, RET_QK_W, RET_W, RET_W, SSD_W, CONV_CH)
    out_specs = [row_spec(w) for w in widths] + [row_spec(LANE)]
    out_shape = [jax.ShapeDtypeStruct((rows, w), BF16) for w in widths]
    out_shape.append(jax.ShapeDtypeStruct((rows, LANE), F32))
    outs = pl.pallas_call(
        functools.partial(_in_proj_kernel, use_rope),
        grid=(nt,),
        in_specs=in_specs,
        out_specs=out_specs,
        out_shape=out_shape,
        compiler_params=_cparams(("parallel",)),
        name="in_proj_rope" if use_rope else "in_proj",
    )(*args)
    names = ("aq", "ak", "av", "rq", "rk", "rv", "rg", "z", "xbc", "dt")
    return dict(zip(names, outs))


def _attn_kernel(n_kv, tq, n_sub, *refs):
    q_ref = refs[0]
    kv_refs = refs[1:1 + 2 * n_kv]
    o_ref = refs[1 + 2 * n_kv]
    ts = tq // n_sub
    for c in range(n_sub):
        q = q_ref[c * ts:(c + 1) * ts, :]
        qs = jnp.concatenate([q[:, r * HEAD_DIM:(r + 1) * HEAD_DIM] for r in range(ATT_REP)], axis=0)
        scores = [_dot_nt(qs, kv_refs[2 * s][...]) for s in range(n_kv)]
        m = functools.reduce(jnp.maximum, [jnp.max(s, axis=-1, keepdims=True) for s in scores])
        probs = [jnp.exp2(s - m) for s in scores]
        denom = functools.reduce(jnp.add, [jnp.sum(p, axis=-1, keepdims=True) for p in probs])
        acc = functools.reduce(jnp.add, [_dot(p.astype(BF16), kv_refs[2 * s + 1][...]) for s, p in enumerate(probs)])
        out = acc / denom
        for r in range(ATT_REP):
            o_ref[c * ts:(c + 1) * ts, r * HEAD_DIM:(r + 1) * HEAD_DIM] = out[r * ts:(r + 1) * ts].astype(BF16)


def _attention(q2d, kv_list, batch, tq, n_sub):
    rows = q2d.shape[0]
    lq = rows // batch
    nq = lq // tq
    gw = ATT_REP * HEAD_DIM
    in_specs = [pl.BlockSpec((tq, gw), lambda b, g, i: (b * nq + i, g))]
    args = [q2d]
    for k2d, v2d in kv_list:
        lk = k2d.shape[0] // batch
        spec = pl.BlockSpec((lk, HEAD_DIM), lambda b, g, i: (b, g))
        in_specs += [spec, spec]
        args += [k2d, v2d]
    return pl.pallas_call(
        functools.partial(_attn_kernel, len(kv_list), tq, n_sub),
        grid=(batch, ATT_KV_HEADS, nq),
        in_specs=in_specs,
        out_specs=pl.BlockSpec((tq, gw), lambda b, g, i: (b * nq + i, g)),
        out_shape=jax.ShapeDtypeStruct((rows, ATT_W), BF16),
        compiler_params=_cparams(("parallel", "parallel", "parallel")),
        name="attention",
    )(*args)


_RT_INTRA, _RT_INTER_F, _RT_STATE_F, _RT_DEC_F, _RT_INTER_B, _RT_STATE_B, _RT_DEC_B = range(7)


def _ret_kernel(n_ctx_chunks, n_lat_chunks, ctx_out, *refs):
    (qc_ref, kc_ref, vc_ref, gc_ref, ql_ref, kl_ref, vl_ref, gl_ref, df_ref, db_ref) = refs[:10]
    if ctx_out:
        oc_ref, ol_ref, yc_s, yl_s, sf_s, sb_s, tab_s = refs[10:]
    else:
        ol_ref, yc_s, yl_s, sf_s, sb_s, tab_s = refs[10:]
        oc_ref = None

    ii = lax.broadcasted_iota(jnp.int32, (CHUNK, CHUNK), 0)
    jj = lax.broadcasted_iota(jnp.int32, (CHUNK, CHUNK), 1)
    dist = (ii - jj).astype(F32)
    rowi = ii.astype(F32)
    ones = jnp.ones((CHUNK, CHUNK), F32)
    for hd in range(RET_HEADS):
        lam_f = jnp.log1p(-jnp.exp2(df_ref[hd]))
        lam_b = jnp.log1p(-jnp.exp2(db_ref[hd]))
        tab_s[hd, _RT_INTRA] = (jnp.where(jj <= ii, jnp.exp(dist * lam_f), 0.0)
                                + jnp.where(jj >= ii, jnp.exp(-dist * lam_b), 0.0))
        tab_s[hd, _RT_INTER_F] = jnp.exp((rowi + 1.0) * lam_f)
        tab_s[hd, _RT_STATE_F] = jnp.exp((CHUNK - 1.0 - rowi) * lam_f)
        tab_s[hd, _RT_DEC_F] = ones * jnp.exp(CHUNK * lam_f)
        tab_s[hd, _RT_INTER_B] = jnp.exp((CHUNK - rowi) * lam_b)
        tab_s[hd, _RT_STATE_B] = jnp.exp(rowi * lam_b)
        tab_s[hd, _RT_DEC_B] = ones * jnp.exp(CHUNK * lam_b)

    sf_s[...] = jnp.zeros_like(sf_s)
    sb_s[...] = jnp.zeros_like(sb_s)

    def fwd_chunk(q_ref, k_ref, v_ref, y_ref, r0):
        for hd in range(RET_HEADS):
            cols = slice(hd * RET_DK, (hd + 1) * RET_DK)
            q = q_ref[pl.ds(r0, CHUNK), cols]
            k = k_ref[pl.ds(r0, CHUNK), cols]
            v = v_ref[pl.ds(r0, CHUNK), cols]
            s = _dot_nt(q, k) * tab_s[hd, _RT_INTRA]
            sf = sf_s[hd]
            y = _dot(s.astype(BF16), v) + tab_s[hd, _RT_INTER_F] * _dot(q, sf.astype(BF16))
            y_ref[pl.ds(r0, CHUNK), cols] = y
            vw = (v.astype(F32) * tab_s[hd, _RT_STATE_F]).astype(BF16)
            sf_s[hd] = sf * tab_s[hd, _RT_DEC_F] + _dot_tn(k, vw)

    def bwd_chunk(q_ref, k_ref, v_ref, g_ref, y_ref, o_ref, r0):
        for hd in range(RET_HEADS):
            cols = slice(hd * RET_DK, (hd + 1) * RET_DK)
            k = k_ref[pl.ds(r0, CHUNK), cols]
            v = v_ref[pl.ds(r0, CHUNK), cols]
            sb = sb_s[hd]
            if o_ref is not None:
                q = q_ref[pl.ds(r0, CHUNK), cols]
                y = y_ref[pl.ds(r0, CHUNK), cols] + tab_s[hd, _RT_INTER_B] * _dot(q, sb.astype(BF16))
                gate = _silu(g_ref[pl.ds(r0, CHUNK), cols].astype(F32))
                o_ref[pl.ds(r0, CHUNK), cols] = (_rms(y) * gate).astype(BF16)
            vw = (v.astype(F32) * tab_s[hd, _RT_STATE_B]).astype(BF16)
            sb_s[hd] = sb * tab_s[hd, _RT_DEC_B] + _dot_tn(k, vw)

    for c in range(n_ctx_chunks):
        fwd_chunk(qc_ref, kc_ref, vc_ref, yc_s, c * CHUNK)

    def lat_fwd(c, carry):
        fwd_chunk(ql_ref, kl_ref, vl_ref, yl_s, pl.multiple_of(c * CHUNK, CHUNK))
        return carry

    lax.fori_loop(0, n_lat_chunks, lat_fwd, 0, unroll=SCAN_UNROLL)

    for c in reversed(range(n_ctx_chunks)):
        bwd_chunk(qc_ref, kc_ref, vc_ref, gc_ref, yc_s, oc_ref, c * CHUNK)

    def lat_bwd(t, carry):
        c = n_lat_chunks - 1 - t
        bwd_chunk(ql_ref, kl_ref, vl_ref, gl_ref, yl_s, ol_ref, pl.multiple_of(c * CHUNK, CHUNK))
        return carry

    lax.fori_loop(0, n_lat_chunks, lat_bwd, 0, unroll=SCAN_UNROLL)


def _retention(tc, tl, decay_f, decay_b, batch, ctx_out):
    lc = tc["rq"].shape[0] // batch
    ll = tl["rq"].shape[0] // batch
    cspec = pl.BlockSpec((lc, RET_W), lambda b: (b, 0))
    lspec = pl.BlockSpec((ll, RET_W), lambda b: (b, 0))
    pspec = pl.BlockSpec((RET_HEADS, 1, LANE), lambda b: (0, 0, 0))
    out_specs = [lspec]
    out_shape = [jax.ShapeDtypeStruct((batch * ll, RET_W), BF16)]
    if ctx_out:
        out_specs = [cspec] + out_specs
        out_shape = [jax.ShapeDtypeStruct((batch * lc, RET_W), BF16)] + out_shape
    bcast = lambda p: jnp.broadcast_to(p.astype(F32)[:, None, None], (RET_HEADS, 1, LANE))
    outs = pl.pallas_call(
        functools.partial(_ret_kernel, lc // CHUNK, ll // CHUNK, ctx_out),
        grid=(batch,),
        in_specs=[cspec] * 4 + [lspec] * 4 + [pspec, pspec],
        out_specs=out_specs,
        out_shape=out_shape,
        scratch_shapes=[
            pltpu.VMEM((lc, RET_W), F32), pltpu.VMEM((ll, RET_W), F32),
            pltpu.VMEM((RET_HEADS, RET_DK, RET_DV), F32), pltpu.VMEM((RET_HEADS, RET_DK, RET_DV), F32),
            pltpu.VMEM((RET_HEADS, 7, CHUNK, CHUNK), F32),
        ],
        compiler_params=_cparams(("parallel",)),
        name="retention",
    )(tc["rq"], tc["rk"], tc["rv"], tc["rg"], tl["rq"], tl["rk"], tl["rv"], tl["rg"],
      bcast(decay_f), bcast(decay_b))
    return (outs[0], outs[1]) if ctx_out else (None, outs[0])


def _expand_heads(v, e_ref):
    hi = v.astype(BF16)
    lo = (v - hi.astype(F32)).astype(BF16)
    e = e_ref[...]
    return _dot(hi, e) + _dot(lo, e)


def _cumsum_rows(v):
    rows = lax.broadcasted_iota(jnp.int32, v.shape, 0)
    s = 1
    while s < v.shape[0]:
        v = v + jnp.where(rows >= s, pltpu.roll(v, s, 0), 0.0)
        s *= 2
    return v


def _ssd_kernel(n_ctx_chunks, n_lat_chunks, ctx_out, *refs):
    (xc_ref, dtc_ref, zc_ref, xl_ref, dtl_ref, zl_ref,
     cw_ref, cb_ref, bias_ref, alog_ref, skip_ref, ng_ref, ef_ref, eb_ref, shift_ref) = refs[:15]
    if ctx_out:
        oc_ref, ol_ref, tok_s, y_s, sf_s, sb_s = refs[15:]
    else:
        ol_ref, tok_s, y_s, sf_s, sb_s = refs[15:]
        oc_ref = None
    lc = n_ctx_chunks * CHUNK

    ii = lax.broadcasted_iota(jnp.int32, (CHUNK, CHUNK), 0)
    jj = lax.broadcasted_iota(jnp.int32, (CHUNK, CHUNK), 1)
    lower = jj < ii
    upper = jj > ii
    neg_a = -jnp.exp(alog_ref[...])
    halo = CONV_HALO

    def conv_silu(x_ref, c, n_chunks):
        if isinstance(c, int):
            r0 = c * CHUNK
            prev0 = max(r0 - halo, 0)
            next0 = min(r0 + CHUNK, (n_chunks - 1) * CHUNK)
        else:
            r0 = pl.multiple_of(c * CHUNK, CHUNK)
            prev0 = pl.multiple_of(jnp.maximum(r0 - halo, 0), halo)
            next0 = pl.multiple_of(jnp.minimum(r0 + CHUNK, (n_chunks - 1) * CHUNK), halo)
        first = c == 0
        last = c == n_chunks - 1
        centre = x_ref[pl.ds(r0, CHUNK), :]
        prev = x_ref[pl.ds(prev0, halo), :]
        nxt = x_ref[pl.ds(next0, halo), :]
        prev = jnp.where(first, jnp.zeros_like(prev), prev)
        nxt = jnp.where(last, jnp.zeros_like(nxt), nxt)
        pad = jnp.zeros((CONV_EXT - CHUNK - 2 * halo, CONV_CH), BF16)
        ext = jnp.concatenate([centre, prev, nxt, pad], axis=0)
        acc = cb_ref[...] + cw_ref[SSD_CONV // 2:SSD_CONV // 2 + 1, :] * centre.astype(F32)
        taps = [k for k in range(SSD_CONV) if k != SSD_CONV // 2]
        for n, k in enumerate(taps):
            acc = acc + cw_ref[k:k + 1, :] * _dot(shift_ref[n], ext)
        return _silu(acc)

    def decays(dt_ref, r0):
        dt = jax.nn.softplus(dt_ref[pl.ds(r0, CHUNK), :] + bias_ref[...])
        la = dt * neg_a
        a_inc = _cumsum_rows(la)
        total = a_inc[CHUNK - 1:CHUNK, :]
        return dt, la, a_inc, total

    def decay_row(total, e_ref):
        return _expand_heads(jnp.broadcast_to(jnp.exp(total), (8, LANE)), e_ref)[0:1, :]

    def fwd_chunk(x_ref, dt_ref, c, n_chunks, base):
        if isinstance(c, int):
            r0 = c * CHUNK
            t0 = base + r0
        else:
            r0 = pl.multiple_of(c * CHUNK, CHUNK)
            t0 = pl.multiple_of(base + r0, CHUNK)
        tok = conv_silu(x_ref, c, n_chunks)
        tok_b = tok.astype(BF16)
        tok_s[pl.ds(t0, CHUNK), :] = tok_b
        xs = tok_b[:, :SSD_W]
        dt, la, a_inc, total = decays(dt_ref, r0)
        a_exc = a_inc - la
        log2e = math.log2(math.e)
        ldt = jnp.log2(dt)
        col_term = a_inc * log2e
        col_term_b = a_exc * log2e
        row_t = (jnp.where(lax.broadcasted_iota(jnp.int32, (CHUNK, LANE), 1) < SSD_HEADS,
                           ldt - col_term, ldt + col_term_b)).T
        dt_t = dt.T
        ys = []
        inter = []
        for g in range(SSD_GROUPS):
            bs = tok_b[:, SSD_W + g * SSD_STATE:SSD_W + (g + 1) * SSD_STATE]
            cs = tok_b[:, SSD_W + SSD_BC + g * SSD_STATE:SSD_W + SSD_BC + (g + 1) * SSD_STATE]
            gmat = _dot_nt(cs, bs)
            for pair in range(HEADS_PER_GROUP // 2):
                ms = []
                for sub in range(2):
                    hd = g * HEADS_PER_GROUP + 2 * pair + sub
                    hb = SSD_HEADS + hd
                    diag = jnp.log2(dt_t[hd:hd + 1, :] + dt_t[hb:hb + 1, :])
                    expo = jnp.where(lower, col_term[:, hd:hd + 1] + row_t[hd:hd + 1, :],
                                     jnp.where(upper, row_t[hb:hb + 1, :] - col_term_b[:, hb:hb + 1], diag))
                    ms.append((gmat * jnp.exp2(expo)).astype(BF16))
                col = (g * HEADS_PER_GROUP + 2 * pair) * SSD_HEAD_DIM
                xpair = xs[:, col:col + 2 * SSD_HEAD_DIM]
                lane = lax.broadcasted_iota(jnp.int32, xpair.shape, 1)
                zero = jnp.zeros_like(xpair)
                rhs = jnp.concatenate([jnp.where(lane < SSD_HEAD_DIM, xpair, zero),
                                       jnp.where(lane < SSD_HEAD_DIM, zero, xpair)], axis=0)
                ys.append(_dot(jnp.concatenate(ms, axis=1), rhs))
            inter.append(_dot(cs, sf_s[:, g * GROUP_W:(g + 1) * GROUP_W].astype(BF16)))
        y = jnp.concatenate(ys, axis=1) + _expand_heads(jnp.exp(a_inc), ef_ref) * jnp.concatenate(inter, axis=1)
        y_s[pl.ds(t0, CHUNK), :] = y
        xw = (xs.astype(F32) * _expand_heads(jnp.exp(total - a_inc) * dt, ef_ref)).astype(BF16)
        dec = decay_row(total, ef_ref)
        for g in range(SSD_GROUPS):
            bs = tok_b[:, SSD_W + g * SSD_STATE:SSD_W + (g + 1) * SSD_STATE]
            sl = slice(g * GROUP_W, (g + 1) * GROUP_W)
            sf_s[:, sl] = sf_s[:, sl] * dec[:, sl] + _dot_tn(bs, xw[:, sl])

    def bwd_chunk(dt_ref, z_ref, o_ref, c, base):
        if isinstance(c, int):
            r0 = c * CHUNK
            t0 = base + r0
        else:
            r0 = pl.multiple_of(c * CHUNK, CHUNK)
            t0 = pl.multiple_of(base + r0, CHUNK)
        tok_b = tok_s[pl.ds(t0, CHUNK), :]
        xs = tok_b[:, :SSD_W].astype(F32)
        dt, la, a_inc, total = decays(dt_ref, r0)
        a_exc = a_inc - la
        if o_ref is not None:
            inter = []
            for g in range(SSD_GROUPS):
                cs = tok_b[:, SSD_W + SSD_BC + g * SSD_STATE:SSD_W + SSD_BC + (g + 1) * SSD_STATE]
                inter.append(_dot(cs, sb_s[:, g * GROUP_W:(g + 1) * GROUP_W].astype(BF16)))
            y = y_s[pl.ds(t0, CHUNK), :] + _expand_heads(jnp.exp(total - a_exc), eb_ref) * jnp.concatenate(inter, axis=1)
            y = (y + skip_ref[...] * xs) * _silu(z_ref[pl.ds(r0, CHUNK), :].astype(F32))
            o_ref[pl.ds(r0, CHUNK), :] = (_rms(y) * ng_ref[...]).astype(BF16)
        xw = (xs * _expand_heads(jnp.exp(a_exc) * dt, eb_ref)).astype(BF16)
        dec = decay_row(total, eb_ref)
        for g in range(SSD_GROUPS):
            bs = tok_b[:, SSD_W + g * SSD_STATE:SSD_W + (g + 1) * SSD_STATE]
            sl = slice(g * GROUP_W, (g + 1) * GROUP_W)
            sb_s[:, sl] = sb_s[:, sl] * dec[:, sl] + _dot_tn(bs, xw[:, sl])

    sf_s[...] = jnp.zeros_like(sf_s)
    sb_s[...] = jnp.zeros_like(sb_s)

    for c in range(n_ctx_chunks):
        fwd_chunk(xc_ref, dtc_ref, c, n_ctx_chunks, 0)

    def lat_fwd(c, carry):
        fwd_chunk(xl_ref, dtl_ref, c, n_lat_chunks, lc)
        return carry

    lax.fori_loop(0, n_lat_chunks, lat_fwd, 0, unroll=SCAN_UNROLL)

    for c in reversed(range(n_ctx_chunks)):
        bwd_chunk(dtc_ref, zc_ref, oc_ref, c, 0)

    def lat_bwd(t, carry):
        bwd_chunk(dtl_ref, zl_ref, ol_ref, n_lat_chunks - 1 - t, lc)
        return carry

    lax.fori_loop(0, n_lat_chunks, lat_bwd, 0, unroll=SCAN_UNROLL)


def _conv_shift_matrices():
    taps = [k for k in range(SSD_CONV) if k != SSD_CONV // 2]
    mats = np.zeros((len(taps), CHUNK, CONV_EXT), np.float32)
    for n, k in enumerate(taps):
        for i in range(CHUNK):
            src = i + k - SSD_CONV // 2
            if src < 0:
                col = CHUNK + CONV_HALO + src
            elif src >= CHUNK:
                col = CHUNK + CONV_HALO + (src - CHUNK)
            else:
                col = src
            mats[n, i, col] = 1.0
    return mats


def _head_lane_vec(f, b):
    v = jnp.zeros((1, LANE), F32)
    v = v.at[0, :SSD_HEADS].set(f.astype(F32))
    return v.at[0, SSD_HEADS:2 * SSD_HEADS].set(b.astype(F32))


def _ssd(tc, tl, p, batch, ctx_out):
    lc = tc["xbc"].shape[0] // batch
    ll = tl["xbc"].shape[0] // batch
    rows = lambda n, w: pl.BlockSpec((n, w), lambda b: (b, 0))
    vec = lambda w: pl.BlockSpec((1, w), lambda b: (0, 0))
    whole = lambda a: pl.BlockSpec(a.shape, lambda b: (0,) * a.ndim)
    head_of_col = np.arange(SSD_W) // SSD_HEAD_DIM
    ef = jnp.asarray(np.arange(LANE)[:, None] == head_of_col[None, :], BF16)
    eb = jnp.asarray(np.arange(LANE)[:, None] == head_of_col[None, :] + SSD_HEADS, BF16)
    shift = jnp.asarray(_conv_shift_matrices(), BF16)
    conv_w = p["conv_w"].astype(F32)
    conv_b = p["conv_b"].astype(F32).reshape(1, CONV_CH)
    bias = _head_lane_vec(p["dt_bias_f"], p["dt_bias_b"])
    alog = _head_lane_vec(p["a_log_f"], p["a_log_b"])
    skip = jnp.repeat(p["d_skip"].astype(F32), SSD_HEAD_DIM).reshape(1, SSD_W)
    norm_g = p["ssd_norm_g"].astype(F32).reshape(1, SSD_W)
    out_specs = [rows(ll, SSD_W)]
    out_shape = [jax.ShapeDtypeStruct((batch * ll, SSD_W), BF16)]
    if ctx_out:
        out_specs = [rows(lc, SSD_W)] + out_specs
        out_shape = [jax.ShapeDtypeStruct((batch * lc, SSD_W), BF16)] + out_shape
    outs = pl.pallas_call(
        functools.partial(_ssd_kernel, lc // CHUNK, ll // CHUNK, ctx_out),
        grid=(batch,),
        in_specs=[rows(lc, CONV_CH), rows(lc, LANE), rows(lc, SSD_W),
                  rows(ll, CONV_CH), rows(ll, LANE), rows(ll, SSD_W),
                  whole(conv_w), vec(CONV_CH), vec(LANE), vec(LANE), vec(SSD_W), vec(SSD_W),
                  whole(ef), whole(eb), whole(shift)],
        out_specs=out_specs,
        out_shape=out_shape,
        scratch_shapes=[
            pltpu.VMEM((lc + ll, CONV_CH), BF16), pltpu.VMEM((lc + ll, SSD_W), F32),
            pltpu.VMEM((SSD_STATE, SSD_W), F32), pltpu.VMEM((SSD_STATE, SSD_W), F32),
        ],
        compiler_params=_cparams(("parallel",)),
        name="ssd_scan",
    )(tc["xbc"], tc["dt"], tc["z"], tl["xbc"], tl["dt"], tl["z"],
      conv_w, conv_b, bias, alog, skip, norm_g, ef, eb, shift)
    return (outs[0], outs[1]) if ctx_out else (None, outs[0])


def _out_proj_kernel(n_sub, att_ref, ret_ref, ssd_ref, w_ref, x_ref, g1_ref, pmg_ref, x1_ref):
    ts = x_ref.shape[0] // n_sub
    for c in range(n_sub):
        rows = slice(c * ts, (c + 1) * ts)
        m = (_dot(att_ref[rows, :], w_ref[0:ATT_W, :])
             + _dot(ret_ref[rows, :], w_ref[ATT_W:ATT_W + RET_W, :])
             + _dot(ssd_ref[rows, :], w_ref[ATT_W + RET_W:, :]))
        x1_ref[rows, :] = x_ref[rows, :] + g1_ref[...] * (_rms(m) * pmg_ref[...])


def _out_proj(att, ret, ssd, w_out_b, x2d, mod4, group_of_tile, post_mix_g, tm):
    rows, d = x2d.shape
    row_spec = lambda w: pl.BlockSpec((tm, w), lambda i: (i, 0))
    mod_spec = lambda k: pl.BlockSpec((None, None, 1, d), lambda i: (group_of_tile(i), k, 0, 0))
    vec_spec = pl.BlockSpec((1, d), lambda i: (0, 0))
    n_sub = 2 if tm % 32 == 0 else 1
    return pl.pallas_call(
        functools.partial(_out_proj_kernel, n_sub),
        grid=(rows // tm,),
        in_specs=[row_spec(ATT_W), row_spec(RET_W), row_spec(SSD_W),
                  pl.BlockSpec(w_out_b.shape, lambda i: (0, 0)),
                  row_spec(d), mod_spec(2), vec_spec],
        out_specs=row_spec(d),
        out_shape=jax.ShapeDtypeStruct((rows, d), F32),
        compiler_params=_cparams(("parallel",)),
        name="out_proj",
    )(att, ret, ssd, w_out_b, x2d, mod4, post_mix_g)


def _ffn_kernel(x1_ref, sc_ref, sh_ref, pfg_ref, wg_ref, wu_ref, wd_ref, g2_ref, pg_ref, o_ref, hf_s):
    j = pl.program_id(1)

    tm = o_ref.shape[0]
    tr = _pick_tile(tm, FFN_EPILOGUE_ROWS)

    @pl.when(j == 0)
    def _():
        for r in range(tm // tr):
            rows = slice(r * tr, (r + 1) * tr)
            hf_s[rows, :] = (_rms(x1_ref[rows, :]) * pfg_ref[...] * (1.0 + sc_ref[...]) + sh_ref[...]).astype(BF16)
        o_ref[...] = jnp.zeros_like(o_ref)

    hf = hf_s[...]
    a = (_silu(_dot(hf, wg_ref[...])) * _dot(hf, wu_ref[...])).astype(BF16)
    n_out = o_ref.shape[1]
    tn = _pick_tile(n_out, FFN_DOWN_COLS)
    for n in range(n_out // tn):
        o_ref[:, n * tn:(n + 1) * tn] += _dot(a, wd_ref[:, n * tn:(n + 1) * tn])

    @pl.when(j == pl.num_programs(1) - 1)
    def _():
        for r in range(tm // tr):
            rows = slice(r * tr, (r + 1) * tr)
            o_ref[rows, :] = x1_ref[rows, :] + g2_ref[...] * (_rms(o_ref[rows, :]) * pg_ref[...])


def _ffn(x1, wg, wu, wd, mod4, group_of_tile, pre_ffn_g, post_ffn_g, tm, tf):
    rows, d = x1.shape
    dff = wg.shape[1]
    mod_spec = lambda k: pl.BlockSpec((None, None, 1, d), lambda i, j: (group_of_tile(i), k, 0, 0))
    vec_spec = pl.BlockSpec((1, d), lambda i, j: (0, 0))
    return pl.pallas_call(
        _ffn_kernel,
        grid=(rows // tm, dff // tf),
        in_specs=[
            pl.BlockSpec((tm, d), lambda i, j: (i, 0)),
            mod_spec(4), mod_spec(3), vec_spec,
            pl.BlockSpec((d, tf), lambda i, j: (0, j)),
            pl.BlockSpec((d, tf), lambda i, j: (0, j)),
            pl.BlockSpec((tf, d), lambda i, j: (j, 0)),
            mod_spec(5), vec_spec,
        ],
        out_specs=pl.BlockSpec((tm, d), lambda i, j: (i, 0)),
        out_shape=jax.ShapeDtypeStruct((rows, d), F32),
        scratch_shapes=[pltpu.VMEM((tm, d), BF16)],
        compiler_params=_cparams(("parallel", "arbitrary")),
        name="swiglu_ffn",
    )(x1, mod4, mod4, pre_ffn_g, wg, wu, wd, mod4, post_ffn_g)


def _cast_kernel(w_ref, o_ref):
    o_ref[...] = w_ref[...].astype(BF16)


def _cast_layer(w_stack, layer, n_cols):
    _, rows, _ = w_stack.shape
    tr = _pick_tile(rows, 256)
    return pl.pallas_call(
        _cast_kernel,
        grid=(rows // tr,),
        in_specs=[pl.BlockSpec((None, tr, n_cols), lambda i: (layer, i, 0))],
        out_specs=pl.BlockSpec((tr, n_cols), lambda i: (i, 0)),
        out_shape=jax.ShapeDtypeStruct((rows, n_cols), BF16),
        compiler_params=_cparams(("parallel",)),
        name="cast_weight",
    )(w_stack)


def _rope_tables(seq_len):
    rows = seq_len // GRID_W
    row = jnp.repeat(jnp.arange(rows, dtype=F32), GRID_W)
    col = jnp.tile(jnp.arange(GRID_W, dtype=F32), rows)
    n_freq = HEAD_DIM // 4
    inv = ROPE_THETA ** (-jnp.arange(n_freq, dtype=F32) / n_freq)
    ang = jnp.concatenate([row[:, None] * inv, col[:, None] * inv], axis=-1)
    cos, sin = jnp.cos(ang), jnp.sin(ang)
    cos_t = jnp.repeat(cos, 2, axis=-1)
    sin_t = jnp.stack([-sin, sin], axis=-1).reshape(seq_len, HEAD_DIM)
    return cos_t, sin_t


def _pick_tile(n, target):
    t = min(n, target)
    while n % t:
        t //= 2
    return t


def _layer_weights(l, w_in, w_out, w_gate, w_up, w_down):
    w_in_b = _cast_layer(w_in, l, OFF_DT)
    w_dt_b = jnp.pad(w_in[l, :, OFF_DT:].astype(BF16), ((0, 0), (0, LANE - 2 * SSD_HEADS)))
    return (w_in_b, w_dt_b, _cast_layer(w_out, l, w_out.shape[2]), _cast_layer(w_gate, l, w_gate.shape[2]),
            _cast_layer(w_up, l, w_up.shape[2]), _cast_layer(w_down, l, w_down.shape[2]))


def kernel(x, c, ctx, c_ctx, w_mod, b_mod, pre_mix_g, post_mix_g, pre_ffn_g, post_ffn_g, w_in, q_norm_g, k_norm_g, ret_decay_f, ret_decay_b, conv_w, conv_b, dt_bias_f, dt_bias_b, a_log_f, a_log_b, d_skip, ssd_norm_g, w_out, w_gate, w_up, w_down):
    batch, seq, d = x.shape
    lc = ctx.shape[1]
    depth = w_mod.shape[0]
    assert batch < N_MOD_ROWS and seq % CHUNK == 0 and lc % CHUNK == 0 and seq % GRID_W == 0

    c16 = jnp.zeros((N_MOD_ROWS, d), F32).at[:batch].set(c.astype(F32)).at[batch].set(c_ctx.astype(F32))
    mod_all = _modulation(c16, w_mod.astype(F32), b_mod.astype(F32))
    rope_tabs = _rope_tables(seq)

    tm_lat = _pick_tile(seq, 512)
    tm_ctx = _pick_tile(batch * lc, 512)
    tm_ffn_lat = _pick_tile(seq, 1024)
    tm_ffn_ctx = _pick_tile(batch * lc, 1024)
    lat_group = lambda tm: (lambda i: i // (seq // tm))
    ctx_group = lambda tm: (lambda i: batch)
    tq_lat = _pick_tile(seq, 512)
    tq_ctx = _pick_tile(lc, 256)
    sub_lat = 4 if tq_lat % 64 == 0 else 1
    sub_ctx = 2 if tq_ctx % 32 == 0 else 1
    tf = _pick_tile(w_gate.shape[2], 512)

    xl = x.reshape(batch * seq, d).astype(F32)
    xc = ctx.reshape(batch * lc, d).astype(F32)
    vec = lambda v: v.astype(F32).reshape(1, d)

    for l in range(depth):
        last = l == depth - 1
        w_in_b, w_dt_b, w_out_b, wg, wu, wd = _layer_weights(l, w_in, w_out, w_gate, w_up, w_down)
        qg = q_norm_g[l].astype(F32).reshape(1, HEAD_DIM)
        kg = k_norm_g[l].astype(F32).reshape(1, HEAD_DIM)
        mod4 = mod_all[l].reshape(N_MOD_ROWS, 6, 1, d)
        p = dict(conv_w=conv_w[l], conv_b=conv_b[l], dt_bias_f=dt_bias_f[l], dt_bias_b=dt_bias_b[l],
                 a_log_f=a_log_f[l], a_log_b=a_log_b[l], d_skip=d_skip[l], ssd_norm_g=ssd_norm_g[l])

        tl = _in_proj(xl, mod4, lat_group(tm_lat), vec(pre_mix_g[l]), w_in_b, w_dt_b, qg, kg, rope_tabs, tm_lat)
        tc = _in_proj(xc, mod4, ctx_group(tm_ctx), vec(pre_mix_g[l]), w_in_b, w_dt_b, qg, kg, None, tm_ctx)

        att_l = _attention(tl["aq"], [(tc["ak"], tc["av"]), (tl["ak"], tl["av"])], batch, tq_lat, sub_lat)
        ret_c, ret_l = _retention(tc, tl, ret_decay_f[l], ret_decay_b[l], batch, not last)
        ssd_c, ssd_l = _ssd(tc, tl, p, batch, not last)

        x1 = _out_proj(att_l, ret_l, ssd_l, w_out_b, xl, mod4, lat_group(tm_lat), vec(post_mix_g[l]), tm_lat)
        xl = _ffn(x1, wg, wu, wd, mod4, lat_group(tm_ffn_lat), vec(pre_ffn_g[l]), vec(post_ffn_g[l]),
                  tm_ffn_lat, tf)

        if not last:
            att_c = _attention(tc["aq"], [(tc["ak"], tc["av"])], batch, tq_ctx, sub_ctx)
            x1c = _out_proj(att_c, ret_c, ssd_c, w_out_b, xc, mod4, ctx_group(tm_ctx), vec(post_mix_g[l]), tm_ctx)
            xc = _ffn(x1c, wg, wu, wd, mod4, ctx_group(tm_ffn_ctx), vec(pre_ffn_g[l]), vec(post_ffn_g[l]),
                      tm_ffn_ctx, tf)

    return xl.reshape(batch, seq, d).astype(x.dtype)
```

```python
import functools
import math

import numpy as np
import jax
import jax.numpy as jnp
from jax import lax
from jax.experimental import pallas as pl
from jax.experimental.pallas import tpu as pltpu

F32 = jnp.float32
BF16 = jnp.bfloat16

GRID_W = 64
HEAD_DIM = 128
ATT_HEADS = 6
ATT_KV_HEADS = 2
ATT_REP = ATT_HEADS // ATT_KV_HEADS
RET_HEADS = 4
RET_DK = 128
RET_DV = 128
SSD_HEADS = 12
SSD_HEAD_DIM = 64
SSD_GROUPS = 2
SSD_STATE = 128
SSD_CONV = 5
ATT_W = ATT_HEADS * HEAD_DIM
KV_W = ATT_KV_HEADS * HEAD_DIM
RET_QK_W = RET_HEADS * RET_DK
RET_W = RET_HEADS * RET_DV
SSD_W = SSD_HEADS * SSD_HEAD_DIM
SSD_BC = SSD_GROUPS * SSD_STATE
CONV_CH = SSD_W + 2 * SSD_BC
GROUP_W = SSD_W // SSD_GROUPS
HEADS_PER_GROUP = SSD_HEADS // SSD_GROUPS
CHUNK = 128
ROPE_THETA = 10000.0
EPS = 1e-6
LANE = 128
FFN_DOWN_COLS = 512
FFN_EPILOGUE_ROWS = 128
FFN_SUB_ROWS = 256
SCAN_UNROLL = 2
CONV_HALO = 16
CONV_EXT = 256

OFF_AQ = 0
OFF_AK = OFF_AQ + ATT_W
OFF_AV = OFF_AK + KV_W
OFF_RQ = OFF_AV + KV_W
OFF_RK = OFF_RQ + RET_QK_W
OFF_RV = OFF_RK + RET_QK_W
OFF_RG = OFF_RV + RET_W
OFF_Z = OFF_RG + RET_W
OFF_XBC = OFF_Z + SSD_W
OFF_DT = OFF_XBC + CONV_CH
D_IN = OFF_DT + 2 * SSD_HEADS
D_IN_PAD = OFF_DT + LANE

VMEM_LIMIT = 60 * 1024 * 1024
N_MOD_ROWS = 16


def _cparams(sem):
    return pltpu.CompilerParams(dimension_semantics=sem, vmem_limit_bytes=VMEM_LIMIT)


def _silu(v):
    return v * jax.nn.sigmoid(v)


def _rms(v):
    return v * lax.rsqrt(jnp.mean(v * v, axis=-1, keepdims=True) + EPS)


def _dot(a, b):
    return jnp.dot(a, b, preferred_element_type=F32)


def _dot_nt(a, b):
    return lax.dot_general(a, b, (((1,), (1,)), ((), ())), preferred_element_type=F32)


def _dot_tn(a, b):
    return lax.dot_general(a, b, (((0,), (0,)), ((), ())), preferred_element_type=F32)


def _mod_kernel(c_ref, w_ref, b_ref, o_ref):
    a = _silu(c_ref[...]).astype(BF16)
    o_ref[...] = _dot(a, w_ref[...].astype(BF16)) + b_ref[...]


def _modulation(c16, w_mod, b_mod):
    depth, d, n = w_mod.shape
    tn = _pick_tile(n, 1024)
    return pl.pallas_call(
        _mod_kernel,
        grid=(depth, n // tn),
        in_specs=[
            pl.BlockSpec((N_MOD_ROWS, d), lambda l, j: (0, 0)),
            pl.BlockSpec((None, d, tn), lambda l, j: (l, 0, j)),
            pl.BlockSpec((None, 1, tn), lambda l, j: (l, 0, j)),
        ],
        out_specs=pl.BlockSpec((None, N_MOD_ROWS, tn), lambda l, j: (l, 0, j)),
        out_shape=jax.ShapeDtypeStruct((depth, N_MOD_ROWS, n), F32),
        compiler_params=_cparams(("parallel", "parallel")),
        name="adaln_modulation",
    )(c16, w_mod, b_mod.reshape(depth, 1, n))


def _in_proj_kernel(use_rope, *refs):
    if use_rope:
        (x_ref, sc_ref, sh_ref, g_ref, w_ref, wdt_ref, qg_ref, kg_ref, cos_ref, sin_ref,
         aq_ref, ak_ref, av_ref, rq_ref, rk_ref, rv_ref, rg_ref, z_ref, xbc_ref, dt_ref) = refs
        cos = cos_ref[...]
        sin = sin_ref[...]
        even_lane = (lax.broadcasted_iota(jnp.int32, cos.shape, 1) & 1) == 0
    else:
        (x_ref, sc_ref, sh_ref, g_ref, w_ref, wdt_ref, qg_ref, kg_ref,
         aq_ref, ak_ref, av_ref, rq_ref, rk_ref, rv_ref, rg_ref, z_ref, xbc_ref, dt_ref) = refs

    x = x_ref[...]
    h = (_rms(x) * g_ref[...] * (1.0 + sc_ref[...]) + sh_ref[...]).astype(BF16)

    def proj(lo, width):
        return _dot(h, w_ref[:, lo:lo + width])

    def rope(t):
        if not use_rope:
            return t
        partner = jnp.where(even_lane, pltpu.roll(t, HEAD_DIM - 1, 1), pltpu.roll(t, 1, 1))
        return t * cos + partner * sin

    att_scale = HEAD_DIM ** -0.5 * math.log2(math.e)
    ret_scale = RET_DK ** -0.5

    acc = proj(OFF_AQ, ATT_W)
    for hd in range(ATT_HEADS):
        t = acc[:, hd * HEAD_DIM:(hd + 1) * HEAD_DIM]
        t = rope(_rms(t) * qg_ref[...]) * att_scale
        aq_ref[:, hd * HEAD_DIM:(hd + 1) * HEAD_DIM] = t.astype(BF16)

    acc = proj(OFF_AK, KV_W)
    for hd in range(ATT_KV_HEADS):
        t = acc[:, hd * HEAD_DIM:(hd + 1) * HEAD_DIM]
        t = rope(_rms(t) * kg_ref[...])
        ak_ref[:, hd * HEAD_DIM:(hd + 1) * HEAD_DIM] = t.astype(BF16)

    av_ref[...] = proj(OFF_AV, KV_W).astype(BF16)

    acc = proj(OFF_RQ, RET_QK_W)
    for hd in range(RET_HEADS):
        t = rope(acc[:, hd * RET_DK:(hd + 1) * RET_DK])
        rq_ref[:, hd * RET_DK:(hd + 1) * RET_DK] = t.astype(BF16)

    acc = proj(OFF_RK, RET_QK_W)
    for hd in range(RET_HEADS):
        t = rope(acc[:, hd * RET_DK:(hd + 1) * RET_DK] * ret_scale)
        rk_ref[:, hd * RET_DK:(hd + 1) * RET_DK] = t.astype(BF16)

    rv_ref[...] = proj(OFF_RV, RET_W).astype(BF16)
    rg_ref[...] = proj(OFF_RG, RET_W).astype(BF16)
    z_ref[...] = proj(OFF_Z, SSD_W).astype(BF16)
    xbc_ref[...] = proj(OFF_XBC, CONV_CH).astype(BF16)
    dt_ref[...] = _dot(h, wdt_ref[...])


def _in_proj(x2d, mod4, group_of_tile, pre_g, w_in_b, w_dt_b, qg, kg, rope_tabs, tm):
    rows, d = x2d.shape
    nt = rows // tm
    use_rope = rope_tabs is not None
    row_spec = lambda w: pl.BlockSpec((tm, w), lambda i: (i, 0))
    mod_spec = lambda k: pl.BlockSpec((None, None, 1, d), lambda i: (group_of_tile(i), k, 0, 0))
    vec_spec = lambda w: pl.BlockSpec((1, w), lambda i: (0, 0))
    in_specs = [
        row_spec(d), mod_spec(1), mod_spec(0), vec_spec(d),
        pl.BlockSpec((d, OFF_DT), lambda i: (0, 0), pipeline_mode=pl.Buffered(1)),
        pl.BlockSpec((d, LANE), lambda i: (0, 0)),
        vec_spec(HEAD_DIM), vec_spec(HEAD_DIM),
    ]
    args = [x2d, mod4, mod4, pre_g, w_in_b, w_dt_b, qg, kg]
    if use_rope:
        cos_t, sin_t = rope_tabs
        tiles_per_seq = cos_t.shape[0] // tm
        tab_spec = pl.BlockSpec((tm, HEAD_DIM), lambda i: (i % tiles_per_seq, 0))
        in_specs += [tab_spec, tab_spec]
        args += [cos_t, sin_t]
    widths = (ATT_W, KV_W, KV_W, RET_QK_W, RET_QK_W, RET_W, RET_W, SSD_W, CONV_CH)
    out_specs = [row_spec(w) for w in widths] + [row_spec(LANE)]
    out_shape = [jax.ShapeDtypeStruct((rows, w), BF16) for w in widths]
    out_shape.append(jax.ShapeDtypeStruct((rows, LANE), F32))
    outs = pl.pallas_call(
        functools.partial(_in_proj_kernel, use_rope),
        grid=(nt,),
        in_specs=in_specs,
        out_specs=out_specs,
        out_shape=out_shape,
        compiler_params=_cparams(("parallel",)),
        name="in_proj_rope" if use_rope else "in_proj",
    )(*args)
    names = ("aq", "ak", "av", "rq", "rk", "rv", "rg", "z", "xbc", "dt")
    return dict(zip(names, outs))


def _attn_kernel(n_kv, tq, n_sub, *refs):
    q_ref = refs[0]
    kv_refs = refs[1:1 + 2 * n_kv]
    o_ref = refs[1 + 2 * n_kv]
    ts = tq // n_sub
    all_scores = []
    for c in range(n_sub):
        q = q_ref[c * ts:(c + 1) * ts, :]
        qs = jnp.concatenate([q[:, r * HEAD_DIM:(r + 1) * HEAD_DIM] for r in range(ATT_REP)], axis=0)
        all_scores.append([_dot_nt(qs, kv_refs[2 * s][...]) for s in range(n_kv)])
    v_ones = []
    for s in range(n_kv):
        v = kv_refs[2 * s + 1][...]
        v_ones.append(jnp.concatenate([v, jnp.ones_like(v)], axis=1))
    for c in range(n_sub):
        scores = all_scores[c]
        m = functools.reduce(jnp.maximum, [jnp.max(s, axis=-1, keepdims=True) for s in scores])
        acc = functools.reduce(jnp.add, [_dot(jnp.exp2(s - m).astype(BF16), v_ones[i]) for i, s in enumerate(scores)])
        out = acc[:, :HEAD_DIM] / acc[:, HEAD_DIM:]
        for r in range(ATT_REP):
            o_ref[c * ts:(c + 1) * ts, r * HEAD_DIM:(r + 1) * HEAD_DIM] = out[r * ts:(r + 1) * ts].astype(BF16)


def _attention(q2d, kv_list, batch, tq, n_sub):
    rows = q2d.shape[0]
    lq = rows // batch
    nq = lq // tq
    gw = ATT_REP * HEAD_DIM
    in_specs = [pl.BlockSpec((tq, gw), lambda b, g, i: (b * nq + i, g))]
    args = [q2d]
    for k2d, v2d in kv_list:
        lk = k2d.shape[0] // batch
        spec = pl.BlockSpec((lk, HEAD_DIM), lambda b, g, i: (b, g))
        in_specs += [spec, spec]
        args += [k2d, v2d]
    return pl.pallas_call(
        functools.partial(_attn_kernel, len(kv_list), tq, n_sub),
        grid=(batch, ATT_KV_HEADS, nq),
        in_specs=in_specs,
        out_specs=pl.BlockSpec((tq, gw), lambda b, g, i: (b * nq + i, g)),
        out_shape=jax.ShapeDtypeStruct((rows, ATT_W), BF16),
        compiler_params=_cparams(("parallel", "parallel", "parallel")),
        name="attention",
    )(*args)


_RT_INTRA, _RT_INTER_F, _RT_STATE_F, _RT_DEC_F, _RT_INTER_B, _RT_STATE_B, _RT_DEC_B = range(7)


def _ret_kernel(n_ctx_chunks, n_lat_chunks, ctx_out, *refs):
    (qc_ref, kc_ref, vc_ref, gc_ref, ql_ref, kl_ref, vl_ref, gl_ref, df_ref, db_ref) = refs[:10]
    if ctx_out:
        oc_ref, ol_ref, yc_s, yl_s, sf_s, sb_s, tab_s, ds_s = refs[10:]
    else:
        ol_ref, yc_s, yl_s, sf_s, sb_s, tab_s, ds_s = refs[10:]
        oc_ref = None

    ii = lax.broadcasted_iota(jnp.int32, (CHUNK, CHUNK), 0)
    jj = lax.broadcasted_iota(jnp.int32, (CHUNK, CHUNK), 1)
    dist = (ii - jj).astype(F32)
    rowi = ii.astype(F32)
    ones = jnp.ones((CHUNK, CHUNK), F32)
    for hd in range(RET_HEADS):
        lam_f = jnp.log1p(-jnp.exp2(df_ref[hd]))
        lam_b = jnp.log1p(-jnp.exp2(db_ref[hd]))
        tab_s[hd, _RT_INTRA] = (jnp.where(jj <= ii, jnp.exp(dist * lam_f), 0.0)
                                + jnp.where(jj >= ii, jnp.exp(-dist * lam_b), 0.0))
        tab_s[hd, _RT_INTER_F] = jnp.exp((rowi + 1.0) * lam_f)
        tab_s[hd, _RT_STATE_F] = jnp.exp((CHUNK - 1.0 - rowi) * lam_f)
        tab_s[hd, _RT_DEC_F] = ones * jnp.exp(CHUNK * lam_f)
        tab_s[hd, _RT_INTER_B] = jnp.exp((CHUNK - rowi) * lam_b)
        tab_s[hd, _RT_STATE_B] = jnp.exp(rowi * lam_b)
        tab_s[hd, _RT_DEC_B] = ones * jnp.exp(CHUNK * lam_b)

    sf_s[...] = jnp.zeros_like(sf_s)
    sb_s[...] = jnp.zeros_like(sb_s)

    def fwd_chunk(q_ref, k_ref, v_ref, y_ref, r0, idx):
        heads = range(RET_HEADS)
        cols = [slice(hd * RET_DK, (hd + 1) * RET_DK) for hd in heads]
        q = [q_ref[pl.ds(r0, CHUNK), cols[hd]] for hd in heads]
        k = [k_ref[pl.ds(r0, CHUNK), cols[hd]] for hd in heads]
        v = [v_ref[pl.ds(r0, CHUNK), cols[hd]] for hd in heads]
        s = [_dot_nt(q[hd], k[hd]) for hd in heads]
        ds = []
        for hd in heads:
            vf = v[hd].astype(F32)
            vw = jnp.concatenate([(vf * tab_s[hd, _RT_STATE_F]).astype(BF16),
                                  (vf * tab_s[hd, _RT_STATE_B]).astype(BF16)], axis=1)
            ds.append(_dot_tn(k[hd], vw))
        for hd in heads:
            sf = sf_s[hd]
            q_scaled = (q[hd].astype(F32) * tab_s[hd, _RT_INTER_F]).astype(BF16)
            lhs = jnp.concatenate([(s[hd] * tab_s[hd, _RT_INTRA]).astype(BF16), q_scaled], axis=1)
            rhs = jnp.concatenate([v[hd], sf.astype(BF16)], axis=0)
            y_ref[pl.ds(r0, CHUNK), cols[hd]] = _dot(lhs, rhs)
            sf_s[hd] = sf * tab_s[hd, _RT_DEC_F] + ds[hd][:, :RET_DV]
            ds_s[idx, hd] = ds[hd][:, RET_DV:]

    def bwd_chunk(q_ref, g_ref, y_ref, o_ref, r0, idx):
        for hd in range(RET_HEADS):
            cols = slice(hd * RET_DK, (hd + 1) * RET_DK)
            sb = sb_s[hd]
            if o_ref is not None:
                q = q_ref[pl.ds(r0, CHUNK), cols]
                y = y_ref[pl.ds(r0, CHUNK), cols] + tab_s[hd, _RT_INTER_B] * _dot(q, sb.astype(BF16))
                gate = _silu(g_ref[pl.ds(r0, CHUNK), cols].astype(F32))
                o_ref[pl.ds(r0, CHUNK), cols] = (_rms(y) * gate).astype(BF16)
            sb_s[hd] = sb * tab_s[hd, _RT_DEC_B] + ds_s[idx, hd]

    for c in range(n_ctx_chunks):
        fwd_chunk(qc_ref, kc_ref, vc_ref, yc_s, c * CHUNK, c)

    def lat_fwd(c, carry):
        fwd_chunk(ql_ref, kl_ref, vl_ref, yl_s, pl.multiple_of(c * CHUNK, CHUNK), n_ctx_chunks + c)
        return carry

    lax.fori_loop(0, n_lat_chunks, lat_fwd, 0, unroll=SCAN_UNROLL)

    for c in reversed(range(n_ctx_chunks)):
        bwd_chunk(qc_ref, gc_ref, yc_s, oc_ref, c * CHUNK, c)

    def lat_bwd(t, carry):
        c = n_lat_chunks - 1 - t
        bwd_chunk(ql_ref, gl_ref, yl_s, ol_ref, pl.multiple_of(c * CHUNK, CHUNK), n_ctx_chunks + c)
        return carry

    lax.fori_loop(0, n_lat_chunks, lat_bwd, 0, unroll=SCAN_UNROLL)


def _retention(tc, tl, decay_f, decay_b, batch, ctx_out):
    lc = tc["rq"].shape[0] // batch
    ll = tl["rq"].shape[0] // batch
    cspec = pl.BlockSpec((lc, RET_W), lambda b: (b, 0))
    lspec = pl.BlockSpec((ll, RET_W), lambda b: (b, 0))
    pspec = pl.BlockSpec((RET_HEADS, 1, LANE), lambda b: (0, 0, 0))
    out_specs = [lspec]
    out_shape = [jax.ShapeDtypeStruct((batch * ll, RET_W), BF16)]
    if ctx_out:
        out_specs = [cspec] + out_specs
        out_shape = [jax.ShapeDtypeStruct((batch * lc, RET_W), BF16)] + out_shape
    bcast = lambda p: jnp.broadcast_to(p.astype(F32)[:, None, None], (RET_HEADS, 1, LANE))
    outs = pl.pallas_call(
        functools.partial(_ret_kernel, lc // CHUNK, ll // CHUNK, ctx_out),
        grid=(batch,),
        in_specs=[cspec] * 4 + [lspec] * 4 + [pspec, pspec],
        out_specs=out_specs,
        out_shape=out_shape,
        scratch_shapes=[
            pltpu.VMEM((lc, RET_W), F32), pltpu.VMEM((ll, RET_W), F32),
            pltpu.VMEM((RET_HEADS, RET_DK, RET_DV), F32), pltpu.VMEM((RET_HEADS, RET_DK, RET_DV), F32),
            pltpu.VMEM((RET_HEADS, 7, CHUNK, CHUNK), F32),
            pltpu.VMEM(((lc + ll) // CHUNK, RET_HEADS, RET_DK, RET_DV), F32),
        ],
        compiler_params=_cparams(("parallel",)),
        name="retention",
    )(tc["rq"], tc["rk"], tc["rv"], tc["rg"], tl["rq"], tl["rk"], tl["rv"], tl["rg"],
      bcast(decay_f), bcast(decay_b))
    return (outs[0], outs[1]) if ctx_out else (None, outs[0])


def _expand_heads(v, e_ref):
    hi = v.astype(BF16)
    lo = (v - hi.astype(F32)).astype(BF16)
    e = e_ref[...]
    return _dot(hi, e) + _dot(lo, e)


def _cumsum_rows(v):
    rows = lax.broadcasted_iota(jnp.int32, v.shape, 0)
    s = 1
    while s < v.shape[0]:
        v = v + jnp.where(rows >= s, pltpu.roll(v, s, 0), 0.0)
        s *= 2
    return v


def _ssd_kernel(n_ctx_chunks, n_lat_chunks, ctx_out, *refs):
    (xc_ref, dtc_ref, zc_ref, xl_ref, dtl_ref, zl_ref,
     cw_ref, cb_ref, bias_ref, alog_ref, skip_ref, ng_ref, ef_ref, eb_ref, shift_ref) = refs[:15]
    if ctx_out:
        oc_ref, ol_ref, tok_s, y_s, sf_s, sb_s = refs[15:]
    else:
        ol_ref, tok_s, y_s, sf_s, sb_s = refs[15:]
        oc_ref = None
    lc = n_ctx_chunks * CHUNK

    ii = lax.broadcasted_iota(jnp.int32, (CHUNK, CHUNK), 0)
    jj = lax.broadcasted_iota(jnp.int32, (CHUNK, CHUNK), 1)
    lower = jj < ii
    upper = jj > ii
    neg_a = -jnp.exp(alog_ref[...])
    halo = CONV_HALO

    def conv_silu(x_ref, c, n_chunks):
        if isinstance(c, int):
            r0 = c * CHUNK
            prev0 = max(r0 - halo, 0)
            next0 = min(r0 + CHUNK, (n_chunks - 1) * CHUNK)
        else:
            r0 = pl.multiple_of(c * CHUNK, CHUNK)
            prev0 = pl.multiple_of(jnp.maximum(r0 - halo, 0), halo)
            next0 = pl.multiple_of(jnp.minimum(r0 + CHUNK, (n_chunks - 1) * CHUNK), halo)
        first = c == 0
        last = c == n_chunks - 1
        centre = x_ref[pl.ds(r0, CHUNK), :]
        prev = x_ref[pl.ds(prev0, halo), :]
        nxt = x_ref[pl.ds(next0, halo), :]
        prev = jnp.where(first, jnp.zeros_like(prev), prev)
        nxt = jnp.where(last, jnp.zeros_like(nxt), nxt)
        pad = jnp.zeros((CONV_EXT - CHUNK - 2 * halo, CONV_CH), BF16)
        ext = jnp.concatenate([centre, prev, nxt, pad], axis=0)
        acc = cb_ref[...] + cw_ref[SSD_CONV // 2:SSD_CONV // 2 + 1, :] * centre.astype(F32)
        taps = [k for k in range(SSD_CONV) if k != SSD_CONV // 2]
        for n, k in enumerate(taps):
            acc = acc + cw_ref[k:k + 1, :] * _dot(shift_ref[n], ext)
        return _silu(acc)

    def decays(dt_ref, r0):
        dt = jax.nn.softplus(dt_ref[pl.ds(r0, CHUNK), :] + bias_ref[...])
        la = dt * neg_a
        a_inc = _cumsum_rows(la)
        total = a_inc[CHUNK - 1:CHUNK, :]
        return dt, la, a_inc, total

    def decay_row(total, e_ref):
        return _expand_heads(jnp.broadcast_to(jnp.exp(total), (8, LANE)), e_ref)[0:1, :]

    def fwd_chunk(x_ref, dt_ref, c, n_chunks, base):
        if isinstance(c, int):
            r0 = c * CHUNK
            t0 = base + r0
        else:
            r0 = pl.multiple_of(c * CHUNK, CHUNK)
            t0 = pl.multiple_of(base + r0, CHUNK)
        tok = conv_silu(x_ref, c, n_chunks)
        tok_b = tok.astype(BF16)
        tok_s[pl.ds(t0, CHUNK), :] = tok_b
        xs = tok_b[:, :SSD_W]
        dt, la, a_inc, total = decays(dt_ref, r0)
        a_exc = a_inc - la
        log2e = math.log2(math.e)
        ldt = jnp.log2(dt)
        col_term = a_inc * log2e
        col_term_b = a_exc * log2e
        row_t = (jnp.where(lax.broadcasted_iota(jnp.int32, (CHUNK, LANE), 1) < SSD_HEADS,
                           ldt - col_term, ldt + col_term_b)).T
        dt_t = dt.T
        groups = range(SSD_GROUPS)
        bs = [tok_b[:, SSD_W + g * SSD_STATE:SSD_W + (g + 1) * SSD_STATE] for g in groups]
        cs = [tok_b[:, SSD_W + SSD_BC + g * SSD_STATE:SSD_W + SSD_BC + (g + 1) * SSD_STATE] for g in groups]
        gsl = [slice(g * GROUP_W, (g + 1) * GROUP_W) for g in groups]
        gmat = [_dot_nt(cs[g], bs[g]) for g in groups]
        inter = [_dot(cs[g], sf_s[:, gsl[g]].astype(BF16)) for g in groups]
        inter_scale = _expand_heads(jnp.exp(a_inc), ef_ref)
        state_scale = _expand_heads(jnp.exp(total - a_inc) * dt, ef_ref)
        dec = decay_row(total, ef_ref)
        xw = (xs.astype(F32) * state_scale).astype(BF16)
        dstate = [_dot_tn(bs[g], xw[:, gsl[g]]) for g in groups]
        ms = []
        for hd in range(SSD_HEADS):
            hb = SSD_HEADS + hd
            diag = jnp.log2(dt_t[hd:hd + 1, :] + dt_t[hb:hb + 1, :])
            expo = jnp.where(lower, col_term[:, hd:hd + 1] + row_t[hd:hd + 1, :],
                             jnp.where(upper, row_t[hb:hb + 1, :] - col_term_b[:, hb:hb + 1], diag))
            ms.append((gmat[hd // HEADS_PER_GROUP] * jnp.exp2(expo)).astype(BF16))
        ys = []
        for pair in range(SSD_HEADS // 2):
            col = 2 * pair * SSD_HEAD_DIM
            xpair = xs[:, col:col + 2 * SSD_HEAD_DIM]
            lane = lax.broadcasted_iota(jnp.int32, xpair.shape, 1)
            zero = jnp.zeros_like(xpair)
            rhs = jnp.concatenate([jnp.where(lane < SSD_HEAD_DIM, xpair, zero),
                                   jnp.where(lane < SSD_HEAD_DIM, zero, xpair)], axis=0)
            ys.append(_dot(jnp.concatenate(ms[2 * pair:2 * pair + 2], axis=1), rhs))
        y_s[pl.ds(t0, CHUNK), :] = jnp.concatenate(ys, axis=1) + inter_scale * jnp.concatenate(inter, axis=1)
        for g in groups:
            sf_s[:, gsl[g]] = sf_s[:, gsl[g]] * dec[:, gsl[g]] + dstate[g]

    def bwd_chunk(dt_ref, z_ref, o_ref, c, base):
        if isinstance(c, int):
            r0 = c * CHUNK
            t0 = base + r0
        else:
            r0 = pl.multiple_of(c * CHUNK, CHUNK)
            t0 = pl.multiple_of(base + r0, CHUNK)
        tok_b = tok_s[pl.ds(t0, CHUNK), :]
        xs = tok_b[:, :SSD_W].astype(F32)
        dt, la, a_inc, total = decays(dt_ref, r0)
        a_exc = a_inc - la
        if o_ref is not None:
            inter = []
            for g in range(SSD_GROUPS):
                cs = tok_b[:, SSD_W + SSD_BC + g * SSD_STATE:SSD_W + SSD_BC + (g + 1) * SSD_STATE]
                inter.append(_dot(cs, sb_s[:, g * GROUP_W:(g + 1) * GROUP_W].astype(BF16)))
            y = y_s[pl.ds(t0, CHUNK), :] + _expand_heads(jnp.exp(total - a_exc), eb_ref) * jnp.concatenate(inter, axis=1)
            y = (y + skip_ref[...] * xs) * _silu(z_ref[pl.ds(r0, CHUNK), :].astype(F32))
            o_ref[pl.ds(r0, CHUNK), :] = (_rms(y) * ng_ref[...]).astype(BF16)
        xw = (xs * _expand_heads(jnp.exp(a_exc) * dt, eb_ref)).astype(BF16)
        dec = decay_row(total, eb_ref)
        for g in range(SSD_GROUPS):
            bs = tok_b[:, SSD_W + g * SSD_STATE:SSD_W + (g + 1) * SSD_STATE]
            sl = slice(g * GROUP_W, (g + 1) * GROUP_W)
            sb_s[:, sl] = sb_s[:, sl] * dec[:, sl] + _dot_tn(bs, xw[:, sl])

    sf_s[...] = jnp.zeros_like(sf_s)
    sb_s[...] = jnp.zeros_like(sb_s)

    for c in range(n_ctx_chunks):
        fwd_chunk(xc_ref, dtc_ref, c, n_ctx_chunks, 0)

    def lat_fwd(c, carry):
        fwd_chunk(xl_ref, dtl_ref, c, n_lat_chunks, lc)
        return carry

    lax.fori_loop(0, n_lat_chunks, lat_fwd, 0, unroll=SCAN_UNROLL)

    for c in reversed(range(n_ctx_chunks)):
        bwd_chunk(dtc_ref, zc_ref, oc_ref, c, 0)

    def lat_bwd(t, carry):
        bwd_chunk(dtl_ref, zl_ref, ol_ref, n_lat_chunks - 1 - t, lc)
        return carry

    lax.fori_loop(0, n_lat_chunks, lat_bwd, 0, unroll=SCAN_UNROLL)


def _conv_shift_matrices():
    taps = [k for k in range(SSD_CONV) if k != SSD_CONV // 2]
    mats = np.zeros((len(taps), CHUNK, CONV_EXT), np.float32)
    for n, k in enumerate(taps):
        for i in range(CHUNK):
            src = i + k - SSD_CONV // 2
            if src < 0:
                col = CHUNK + CONV_HALO + src
            elif src >= CHUNK:
                col = CHUNK + CONV_HALO + (src - CHUNK)
            else:
                col = src
            mats[n, i, col] = 1.0
    return mats


def _head_lane_vec(f, b):
    v = jnp.zeros((1, LANE), F32)
    v = v.at[0, :SSD_HEADS].set(f.astype(F32))
    return v.at[0, SSD_HEADS:2 * SSD_HEADS].set(b.astype(F32))


def _ssd(tc, tl, p, batch, ctx_out):
    lc = tc["xbc"].shape[0] // batch
    ll = tl["xbc"].shape[0] // batch
    rows = lambda n, w: pl.BlockSpec((n, w), lambda b: (b, 0))
    vec = lambda w: pl.BlockSpec((1, w), lambda b: (0, 0))
    whole = lambda a: pl.BlockSpec(a.shape, lambda b: (0,) * a.ndim)
    head_of_col = np.arange(SSD_W) // SSD_HEAD_DIM
    ef = jnp.asarray(np.arange(LANE)[:, None] == head_of_col[None, :], BF16)
    eb = jnp.asarray(np.arange(LANE)[:, None] == head_of_col[None, :] + SSD_HEADS, BF16)
    shift = jnp.asarray(_conv_shift_matrices(), BF16)
    conv_w = p["conv_w"].astype(F32)
    conv_b = p["conv_b"].astype(F32).reshape(1, CONV_CH)
    bias = _head_lane_vec(p["dt_bias_f"], p["dt_bias_b"])
    alog = _head_lane_vec(p["a_log_f"], p["a_log_b"])
    skip = jnp.repeat(p["d_skip"].astype(F32), SSD_HEAD_DIM).reshape(1, SSD_W)
    norm_g = p["ssd_norm_g"].astype(F32).reshape(1, SSD_W)
    out_specs = [rows(ll, SSD_W)]
    out_shape = [jax.ShapeDtypeStruct((batch * ll, SSD_W), BF16)]
    if ctx_out:
        out_specs = [rows(lc, SSD_W)] + out_specs
        out_shape = [jax.ShapeDtypeStruct((batch * lc, SSD_W), BF16)] + out_shape
    outs = pl.pallas_call(
        functools.partial(_ssd_kernel, lc // CHUNK, ll // CHUNK, ctx_out),
        grid=(batch,),
        in_specs=[rows(lc, CONV_CH), rows(lc, LANE), rows(lc, SSD_W),
                  rows(ll, CONV_CH), rows(ll, LANE), rows(ll, SSD_W),
                  whole(conv_w), vec(CONV_CH), vec(LANE), vec(LANE), vec(SSD_W), vec(SSD_W),
                  whole(ef), whole(eb), whole(shift)],
        out_specs=out_specs,
        out_shape=out_shape,
        scratch_shapes=[
            pltpu.VMEM((lc + ll, CONV_CH), BF16), pltpu.VMEM((lc + ll, SSD_W), F32),
            pltpu.VMEM((SSD_STATE, SSD_W), F32), pltpu.VMEM((SSD_STATE, SSD_W), F32),
        ],
        compiler_params=_cparams(("parallel",)),
        name="ssd_scan",
    )(tc["xbc"], tc["dt"], tc["z"], tl["xbc"], tl["dt"], tl["z"],
      conv_w, conv_b, bias, alog, skip, norm_g, ef, eb, shift)
    return (outs[0], outs[1]) if ctx_out else (None, outs[0])


def _out_proj_kernel(n_sub, att_ref, ret_ref, ssd_ref, w_ref, x_ref, g1_ref, pmg_ref, x1_ref):
    ts = x_ref.shape[0] // n_sub
    for c in range(n_sub):
        rows = slice(c * ts, (c + 1) * ts)
        m = (_dot(att_ref[rows, :], w_ref[0:ATT_W, :])
             + _dot(ret_ref[rows, :], w_ref[ATT_W:ATT_W + RET_W, :])
             + _dot(ssd_ref[rows, :], w_ref[ATT_W + RET_W:, :]))
        x1_ref[rows, :] = x_ref[rows, :] + g1_ref[...] * (_rms(m) * pmg_ref[...])


def _out_proj(att, ret, ssd, w_out_b, x2d, mod4, group_of_tile, post_mix_g, tm):
    rows, d = x2d.shape
    row_spec = lambda w: pl.BlockSpec((tm, w), lambda i: (i, 0))
    mod_spec = lambda k: pl.BlockSpec((None, None, 1, d), lambda i: (group_of_tile(i), k, 0, 0))
    vec_spec = pl.BlockSpec((1, d), lambda i: (0, 0))
    n_sub = 2 if tm % 32 == 0 else 1
    return pl.pallas_call(
        functools.partial(_out_proj_kernel, n_sub),
        grid=(rows // tm,),
        in_specs=[row_spec(ATT_W), row_spec(RET_W), row_spec(SSD_W),
                  pl.BlockSpec(w_out_b.shape, lambda i: (0, 0)),
                  row_spec(d), mod_spec(2), vec_spec],
        out_specs=row_spec(d),
        out_shape=jax.ShapeDtypeStruct((rows, d), F32),
        compiler_params=_cparams(("parallel",)),
        name="out_proj",
    )(att, ret, ssd, w_out_b, x2d, mod4, post_mix_g)


def _ffn_kernel(x1_ref, sc_ref, sh_ref, pfg_ref, wg_ref, wu_ref, wd_ref, g2_ref, pg_ref, o_ref, hf_s):
    j = pl.program_id(1)
    last_j = pl.num_programs(1) - 1
    tm, n_out = o_ref.shape
    ts = _pick_tile(tm, FFN_SUB_ROWS)
    tr = _pick_tile(ts, FFN_EPILOGUE_ROWS)
    tn = _pick_tile(n_out, FFN_DOWN_COLS)

    def hidden(rows):
        hf = hf_s[rows, :]
        return (_silu(_dot(hf, wg_ref[...])) * _dot(hf, wu_ref[...])).astype(BF16)

    @pl.when(j == 0)
    def _():
        gain = pfg_ref[...] * (1.0 + sc_ref[...])
        for c in range(tm // ts):
            for r in range(ts // tr):
                rows = slice(c * ts + r * tr, c * ts + (r + 1) * tr)
                hf_s[rows, :] = (_rms(x1_ref[rows, :]) * gain + sh_ref[...]).astype(BF16)
            rows = slice(c * ts, (c + 1) * ts)
            o_ref[rows, :] = _dot(hidden(rows), wd_ref[...])

    @pl.when(jnp.logical_and(j > 0, j < last_j))
    def _():
        a = hidden(slice(0, tm))
        for n in range(n_out // tn):
            o_ref[:, n * tn:(n + 1) * tn] += _dot(a, wd_ref[:, n * tn:(n + 1) * tn])

    @pl.when(j == last_j)
    def _():
        gain = g2_ref[...] * pg_ref[...]
        for c in range(tm // ts):
            rows = slice(c * ts, (c + 1) * ts)
            f = o_ref[rows, :] + _dot(hidden(rows), wd_ref[...])
            o_ref[rows, :] = x1_ref[rows, :] + _rms(f) * gain


def _ffn(x1, wg, wu, wd, mod4, group_of_tile, pre_ffn_g, post_ffn_g, tm, tf):
    rows, d = x1.shape
    dff = wg.shape[1]
    assert dff // tf >= 2
    mod_spec = lambda k: pl.BlockSpec((None, None, 1, d), lambda i, j: (group_of_tile(i), k, 0, 0))
    vec_spec = pl.BlockSpec((1, d), lambda i, j: (0, 0))
    return pl.pallas_call(
        _ffn_kernel,
        grid=(rows // tm, dff // tf),
        in_specs=[
            pl.BlockSpec((tm, d), lambda i, j: (i, 0)),
            mod_spec(4), mod_spec(3), vec_spec,
            pl.BlockSpec((d, tf), lambda i, j: (0, j)),
            pl.BlockSpec((d, tf), lambda i, j: (0, j)),
            pl.BlockSpec((tf, d), lambda i, j: (j, 0)),
            mod_spec(5), vec_spec,
        ],
        out_specs=pl.BlockSpec((tm, d), lambda i, j: (i, 0)),
        out_shape=jax.ShapeDtypeStruct((rows, d), F32),
        scratch_shapes=[pltpu.VMEM((tm, d), BF16)],
        compiler_params=_cparams(("parallel", "arbitrary")),
        name="swiglu_ffn",
    )(x1, mod4, mod4, pre_ffn_g, wg, wu, wd, mod4, post_ffn_g)


def _cast_kernel(w_ref, o_ref):
    o_ref[...] = w_ref[...].astype(BF16)


def _cast_layer(w_stack, layer, n_cols):
    _, rows, _ = w_stack.shape
    tr = _pick_tile(rows, 256)
    return pl.pallas_call(
        _cast_kernel,
        grid=(rows // tr,),
        in_specs=[pl.BlockSpec((None, tr, n_cols), lambda i: (layer, i, 0))],
        out_specs=pl.BlockSpec((tr, n_cols), lambda i: (i, 0)),
        out_shape=jax.ShapeDtypeStruct((rows, n_cols), BF16),
        compiler_params=_cparams(("parallel",)),
        name="cast_weight",
    )(w_stack)


def _rope_tables(seq_len):
    rows = seq_len // GRID_W
    row = jnp.repeat(jnp.arange(rows, dtype=F32), GRID_W)
    col = jnp.tile(jnp.arange(GRID_W, dtype=F32), rows)
    n_freq = HEAD_DIM // 4
    inv = ROPE_THETA ** (-jnp.arange(n_freq, dtype=F32) / n_freq)
    ang = jnp.concatenate([row[:, None] * inv, col[:, None] * inv], axis=-1)
    cos, sin = jnp.cos(ang), jnp.sin(ang)
    cos_t = jnp.repeat(cos, 2, axis=-1)
    sin_t = jnp.stack([-sin, sin], axis=-1).reshape(seq_len, HEAD_DIM)
    return cos_t, sin_t


def _pick_tile(n, target):
    t = min(n, target)
    while n % t:
        t //= 2
    return t


def _cast_tail_kernel(n_valid, w_ref, o_ref):
    lane = lax.broadcasted_iota(jnp.int32, w_ref.shape, 1)
    o_ref[...] = jnp.where(lane < n_valid, w_ref[...], 0.0).astype(BF16)


def _cast_layer_tail(w_stack, layer, col0):
    _, rows, cols = w_stack.shape
    assert col0 % LANE == 0 and 0 < cols - col0 <= LANE
    tr = _pick_tile(rows, 1024)
    return pl.pallas_call(
        functools.partial(_cast_tail_kernel, cols - col0),
        grid=(rows // tr,),
        in_specs=[pl.BlockSpec((None, tr, LANE), lambda i: (layer, i, col0 // LANE))],
        out_specs=pl.BlockSpec((tr, LANE), lambda i: (i, 0)),
        out_shape=jax.ShapeDtypeStruct((rows, LANE), BF16),
        compiler_params=_cparams(("parallel",)),
        name="cast_weight_tail",
    )(w_stack)


def _layer_weights(l, w_in, w_out, w_gate, w_up, w_down):
    w_in_b = _cast_layer(w_in, l, OFF_DT)
    w_dt_b = _cast_layer_tail(w_in, l, OFF_DT)
    return (w_in_b, w_dt_b, _cast_layer(w_out, l, w_out.shape[2]), _cast_layer(w_gate, l, w_gate.shape[2]),
            _cast_layer(w_up, l, w_up.shape[2]), _cast_layer(w_down, l, w_down.shape[2]))


def kernel(x, c, ctx, c_ctx, w_mod, b_mod, pre_mix_g, post_mix_g, pre_ffn_g, post_ffn_g, w_in, q_norm_g, k_norm_g, ret_decay_f, ret_decay_b, conv_w, conv_b, dt_bias_f, dt_bias_b, a_log_f, a_log_b, d_skip, ssd_norm_g, w_out, w_gate, w_up, w_down):
    batch, seq, d = x.shape
    lc = ctx.shape[1]
    depth = w_mod.shape[0]
    assert batch < N_MOD_ROWS and seq % CHUNK == 0 and lc % CHUNK == 0 and seq % GRID_W == 0

    c16 = jnp.zeros((N_MOD_ROWS, d), F32).at[:batch].set(c.astype(F32)).at[batch].set(c_ctx.astype(F32))
    mod_all = _modulation(c16, w_mod.astype(F32), b_mod.astype(F32))
    rope_tabs = _rope_tables(seq)

    tm_lat = _pick_tile(seq, 512)
    tm_ctx = _pick_tile(batch * lc, 512)
    tm_ffn_lat = _pick_tile(seq, 1024)
    tm_ffn_ctx = _pick_tile(batch * lc, 1024)
    lat_group = lambda tm: (lambda i: i // (seq // tm))
    ctx_group = lambda tm: (lambda i: batch)
    tq_lat = _pick_tile(seq, 1024)
    tq_ctx = _pick_tile(lc, 256)
    sub_lat = max(tq_lat // 128, 1)
    sub_ctx = 2 if tq_ctx % 32 == 0 else 1
    tf = _pick_tile(w_gate.shape[2], 512)

    xl = x.reshape(batch * seq, d).astype(F32)
    xc = ctx.reshape(batch * lc, d).astype(F32)
    vec = lambda v: v.astype(F32).reshape(1, d)

    for l in range(depth):
        last = l == depth - 1
        w_in_b, w_dt_b, w_out_b, wg, wu, wd = _layer_weights(l, w_in, w_out, w_gate, w_up, w_down)
        qg = q_norm_g[l].astype(F32).reshape(1, HEAD_DIM)
        kg = k_norm_g[l].astype(F32).reshape(1, HEAD_DIM)
        mod4 = mod_all[l].reshape(N_MOD_ROWS, 6, 1, d)
        p = dict(conv_w=conv_w[l], conv_b=conv_b[l], dt_bias_f=dt_bias_f[l], dt_bias_b=dt_bias_b[l],
                 a_log_f=a_log_f[l], a_log_b=a_log_b[l], d_skip=d_skip[l], ssd_norm_g=ssd_norm_g[l])

        tl = _in_proj(xl, mod4, lat_group(tm_lat), vec(pre_mix_g[l]), w_in_b, w_dt_b, qg, kg, rope_tabs, tm_lat)
        tc = _in_proj(xc, mod4, ctx_group(tm_ctx), vec(pre_mix_g[l]), w_in_b, w_dt_b, qg, kg, None, tm_ctx)

        att_l = _attention(tl["aq"], [(tc["ak"], tc["av"]), (tl["ak"], tl["av"])], batch, tq_lat, sub_lat)
        ret_c, ret_l = _retention(tc, tl, ret_decay_f[l], ret_decay_b[l], batch, not last)
        ssd_c, ssd_l = _ssd(tc, tl, p, batch, not last)

        x1 = _out_proj(att_l, ret_l, ssd_l, w_out_b, xl, mod4, lat_group(tm_lat), vec(post_mix_g[l]), tm_lat)
        xl = _ffn(x1, wg, wu, wd, mod4, lat_group(tm_ffn_lat), vec(pre_ffn_g[l]), vec(post_ffn_g[l]),
                  tm_ffn_lat, tf)

        if not last:
            att_c = _attention(tc["aq"], [(tc["ak"], tc["av"])], batch, tq_ctx, sub_ctx)
            x1c = _out_proj(att_c, ret_c, ssd_c, w_out_b, xc, mod4, ctx_group(tm_ctx), vec(post_mix_g[l]), tm_ctx)
            xc = _ffn(x1c, wg, wu, wd, mod4, ctx_group(tm_ffn_ctx), vec(pre_ffn_g[l]), vec(post_ffn_g[l]),
                      tm_ffn_ctx, tf)

    return xl.reshape(batch, seq, d).astype(x.dtype)
```

```python
import functools
import math

import numpy as np
import jax
import jax.numpy as jnp
from jax import lax
from jax.experimental import pallas as pl
from jax.experimental.pallas import tpu as pltpu

F32 = jnp.float32
BF16 = jnp.bfloat16

GRID_W = 64
HEAD_DIM = 128
ATT_HEADS = 6
ATT_KV_HEADS = 2
ATT_REP = ATT_HEADS // ATT_KV_HEADS
RET_HEADS = 4
RET_DK = 128
RET_DV = 128
SSD_HEADS = 12
SSD_HEAD_DIM = 64
SSD_GROUPS = 2
SSD_STATE = 128
SSD_CONV = 5
ATT_W = ATT_HEADS * HEAD_DIM
KV_W = ATT_KV_HEADS * HEAD_DIM
RET_QK_W = RET_HEADS * RET_DK
RET_W = RET_HEADS * RET_DV
SSD_W = SSD_HEADS * SSD_HEAD_DIM
SSD_BC = SSD_GROUPS * SSD_STATE
CONV_CH = SSD_W + 2 * SSD_BC
GROUP_W = SSD_W // SSD_GROUPS
HEADS_PER_GROUP = SSD_HEADS // SSD_GROUPS
CHUNK = 128
ROPE_THETA = 10000.0
EPS = 1e-6
LANE = 128
FFN_DOWN_COLS = 512
FFN_EPILOGUE_ROWS = 128
FFN_SUB_ROWS = 256
SCAN_UNROLL = 4
CONV_HALO = 16
CONV_EXT = 256

OFF_AQ = 0
OFF_AK = OFF_AQ + ATT_W
OFF_AV = OFF_AK + KV_W
OFF_RQ = OFF_AV + KV_W
OFF_RK = OFF_RQ + RET_QK_W
OFF_RV = OFF_RK + RET_QK_W
OFF_RG = OFF_RV + RET_W
OFF_Z = OFF_RG + RET_W
OFF_XBC = OFF_Z + SSD_W
OFF_DT = OFF_XBC + CONV_CH
D_IN = OFF_DT + 2 * SSD_HEADS
D_IN_PAD = OFF_DT + LANE

VMEM_LIMIT = 60 * 1024 * 1024
N_MOD_ROWS = 16


def _cparams(sem):
    return pltpu.CompilerParams(dimension_semantics=sem, vmem_limit_bytes=VMEM_LIMIT)


def _silu(v):
    return v * jax.nn.sigmoid(v)


def _rms(v):
    return v * lax.rsqrt(jnp.mean(v * v, axis=-1, keepdims=True) + EPS)


def _dot(a, b):
    return jnp.dot(a, b, preferred_element_type=F32)


def _dot_nt(a, b):
    return lax.dot_general(a, b, (((1,), (1,)), ((), ())), preferred_element_type=F32)


def _dot_tn(a, b):
    return lax.dot_general(a, b, (((0,), (0,)), ((), ())), preferred_element_type=F32)


def _mod_kernel(c_ref, w_ref, b_ref, o_ref):
    a = _silu(c_ref[...]).astype(BF16)
    o_ref[...] = _dot(a, w_ref[...].astype(BF16)) + b_ref[...]


def _modulation(c16, w_mod, b_mod):
    depth, d, n = w_mod.shape
    tn = _pick_tile(n, 1024)
    return pl.pallas_call(
        _mod_kernel,
        grid=(depth, n // tn),
        in_specs=[
            pl.BlockSpec((N_MOD_ROWS, d), lambda l, j: (0, 0)),
            pl.BlockSpec((None, d, tn), lambda l, j: (l, 0, j)),
            pl.BlockSpec((None, 1, tn), lambda l, j: (l, 0, j)),
        ],
        out_specs=pl.BlockSpec((None, N_MOD_ROWS, tn), lambda l, j: (l, 0, j)),
        out_shape=jax.ShapeDtypeStruct((depth, N_MOD_ROWS, n), F32),
        compiler_params=_cparams(("parallel", "parallel")),
        name="adaln_modulation",
    )(c16, w_mod, b_mod.reshape(depth, 1, n))


def _in_proj_kernel(use_rope, *refs):
    if use_rope:
        (x_ref, sc_ref, sh_ref, g_ref, w_ref, wdt_ref, qg_ref, kg_ref, cos_ref, sin_ref,
         aq_ref, ak_ref, av_ref, rq_ref, rk_ref, rv_ref, rg_ref, z_ref, xbc_ref, dt_ref) = refs
        cos = cos_ref[...]
        sin = sin_ref[...]
        even_lane = (lax.broadcasted_iota(jnp.int32, cos.shape, 1) & 1) == 0
    else:
        (x_ref, sc_ref, sh_ref, g_ref, w_ref, wdt_ref, qg_ref, kg_ref,
         aq_ref, ak_ref, av_ref, rq_ref, rk_ref, rv_ref, rg_ref, z_ref, xbc_ref, dt_ref) = refs

    x = x_ref[...]
    h = (_rms(x) * g_ref[...] * (1.0 + sc_ref[...]) + sh_ref[...]).astype(BF16)

    def proj(lo, width):
        return _dot_nt(h, w_ref[lo:lo + width, :])

    def rope(t):
        if not use_rope:
            return t
        partner = jnp.where(even_lane, pltpu.roll(t, HEAD_DIM - 1, 1), pltpu.roll(t, 1, 1))
        return t * cos + partner * sin

    att_scale = HEAD_DIM ** -0.5 * math.log2(math.e)
    ret_scale = RET_DK ** -0.5

    acc = proj(OFF_AQ, ATT_W)
    for hd in range(ATT_HEADS):
        t = acc[:, hd * HEAD_DIM:(hd + 1) * HEAD_DIM]
        t = rope(_rms(t) * qg_ref[...]) * att_scale
        aq_ref[:, hd * HEAD_DIM:(hd + 1) * HEAD_DIM] = t.astype(BF16)

    acc = proj(OFF_AK, KV_W)
    for hd in range(ATT_KV_HEADS):
        t = acc[:, hd * HEAD_DIM:(hd + 1) * HEAD_DIM]
        t = rope(_rms(t) * kg_ref[...])
        ak_ref[:, hd * HEAD_DIM:(hd + 1) * HEAD_DIM] = t.astype(BF16)

    av_ref[...] = proj(OFF_AV, KV_W).astype(BF16)

    acc = proj(OFF_RQ, RET_QK_W)
    for hd in range(RET_HEADS):
        t = rope(acc[:, hd * RET_DK:(hd + 1) * RET_DK])
        rq_ref[:, hd * RET_DK:(hd + 1) * RET_DK] = t.astype(BF16)

    acc = proj(OFF_RK, RET_QK_W)
    for hd in range(RET_HEADS):
        t = rope(acc[:, hd * RET_DK:(hd + 1) * RET_DK] * ret_scale)
        rk_ref[:, hd * RET_DK:(hd + 1) * RET_DK] = t.astype(BF16)

    rv_ref[...] = proj(OFF_RV, RET_W).astype(BF16)
    rg_ref[...] = proj(OFF_RG, RET_W).astype(BF16)
    z_ref[...] = proj(OFF_Z, SSD_W).astype(BF16)
    xbc_ref[...] = proj(OFF_XBC, CONV_CH).astype(BF16)
    dt_ref[...] = _dot_nt(h, wdt_ref[...])


def _in_proj(x2d, mod4, group_of_tile, pre_g, w_in_b, w_dt_b, qg, kg, rope_tabs, tm):
    rows, d = x2d.shape
    nt = rows // tm
    use_rope = rope_tabs is not None
    row_spec = lambda w: pl.BlockSpec((tm, w), lambda i: (i, 0))
    mod_spec = lambda k: pl.BlockSpec((None, None, 1, d), lambda i: (group_of_tile(i), k, 0, 0))
    vec_spec = lambda w: pl.BlockSpec((1, w), lambda i: (0, 0))
    in_specs = [
        row_spec(d), mod_spec(1), mod_spec(0), vec_spec(d),
        pl.BlockSpec((OFF_DT, d), lambda i: (0, 0), pipeline_mode=pl.Buffered(1)),
        pl.BlockSpec((LANE, d), lambda i: (0, 0)),
        vec_spec(HEAD_DIM), vec_spec(HEAD_DIM),
    ]
    args = [x2d, mod4, mod4, pre_g, w_in_b, w_dt_b, qg, kg]
    if use_rope:
        cos_t, sin_t = rope_tabs
        tiles_per_seq = cos_t.shape[0] // tm
        tab_spec = pl.BlockSpec((tm, HEAD_DIM), lambda i: (i % tiles_per_seq, 0))
        in_specs += [tab_spec, tab_spec]
        args += [cos_t, sin_t]
    widths = (ATT_W, KV_W, KV_W, RET_QK_W, RET_QK_W, RET_W, RET_W, SSD_W, CONV_CH)
    out_specs = [row_spec(w) for w in widths] + [row_spec(LANE)]
    out_shape = [jax.ShapeDtypeStruct((rows, w), BF16) for w in widths]
    out_shape.append(jax.ShapeDtypeStruct((rows, LANE), F32))
    outs = pl.pallas_call(
        functools.partial(_in_proj_kernel, use_rope),
        grid=(nt,),
        in_specs=in_specs,
        out_specs=out_specs,
        out_shape=out_shape,
        compiler_params=_cparams(("parallel",)),
        name="in_proj_rope" if use_rope else "in_proj",
    )(*args)
    names = ("aq", "ak", "av", "rq", "rk", "rv", "rg", "z", "xbc", "dt")
    return dict(zip(names, outs))


def _attn_kernel(n_kv, tq, n_sub, *refs):
    q_ref = refs[0]
    kv_refs = refs[1:1 + 2 * n_kv]
    o_ref = refs[1 + 2 * n_kv]
    ts = tq // n_sub
    all_scores = []
    for c in range(n_sub):
        q = q_ref[c * ts:(c + 1) * ts, :]
        qs = jnp.concatenate([q[:, r * HEAD_DIM:(r + 1) * HEAD_DIM] for r in range(ATT_REP)], axis=0)
        all_scores.append([_dot_nt(qs, kv_refs[2 * s][...]) for s in range(n_kv)])
    v_ones = []
    for s in range(n_kv):
        v = kv_refs[2 * s + 1][...]
        v_ones.append(jnp.concatenate([v, jnp.ones_like(v)], axis=1))
    for c in range(n_sub):
        scores = all_scores[c]
        m = functools.reduce(jnp.maximum, [jnp.max(s, axis=-1, keepdims=True) for s in scores])
        acc = functools.reduce(jnp.add, [_dot(jnp.exp2(s - m).astype(BF16), v_ones[i]) for i, s in enumerate(scores)])
        out = acc[:, :HEAD_DIM] / acc[:, HEAD_DIM:]
        for r in range(ATT_REP):
            o_ref[c * ts:(c + 1) * ts, r * HEAD_DIM:(r + 1) * HEAD_DIM] = out[r * ts:(r + 1) * ts].astype(BF16)


def _attention(q2d, kv_list, batch, tq, n_sub):
    rows = q2d.shape[0]
    lq = rows // batch
    nq = lq // tq
    gw = ATT_REP * HEAD_DIM
    in_specs = [pl.BlockSpec((tq, gw), lambda b, g, i: (b * nq + i, g))]
    args = [q2d]
    for k2d, v2d in kv_list:
        lk = k2d.shape[0] // batch
        spec = pl.BlockSpec((lk, HEAD_DIM), lambda b, g, i: (b, g))
        in_specs += [spec, spec]
        args += [k2d, v2d]
    return pl.pallas_call(
        functools.partial(_attn_kernel, len(kv_list), tq, n_sub),
        grid=(batch, ATT_KV_HEADS, nq),
        in_specs=in_specs,
        out_specs=pl.BlockSpec((tq, gw), lambda b, g, i: (b * nq + i, g)),
        out_shape=jax.ShapeDtypeStruct((rows, ATT_W), BF16),
        compiler_params=_cparams(("parallel", "parallel", "parallel")),
        name="attention",
    )(*args)


_RT_INTRA, _RT_INTER_F, _RT_STATE_F, _RT_DEC_F, _RT_INTER_B, _RT_STATE_B, _RT_DEC_B = range(7)


def _ret_kernel(n_ctx_chunks, n_lat_chunks, ctx_out, *refs):
    (qc_ref, kc_ref, vc_ref, gc_ref, ql_ref, kl_ref, vl_ref, gl_ref, df_ref, db_ref) = refs[:10]
    if ctx_out:
        oc_ref, ol_ref, yc_s, yl_s, sf_s, sb_s, tab_s, ds_s = refs[10:]
    else:
        ol_ref, yc_s, yl_s, sf_s, sb_s, tab_s, ds_s = refs[10:]
        oc_ref = None

    ii = lax.broadcasted_iota(jnp.int32, (CHUNK, CHUNK), 0)
    jj = lax.broadcasted_iota(jnp.int32, (CHUNK, CHUNK), 1)
    dist = (ii - jj).astype(F32)
    rowi = ii.astype(F32)
    ones = jnp.ones((CHUNK, CHUNK), F32)
    for hd in range(RET_HEADS):
        lam_f = jnp.log1p(-jnp.exp2(df_ref[hd]))
        lam_b = jnp.log1p(-jnp.exp2(db_ref[hd]))
        tab_s[hd, _RT_INTRA] = (jnp.where(jj <= ii, jnp.exp(dist * lam_f), 0.0)
                                + jnp.where(jj >= ii, jnp.exp(-dist * lam_b), 0.0))
        tab_s[hd, _RT_INTER_F] = jnp.exp((rowi + 1.0) * lam_f)
        tab_s[hd, _RT_STATE_F] = jnp.exp((CHUNK - 1.0 - rowi) * lam_f)
        tab_s[hd, _RT_DEC_F] = ones * jnp.exp(CHUNK * lam_f)
        tab_s[hd, _RT_INTER_B] = jnp.exp((CHUNK - rowi) * lam_b)
        tab_s[hd, _RT_STATE_B] = jnp.exp(rowi * lam_b)
        tab_s[hd, _RT_DEC_B] = ones * jnp.exp(CHUNK * lam_b)

    sf_s[...] = jnp.zeros_like(sf_s)
    sb_s[...] = jnp.zeros_like(sb_s)

    def fwd_chunk(q_ref, k_ref, v_ref, y_ref, r0, idx):
        heads = range(RET_HEADS)
        cols = [slice(hd * RET_DK, (hd + 1) * RET_DK) for hd in heads]
        q = [q_ref[pl.ds(r0, CHUNK), cols[hd]] for hd in heads]
        k = [k_ref[pl.ds(r0, CHUNK), cols[hd]] for hd in heads]
        v = [v_ref[pl.ds(r0, CHUNK), cols[hd]] for hd in heads]
        s = [_dot_nt(q[hd], k[hd]) for hd in heads]
        ds = []
        for hd in heads:
            vf = v[hd].astype(F32)
            vw = jnp.concatenate([(vf * tab_s[hd, _RT_STATE_F]).astype(BF16),
                                  (vf * tab_s[hd, _RT_STATE_B]).astype(BF16)], axis=1)
            ds.append(_dot_tn(k[hd], vw))
        for hd in heads:
            sf = sf_s[hd]
            q_scaled = (q[hd].astype(F32) * tab_s[hd, _RT_INTER_F]).astype(BF16)
            lhs = jnp.concatenate([(s[hd] * tab_s[hd, _RT_INTRA]).astype(BF16), q_scaled], axis=1)
            rhs = jnp.concatenate([v[hd], sf.astype(BF16)], axis=0)
            y_ref[pl.ds(r0, CHUNK), cols[hd]] = _dot(lhs, rhs)
            sf_s[hd] = sf * tab_s[hd, _RT_DEC_F] + ds[hd][:, :RET_DV]
            ds_s[idx, hd] = ds[hd][:, RET_DV:]

    def bwd_chunk(q_ref, g_ref, y_ref, o_ref, r0, idx):
        for hd in range(RET_HEADS):
            cols = slice(hd * RET_DK, (hd + 1) * RET_DK)
            sb = sb_s[hd]
            if o_ref is not None:
                q = q_ref[pl.ds(r0, CHUNK), cols]
                y = y_ref[pl.ds(r0, CHUNK), cols] + tab_s[hd, _RT_INTER_B] * _dot(q, sb.astype(BF16))
                gate = _silu(g_ref[pl.ds(r0, CHUNK), cols].astype(F32))
                o_ref[pl.ds(r0, CHUNK), cols] = (_rms(y) * gate).astype(BF16)
            sb_s[hd] = sb * tab_s[hd, _RT_DEC_B] + ds_s[idx, hd]

    for c in range(n_ctx_chunks):
        fwd_chunk(qc_ref, kc_ref, vc_ref, yc_s, c * CHUNK, c)

    def lat_fwd(c, carry):
        fwd_chunk(ql_ref, kl_ref, vl_ref, yl_s, pl.multiple_of(c * CHUNK, CHUNK), n_ctx_chunks + c)
        return carry

    lax.fori_loop(0, n_lat_chunks, lat_fwd, 0, unroll=SCAN_UNROLL)

    for c in reversed(range(n_ctx_chunks)):
        bwd_chunk(qc_ref, gc_ref, yc_s, oc_ref, c * CHUNK, c)

    def lat_bwd(t, carry):
        c = n_lat_chunks - 1 - t
        bwd_chunk(ql_ref, gl_ref, yl_s, ol_ref, pl.multiple_of(c * CHUNK, CHUNK), n_ctx_chunks + c)
        return carry

    lax.fori_loop(0, n_lat_chunks, lat_bwd, 0, unroll=SCAN_UNROLL)


def _retention(tc, tl, decay_f, decay_b, batch, ctx_out):
    lc = tc["rq"].shape[0] // batch
    ll = tl["rq"].shape[0] // batch
    cspec = pl.BlockSpec((lc, RET_W), lambda b: (b, 0))
    lspec = pl.BlockSpec((ll, RET_W), lambda b: (b, 0))
    pspec = pl.BlockSpec((RET_HEADS, 1, LANE), lambda b: (0, 0, 0))
    out_specs = [lspec]
    out_shape = [jax.ShapeDtypeStruct((batch * ll, RET_W), BF16)]
    if ctx_out:
        out_specs = [cspec] + out_specs
        out_shape = [jax.ShapeDtypeStruct((batch * lc, RET_W), BF16)] + out_shape
    bcast = lambda p: jnp.broadcast_to(p.astype(F32)[:, None, None], (RET_HEADS, 1, LANE))
    outs = pl.pallas_call(
        functools.partial(_ret_kernel, lc // CHUNK, ll // CHUNK, ctx_out),
        grid=(batch,),
        in_specs=[cspec] * 4 + [lspec] * 4 + [pspec, pspec],
        out_specs=out_specs,
        out_shape=out_shape,
        scratch_shapes=[
            pltpu.VMEM((lc, RET_W), F32), pltpu.VMEM((ll, RET_W), F32),
            pltpu.VMEM((RET_HEADS, RET_DK, RET_DV), F32), pltpu.VMEM((RET_HEADS, RET_DK, RET_DV), F32),
            pltpu.VMEM((RET_HEADS, 7, CHUNK, CHUNK), F32),
            pltpu.VMEM(((lc + ll) // CHUNK, RET_HEADS, RET_DK, RET_DV), F32),
        ],
        compiler_params=_cparams(("parallel",)),
        name="retention",
    )(tc["rq"], tc["rk"], tc["rv"], tc["rg"], tl["rq"], tl["rk"], tl["rv"], tl["rg"],
      bcast(decay_f), bcast(decay_b))
    return (outs[0], outs[1]) if ctx_out else (None, outs[0])


def _expand_heads(v, e_ref):
    hi = v.astype(BF16)
    lo = (v - hi.astype(F32)).astype(BF16)
    e = e_ref[...]
    return _dot(hi, e) + _dot(lo, e)


def _cumsum_rows(v):
    rows = lax.broadcasted_iota(jnp.int32, v.shape, 0)
    s = 1
    while s < v.shape[0]:
        v = v + jnp.where(rows >= s, pltpu.roll(v, s, 0), 0.0)
        s *= 2
    return v


def _ssd_kernel(n_ctx_chunks, n_lat_chunks, ctx_out, *refs):
    (xc_ref, dtc_ref, zc_ref, xl_ref, dtl_ref, zl_ref,
     cw_ref, cb_ref, bias_ref, alog_ref, skip_ref, ng_ref, ef_ref, eb_ref, shift_ref) = refs[:15]
    if ctx_out:
        oc_ref, ol_ref, tok_s, y_s, sf_s, sb_s = refs[15:]
    else:
        ol_ref, tok_s, y_s, sf_s, sb_s = refs[15:]
        oc_ref = None
    lc = n_ctx_chunks * CHUNK

    ii = lax.broadcasted_iota(jnp.int32, (CHUNK, CHUNK), 0)
    jj = lax.broadcasted_iota(jnp.int32, (CHUNK, CHUNK), 1)
    lower = jj < ii
    upper = jj > ii
    neg_a = -jnp.exp(alog_ref[...])
    halo = CONV_HALO

    def conv_silu(x_ref, c, n_chunks):
        if isinstance(c, int):
            r0 = c * CHUNK
            prev0 = max(r0 - halo, 0)
            next0 = min(r0 + CHUNK, (n_chunks - 1) * CHUNK)
        else:
            r0 = pl.multiple_of(c * CHUNK, CHUNK)
            prev0 = pl.multiple_of(jnp.maximum(r0 - halo, 0), halo)
            next0 = pl.multiple_of(jnp.minimum(r0 + CHUNK, (n_chunks - 1) * CHUNK), halo)
        first = c == 0
        last = c == n_chunks - 1
        centre = x_ref[pl.ds(r0, CHUNK), :]
        prev = x_ref[pl.ds(prev0, halo), :]
        nxt = x_ref[pl.ds(next0, halo), :]
        prev = jnp.where(first, jnp.zeros_like(prev), prev)
        nxt = jnp.where(last, jnp.zeros_like(nxt), nxt)
        pad = jnp.zeros((CONV_EXT - CHUNK - 2 * halo, CONV_CH), BF16)
        ext = jnp.concatenate([centre, prev, nxt, pad], axis=0)
        acc = cb_ref[...] + cw_ref[SSD_CONV // 2:SSD_CONV // 2 + 1, :] * centre.astype(F32)
        taps = [k for k in range(SSD_CONV) if k != SSD_CONV // 2]
        for n, k in enumerate(taps):
            acc = acc + cw_ref[k:k + 1, :] * _dot(shift_ref[n], ext)
        return _silu(acc)

    def decays(dt_ref, r0):
        dt = jax.nn.softplus(dt_ref[pl.ds(r0, CHUNK), :] + bias_ref[...])
        la = dt * neg_a
        a_inc = _cumsum_rows(la)
        total = a_inc[CHUNK - 1:CHUNK, :]
        return dt, la, a_inc, total

    def decay_row(total, e_ref):
        return _expand_heads(jnp.broadcast_to(jnp.exp(total), (8, LANE)), e_ref)[0:1, :]

    def fwd_chunk(x_ref, dt_ref, c, n_chunks, base):
        if isinstance(c, int):
            r0 = c * CHUNK
            t0 = base + r0
        else:
            r0 = pl.multiple_of(c * CHUNK, CHUNK)
            t0 = pl.multiple_of(base + r0, CHUNK)
        tok = conv_silu(x_ref, c, n_chunks)
        tok_b = tok.astype(BF16)
        tok_s[pl.ds(t0, CHUNK), :] = tok_b
        xs = tok_b[:, :SSD_W]
        dt, la, a_inc, total = decays(dt_ref, r0)
        a_exc = a_inc - la
        log2e = math.log2(math.e)
        ldt = jnp.log2(dt)
        col_term = a_inc * log2e
        col_term_b = a_exc * log2e
        row_t = (jnp.where(lax.broadcasted_iota(jnp.int32, (CHUNK, LANE), 1) < SSD_HEADS,
                           ldt - col_term, ldt + col_term_b)).T
        dt_t = dt.T
        groups = range(SSD_GROUPS)
        bs = [tok_b[:, SSD_W + g * SSD_STATE:SSD_W + (g + 1) * SSD_STATE] for g in groups]
        cs = [tok_b[:, SSD_W + SSD_BC + g * SSD_STATE:SSD_W + SSD_BC + (g + 1) * SSD_STATE] for g in groups]
        gsl = [slice(g * GROUP_W, (g + 1) * GROUP_W) for g in groups]
        gmat = [_dot_nt(cs[g], bs[g]) for g in groups]
        inter = [_dot(cs[g], sf_s[:, gsl[g]].astype(BF16)) for g in groups]
        inter_scale = _expand_heads(jnp.exp(a_inc), ef_ref)
        state_scale = _expand_heads(jnp.exp(total - a_inc) * dt, ef_ref)
        dec = decay_row(total, ef_ref)
        xw = (xs.astype(F32) * state_scale).astype(BF16)
        dstate = [_dot_tn(bs[g], xw[:, gsl[g]]) for g in groups]
        ms = []
        for hd in range(SSD_HEADS):
            hb = SSD_HEADS + hd
            diag = jnp.log2(dt_t[hd:hd + 1, :] + dt_t[hb:hb + 1, :])
            expo = jnp.where(lower, col_term[:, hd:hd + 1] + row_t[hd:hd + 1, :],
                             jnp.where(upper, row_t[hb:hb + 1, :] - col_term_b[:, hb:hb + 1], diag))
            ms.append((gmat[hd // HEADS_PER_GROUP] * jnp.exp2(expo)).astype(BF16))
        ys = []
        for pair in range(SSD_HEADS // 2):
            col = 2 * pair * SSD_HEAD_DIM
            xpair = xs[:, col:col + 2 * SSD_HEAD_DIM]
            lane = lax.broadcasted_iota(jnp.int32, xpair.shape, 1)
            zero = jnp.zeros_like(xpair)
            rhs = jnp.concatenate([jnp.where(lane < SSD_HEAD_DIM, xpair, zero),
                                   jnp.where(lane < SSD_HEAD_DIM, zero, xpair)], axis=0)
            ys.append(_dot(jnp.concatenate(ms[2 * pair:2 * pair + 2], axis=1), rhs))
        y_s[pl.ds(t0, CHUNK), :] = jnp.concatenate(ys, axis=1) + inter_scale * jnp.concatenate(inter, axis=1)
        for g in groups:
            sf_s[:, gsl[g]] = sf_s[:, gsl[g]] * dec[:, gsl[g]] + dstate[g]

    def bwd_chunk(dt_ref, z_ref, o_ref, c, base):
        if isinstance(c, int):
            r0 = c * CHUNK
            t0 = base + r0
        else:
            r0 = pl.multiple_of(c * CHUNK, CHUNK)
            t0 = pl.multiple_of(base + r0, CHUNK)
        tok_b = tok_s[pl.ds(t0, CHUNK), :]
        xs = tok_b[:, :SSD_W].astype(F32)
        dt, la, a_inc, total = decays(dt_ref, r0)
        a_exc = a_inc - la
        groups = range(SSD_GROUPS)
        gsl = [slice(g * GROUP_W, (g + 1) * GROUP_W) for g in groups]
        bs = [tok_b[:, SSD_W + g * SSD_STATE:SSD_W + (g + 1) * SSD_STATE] for g in groups]
        xw = (xs * _expand_heads(jnp.exp(a_exc) * dt, eb_ref)).astype(BF16)
        dstate = [_dot_tn(bs[g], xw[:, gsl[g]]) for g in groups]
        dec = decay_row(total, eb_ref)
        if o_ref is not None:
            cs = [tok_b[:, SSD_W + SSD_BC + g * SSD_STATE:SSD_W + SSD_BC + (g + 1) * SSD_STATE] for g in groups]
            inter = [_dot(cs[g], sb_s[:, gsl[g]].astype(BF16)) for g in groups]
            y = (y_s[pl.ds(t0, CHUNK), :]
                 + _expand_heads(jnp.exp(total - a_exc), eb_ref) * jnp.concatenate(inter, axis=1))
            y = (y + skip_ref[...] * xs) * _silu(z_ref[pl.ds(r0, CHUNK), :].astype(F32))
            o_ref[pl.ds(r0, CHUNK), :] = (_rms(y) * ng_ref[...]).astype(BF16)
        for g in groups:
            sb_s[:, gsl[g]] = sb_s[:, gsl[g]] * dec[:, gsl[g]] + dstate[g]

    sf_s[...] = jnp.zeros_like(sf_s)
    sb_s[...] = jnp.zeros_like(sb_s)

    for c in range(n_ctx_chunks):
        fwd_chunk(xc_ref, dtc_ref, c, n_ctx_chunks, 0)

    def lat_fwd(c, carry):
        fwd_chunk(xl_ref, dtl_ref, c, n_lat_chunks, lc)
        return carry

    lax.fori_loop(0, n_lat_chunks, lat_fwd, 0, unroll=SCAN_UNROLL)

    for c in reversed(range(n_ctx_chunks)):
        bwd_chunk(dtc_ref, zc_ref, oc_ref, c, 0)

    def lat_bwd(t, carry):
        bwd_chunk(dtl_ref, zl_ref, ol_ref, n_lat_chunks - 1 - t, lc)
        return carry

    lax.fori_loop(0, n_lat_chunks, lat_bwd, 0, unroll=SCAN_UNROLL)


def _conv_shift_matrices():
    taps = [k for k in range(SSD_CONV) if k != SSD_CONV // 2]
    mats = np.zeros((len(taps), CHUNK, CONV_EXT), np.float32)
    for n, k in enumerate(taps):
        for i in range(CHUNK):
            src = i + k - SSD_CONV // 2
            if src < 0:
                col = CHUNK + CONV_HALO + src
            elif src >= CHUNK:
                col = CHUNK + CONV_HALO + (src - CHUNK)
            else:
                col = src
            mats[n, i, col] = 1.0
    return mats


def _head_lane_vec(f, b):
    v = jnp.zeros((1, LANE), F32)
    v = v.at[0, :SSD_HEADS].set(f.astype(F32))
    return v.at[0, SSD_HEADS:2 * SSD_HEADS].set(b.astype(F32))


def _ssd(tc, tl, p, batch, ctx_out):
    lc = tc["xbc"].shape[0] // batch
    ll = tl["xbc"].shape[0] // batch
    rows = lambda n, w: pl.BlockSpec((n, w), lambda b: (b, 0))
    vec = lambda w: pl.BlockSpec((1, w), lambda b: (0, 0))
    whole = lambda a: pl.BlockSpec(a.shape, lambda b: (0,) * a.ndim)
    head_of_col = np.arange(SSD_W) // SSD_HEAD_DIM
    ef = jnp.asarray(np.arange(LANE)[:, None] == head_of_col[None, :], BF16)
    eb = jnp.asarray(np.arange(LANE)[:, None] == head_of_col[None, :] + SSD_HEADS, BF16)
    shift = jnp.asarray(_conv_shift_matrices(), BF16)
    conv_w = p["conv_w"].astype(F32)
    conv_b = p["conv_b"].astype(F32).reshape(1, CONV_CH)
    bias = _head_lane_vec(p["dt_bias_f"], p["dt_bias_b"])
    alog = _head_lane_vec(p["a_log_f"], p["a_log_b"])
    skip = jnp.repeat(p["d_skip"].astype(F32), SSD_HEAD_DIM).reshape(1, SSD_W)
    norm_g = p["ssd_norm_g"].astype(F32).reshape(1, SSD_W)
    out_specs = [rows(ll, SSD_W)]
    out_shape = [jax.ShapeDtypeStruct((batch * ll, SSD_W), BF16)]
    if ctx_out:
        out_specs = [rows(lc, SSD_W)] + out_specs
        out_shape = [jax.ShapeDtypeStruct((batch * lc, SSD_W), BF16)] + out_shape
    outs = pl.pallas_call(
        functools.partial(_ssd_kernel, lc // CHUNK, ll // CHUNK, ctx_out),
        grid=(batch,),
        in_specs=[rows(lc, CONV_CH), rows(lc, LANE), rows(lc, SSD_W),
                  rows(ll, CONV_CH), rows(ll, LANE), rows(ll, SSD_W),
                  whole(conv_w), vec(CONV_CH), vec(LANE), vec(LANE), vec(SSD_W), vec(SSD_W),
                  whole(ef), whole(eb), whole(shift)],
        out_specs=out_specs,
        out_shape=out_shape,
        scratch_shapes=[
            pltpu.VMEM((lc + ll, CONV_CH), BF16), pltpu.VMEM((lc + ll, SSD_W), F32),
            pltpu.VMEM((SSD_STATE, SSD_W), F32), pltpu.VMEM((SSD_STATE, SSD_W), F32),
        ],
        compiler_params=_cparams(("parallel",)),
        name="ssd_scan",
    )(tc["xbc"], tc["dt"], tc["z"], tl["xbc"], tl["dt"], tl["z"],
      conv_w, conv_b, bias, alog, skip, norm_g, ef, eb, shift)
    return (outs[0], outs[1]) if ctx_out else (None, outs[0])


def _out_proj_kernel(n_sub, att_ref, ret_ref, ssd_ref, w_ref, x_ref, g1_ref, pmg_ref, x1_ref):
    ts = x_ref.shape[0] // n_sub
    for c in range(n_sub):
        rows = slice(c * ts, (c + 1) * ts)
        m = (_dot(att_ref[rows, :], w_ref[0:ATT_W, :])
             + _dot(ret_ref[rows, :], w_ref[ATT_W:ATT_W + RET_W, :])
             + _dot(ssd_ref[rows, :], w_ref[ATT_W + RET_W:, :]))
        x1_ref[rows, :] = x_ref[rows, :] + g1_ref[...] * (_rms(m) * pmg_ref[...])


def _out_proj(att, ret, ssd, w_out_b, x2d, mod4, group_of_tile, post_mix_g, tm):
    rows, d = x2d.shape
    row_spec = lambda w: pl.BlockSpec((tm, w), lambda i: (i, 0))
    mod_spec = lambda k: pl.BlockSpec((None, None, 1, d), lambda i: (group_of_tile(i), k, 0, 0))
    vec_spec = pl.BlockSpec((1, d), lambda i: (0, 0))
    n_sub = 2 if tm % 32 == 0 else 1
    return pl.pallas_call(
        functools.partial(_out_proj_kernel, n_sub),
        grid=(rows // tm,),
        in_specs=[row_spec(ATT_W), row_spec(RET_W), row_spec(SSD_W),
                  pl.BlockSpec(w_out_b.shape, lambda i: (0, 0)),
                  row_spec(d), mod_spec(2), vec_spec],
        out_specs=row_spec(d),
        out_shape=jax.ShapeDtypeStruct((rows, d), F32),
        compiler_params=_cparams(("parallel",)),
        name="out_proj",
    )(att, ret, ssd, w_out_b, x2d, mod4, post_mix_g)


def _ffn_kernel(x1_ref, sc_ref, sh_ref, pfg_ref, wg_ref, wu_ref, wd_ref, g2_ref, pg_ref, o_ref, hf_s):
    j = pl.program_id(1)
    last_j = pl.num_programs(1) - 1
    tm, n_out = o_ref.shape
    ts = _pick_tile(tm, FFN_SUB_ROWS)
    tr = _pick_tile(ts, FFN_EPILOGUE_ROWS)
    tn = _pick_tile(n_out, FFN_DOWN_COLS)

    def hidden(rows):
        hf = hf_s[rows, :]
        return (_silu(_dot(hf, wg_ref[...])) * _dot(hf, wu_ref[...])).astype(BF16)

    @pl.when(j == 0)
    def _():
        gain = pfg_ref[...] * (1.0 + sc_ref[...])
        for c in range(tm // ts):
            for r in range(ts // tr):
                rows = slice(c * ts + r * tr, c * ts + (r + 1) * tr)
                hf_s[rows, :] = (_rms(x1_ref[rows, :]) * gain + sh_ref[...]).astype(BF16)
            rows = slice(c * ts, (c + 1) * ts)
            o_ref[rows, :] = _dot(hidden(rows), wd_ref[...])

    @pl.when(jnp.logical_and(j > 0, j < last_j))
    def _():
        a = hidden(slice(0, tm))
        for n in range(n_out // tn):
            o_ref[:, n * tn:(n + 1) * tn] += _dot(a, wd_ref[:, n * tn:(n + 1) * tn])

    @pl.when(j == last_j)
    def _():
        gain = g2_ref[...] * pg_ref[...]
        for c in range(tm // ts):
            rows = slice(c * ts, (c + 1) * ts)
            f = o_ref[rows, :] + _dot(hidden(rows), wd_ref[...])
            o_ref[rows, :] = x1_ref[rows, :] + _rms(f) * gain


def _ffn(x1, wg, wu, wd, mod4, group_of_tile, pre_ffn_g, post_ffn_g, tm, tf):
    rows, d = x1.shape
    dff = wg.shape[1]
    assert dff // tf >= 2
    mod_spec = lambda k: pl.BlockSpec((None, None, 1, d), lambda i, j: (group_of_tile(i), k, 0, 0))
    vec_spec = pl.BlockSpec((1, d), lambda i, j: (0, 0))
    return pl.pallas_call(
        _ffn_kernel,
        grid=(rows // tm, dff // tf),
        in_specs=[
            pl.BlockSpec((tm, d), lambda i, j: (i, 0)),
            mod_spec(4), mod_spec(3), vec_spec,
            pl.BlockSpec((d, tf), lambda i, j: (0, j)),
            pl.BlockSpec((d, tf), lambda i, j: (0, j)),
            pl.BlockSpec((tf, d), lambda i, j: (j, 0)),
            mod_spec(5), vec_spec,
        ],
        out_specs=pl.BlockSpec((tm, d), lambda i, j: (i, 0)),
        out_shape=jax.ShapeDtypeStruct((rows, d), F32),
        scratch_shapes=[pltpu.VMEM((tm, d), BF16)],
        compiler_params=_cparams(("parallel", "arbitrary")),
        name="swiglu_ffn",
    )(x1, mod4, mod4, pre_ffn_g, wg, wu, wd, mod4, post_ffn_g)


def _cast_kernel(w_ref, o_ref):
    o_ref[...] = w_ref[...].astype(BF16)


def _cast_layer(w_stack, layer, n_rows=None):
    _, rows, cols = w_stack.shape
    n_rows = rows if n_rows is None else n_rows
    tr = _pick_tile(n_rows, 256)
    return pl.pallas_call(
        _cast_kernel,
        grid=(n_rows // tr,),
        in_specs=[pl.BlockSpec((None, tr, cols), lambda i: (layer, i, 0))],
        out_specs=pl.BlockSpec((tr, cols), lambda i: (i, 0)),
        out_shape=jax.ShapeDtypeStruct((n_rows, cols), BF16),
        compiler_params=_cparams(("parallel",)),
        name="cast_weight",
    )(w_stack)


def _rope_tables(seq_len):
    rows = seq_len // GRID_W
    row = jnp.repeat(jnp.arange(rows, dtype=F32), GRID_W)
    col = jnp.tile(jnp.arange(GRID_W, dtype=F32), rows)
    n_freq = HEAD_DIM // 4
    inv = ROPE_THETA ** (-jnp.arange(n_freq, dtype=F32) / n_freq)
    ang = jnp.concatenate([row[:, None] * inv, col[:, None] * inv], axis=-1)
    cos, sin = jnp.cos(ang), jnp.sin(ang)
    cos_t = jnp.repeat(cos, 2, axis=-1)
    sin_t = jnp.stack([-sin, sin], axis=-1).reshape(seq_len, HEAD_DIM)
    return cos_t, sin_t


def _pick_tile(n, target):
    t = min(n, target)
    while n % t:
        t //= 2
    return t


def _cast_tail_kernel(n_valid, w_ref, o_ref):
    row = lax.broadcasted_iota(jnp.int32, w_ref.shape, 0)
    o_ref[...] = jnp.where(row < n_valid, w_ref[...], 0.0).astype(BF16)


def _cast_layer_tail(w_stack, layer, row0):
    _, rows, cols = w_stack.shape
    assert row0 % LANE == 0 and 0 < rows - row0 <= LANE
    return pl.pallas_call(
        functools.partial(_cast_tail_kernel, rows - row0),
        grid=(1,),
        in_specs=[pl.BlockSpec((None, LANE, cols), lambda i: (layer, row0 // LANE, 0))],
        out_specs=pl.BlockSpec((LANE, cols), lambda i: (0, 0)),
        out_shape=jax.ShapeDtypeStruct((LANE, cols), BF16),
        compiler_params=_cparams(("arbitrary",)),
        name="cast_weight_tail",
    )(w_stack)


def _layer_weights(l, w_in, w_out, w_gate, w_up, w_down):
    w_in_t = jnp.swapaxes(w_in, 1, 2)
    w_in_b = _cast_layer(w_in_t, l, OFF_DT)
    w_dt_b = _cast_layer_tail(w_in_t, l, OFF_DT)
    return (w_in_b, w_dt_b, _cast_layer(w_out, l), _cast_layer(w_gate, l), _cast_layer(w_up, l),
            _cast_layer(w_down, l))


def kernel(x, c, ctx, c_ctx, w_mod, b_mod, pre_mix_g, post_mix_g, pre_ffn_g, post_ffn_g, w_in, q_norm_g, k_norm_g, ret_decay_f, ret_decay_b, conv_w, conv_b, dt_bias_f, dt_bias_b, a_log_f, a_log_b, d_skip, ssd_norm_g, w_out, w_gate, w_up, w_down):
    batch, seq, d = x.shape
    lc = ctx.shape[1]
    depth = w_mod.shape[0]
    assert batch < N_MOD_ROWS and seq % CHUNK == 0 and lc % CHUNK == 0 and seq % GRID_W == 0

    c16 = jnp.zeros((N_MOD_ROWS, d), F32).at[:batch].set(c.astype(F32)).at[batch].set(c_ctx.astype(F32))
    mod_all = _modulation(c16, w_mod.astype(F32), b_mod.astype(F32))
    rope_tabs = _rope_tables(seq)

    tm_lat = _pick_tile(seq, 512)
    tm_ctx = _pick_tile(batch * lc, 512)
    tm_ffn_lat = _pick_tile(seq, 1024)
    tm_ffn_ctx = _pick_tile(batch * lc, 1024)
    lat_group = lambda tm: (lambda i: i // (seq // tm))
    ctx_group = lambda tm: (lambda i: batch)
    tq_lat = _pick_tile(seq, 1024)
    tq_ctx = _pick_tile(lc, 256)
    sub_lat = max(tq_lat // 128, 1)
    sub_ctx = 2 if tq_ctx % 32 == 0 else 1
    tf = _pick_tile(w_gate.shape[2], 512)

    xl = x.reshape(batch * seq, d).astype(F32)
    xc = ctx.reshape(batch * lc, d).astype(F32)
    vec = lambda v: v.astype(F32).reshape(1, d)

    for l in range(depth):
        last = l == depth - 1
        w_in_b, w_dt_b, w_out_b, wg, wu, wd = _layer_weights(l, w_in, w_out, w_gate, w_up, w_down)
        qg = q_norm_g[l].astype(F32).reshape(1, HEAD_DIM)
        kg = k_norm_g[l].astype(F32).reshape(1, HEAD_DIM)
        mod4 = mod_all[l].reshape(N_MOD_ROWS, 6, 1, d)
        p = dict(conv_w=conv_w[l], conv_b=conv_b[l], dt_bias_f=dt_bias_f[l], dt_bias_b=dt_bias_b[l],
                 a_log_f=a_log_f[l], a_log_b=a_log_b[l], d_skip=d_skip[l], ssd_norm_g=ssd_norm_g[l])

        tl = _in_proj(xl, mod4, lat_group(tm_lat), vec(pre_mix_g[l]), w_in_b, w_dt_b, qg, kg, rope_tabs, tm_lat)
        tc = _in_proj(xc, mod4, ctx_group(tm_ctx), vec(pre_mix_g[l]), w_in_b, w_dt_b, qg, kg, None, tm_ctx)

        att_l = _attention(tl["aq"], [(tc["ak"], tc["av"]), (tl["ak"], tl["av"])], batch, tq_lat, sub_lat)
        ret_c, ret_l = _retention(tc, tl, ret_decay_f[l], ret_decay_b[l], batch, not last)
        ssd_c, ssd_l = _ssd(tc, tl, p, batch, not last)

        x1 = _out_proj(att_l, ret_l, ssd_l, w_out_b, xl, mod4, lat_group(tm_lat), vec(post_mix_g[l]), tm_lat)
        xl = _ffn(x1, wg, wu, wd, mod4, lat_group(tm_ffn_lat), vec(pre_ffn_g[l]), vec(post_ffn_g[l]),
                  tm_ffn_lat, tf)

        if not last:
            att_c = _attention(tc["aq"], [(tc["ak"], tc["av"])], batch, tq_ctx, sub_ctx)
            x1c = _out_proj(att_c, ret_c, ssd_c, w_out_b, xc, mod4, ctx_group(tm_ctx), vec(post_mix_g[l]), tm_ctx)
            xc = _ffn(x1c, wg, wu, wd, mod4, ctx_group(tm_ffn_ctx), vec(pre_ffn_g[l]), vec(post_ffn_g[l]),
                      tm_ffn_ctx, tf)

    return xl.reshape(batch, seq, d).astype(x.dtype)
```

```python
import functools
import math

import numpy as np
import jax
import jax.numpy as jnp
from jax import lax
from jax.experimental import pallas as pl
from jax.experimental.pallas import tpu as pltpu

F32 = jnp.float32
BF16 = jnp.bfloat16

GRID_W = 64
HEAD_DIM = 128
ATT_HEADS = 6
ATT_KV_HEADS = 2
ATT_REP = ATT_HEADS // ATT_KV_HEADS
RET_HEADS = 4
RET_DK = 128
RET_DV = 128
SSD_HEADS = 12
SSD_HEAD_DIM = 64
SSD_GROUPS = 2
SSD_STATE = 128
SSD_CONV = 5
ATT_W = ATT_HEADS * HEAD_DIM
KV_W = ATT_KV_HEADS * HEAD_DIM
RET_QK_W = RET_HEADS * RET_DK
RET_W = RET_HEADS * RET_DV
SSD_W = SSD_HEADS * SSD_HEAD_DIM
SSD_BC = SSD_GROUPS * SSD_STATE
CONV_CH = SSD_W + 2 * SSD_BC
GROUP_W = SSD_W // SSD_GROUPS
HEADS_PER_GROUP = SSD_HEADS // SSD_GROUPS
CHUNK = 128
ROPE_THETA = 10000.0
EPS = 1e-6
LANE = 128
FFN_DOWN_COLS = 512
FFN_EPILOGUE_ROWS = 128
OUT_PROJ_SUB_ROWS = 256
FFN_SUB_ROWS = 256
SCAN_UNROLL = 4
CONV_HALO = 16
CONV_EXT = 256

OFF_AQ = 0
OFF_AK = OFF_AQ + ATT_W
OFF_AV = OFF_AK + KV_W
OFF_RQ = OFF_AV + KV_W
OFF_RK = OFF_RQ + RET_QK_W
OFF_RV = OFF_RK + RET_QK_W
OFF_RG = OFF_RV + RET_W
OFF_Z = OFF_RG + RET_W
OFF_XBC = OFF_Z + SSD_W
OFF_DT = OFF_XBC + CONV_CH
D_IN = OFF_DT + 2 * SSD_HEADS
D_IN_PAD = OFF_DT + LANE

VMEM_LIMIT = 60 * 1024 * 1024
N_MOD_ROWS = 16


def _cparams(sem):
    return pltpu.CompilerParams(dimension_semantics=sem, vmem_limit_bytes=VMEM_LIMIT)


def _silu(v):
    return v * jax.nn.sigmoid(v)


def _rms(v):
    return v * lax.rsqrt(jnp.mean(v * v, axis=-1, keepdims=True) + EPS)


def _dot(a, b):
    return jnp.dot(a, b, preferred_element_type=F32)


def _dot_nt(a, b):
    return lax.dot_general(a, b, (((1,), (1,)), ((), ())), preferred_element_type=F32)


def _dot_tn(a, b):
    return lax.dot_general(a, b, (((0,), (0,)), ((), ())), preferred_element_type=F32)


def _mod_kernel(c_ref, w_ref, b_ref, o_ref):
    a = _silu(c_ref[...]).astype(BF16)
    o_ref[...] = _dot(a, w_ref[...].astype(BF16)) + b_ref[...]


def _modulation(c16, w_mod, b_mod):
    depth, d, n = w_mod.shape
    tn = _pick_tile(n, 1024)
    return pl.pallas_call(
        _mod_kernel,
        grid=(depth, n // tn),
        in_specs=[
            pl.BlockSpec((N_MOD_ROWS, d), lambda l, j: (0, 0)),
            pl.BlockSpec((None, d, tn), lambda l, j: (l, 0, j)),
            pl.BlockSpec((None, 1, tn), lambda l, j: (l, 0, j)),
        ],
        out_specs=pl.BlockSpec((None, N_MOD_ROWS, tn), lambda l, j: (l, 0, j)),
        out_shape=jax.ShapeDtypeStruct((depth, N_MOD_ROWS, n), F32),
        compiler_params=_cparams(("parallel", "parallel")),
        name="adaln_modulation",
    )(c16, w_mod, b_mod.reshape(depth, 1, n))


def _in_proj_kernel(use_rope, *refs):
    if use_rope:
        (x_ref, sc_ref, sh_ref, g_ref, w_ref, wdt_ref, qg_ref, kg_ref, cos_ref, sin_ref,
         aq_ref, ak_ref, av_ref, rq_ref, rk_ref, rv_ref, rg_ref, z_ref, xbc_ref, dt_ref) = refs
        cos = cos_ref[...]
        sin = sin_ref[...]
        even_lane = (lax.broadcasted_iota(jnp.int32, cos.shape, 1) & 1) == 0
    else:
        (x_ref, sc_ref, sh_ref, g_ref, w_ref, wdt_ref, qg_ref, kg_ref,
         aq_ref, ak_ref, av_ref, rq_ref, rk_ref, rv_ref, rg_ref, z_ref, xbc_ref, dt_ref) = refs

    x = x_ref[...]
    h = (_rms(x) * g_ref[...] * (1.0 + sc_ref[...]) + sh_ref[...]).astype(BF16)

    def proj(lo, width):
        return _dot_nt(h, w_ref[lo:lo + width, :])

    def rope(t):
        if not use_rope:
            return t
        partner = jnp.where(even_lane, pltpu.roll(t, HEAD_DIM - 1, 1), pltpu.roll(t, 1, 1))
        return t * cos + partner * sin

    att_scale = HEAD_DIM ** -0.5 * math.log2(math.e)
    ret_scale = RET_DK ** -0.5

    acc = proj(OFF_AQ, ATT_W)
    for hd in range(ATT_HEADS):
        t = acc[:, hd * HEAD_DIM:(hd + 1) * HEAD_DIM]
        t = rope(_rms(t) * qg_ref[...]) * att_scale
        aq_ref[:, hd * HEAD_DIM:(hd + 1) * HEAD_DIM] = t.astype(BF16)

    acc = proj(OFF_AK, KV_W)
    for hd in range(ATT_KV_HEADS):
        t = acc[:, hd * HEAD_DIM:(hd + 1) * HEAD_DIM]
        t = rope(_rms(t) * kg_ref[...])
        ak_ref[:, hd * HEAD_DIM:(hd + 1) * HEAD_DIM] = t.astype(BF16)

    av_ref[...] = proj(OFF_AV, KV_W).astype(BF16)

    acc = proj(OFF_RQ, RET_QK_W)
    for hd in range(RET_HEADS):
        t = rope(acc[:, hd * RET_DK:(hd + 1) * RET_DK])
        rq_ref[:, hd * RET_DK:(hd + 1) * RET_DK] = t.astype(BF16)

    acc = proj(OFF_RK, RET_QK_W)
    for hd in range(RET_HEADS):
        t = rope(acc[:, hd * RET_DK:(hd + 1) * RET_DK] * ret_scale)
        rk_ref[:, hd * RET_DK:(hd + 1) * RET_DK] = t.astype(BF16)

    rv_ref[...] = proj(OFF_RV, RET_W).astype(BF16)
    rg_ref[...] = proj(OFF_RG, RET_W).astype(BF16)
    z_ref[...] = proj(OFF_Z, SSD_W).astype(BF16)
    xbc_ref[...] = proj(OFF_XBC, CONV_CH).astype(BF16)
    dt_ref[...] = _dot_nt(h, wdt_ref[...])


def _in_proj(x2d, mod4, group_of_tile, pre_g, w_in_b, w_dt_b, qg, kg, rope_tabs, tm):
    rows, d = x2d.shape
    nt = rows // tm
    use_rope = rope_tabs is not None
    row_spec = lambda w: pl.BlockSpec((tm, w), lambda i: (i, 0))
    mod_spec = lambda k: pl.BlockSpec((None, None, 1, d), lambda i: (group_of_tile(i), k, 0, 0))
    vec_spec = lambda w: pl.BlockSpec((1, w), lambda i: (0, 0))
    in_specs = [
        row_spec(d), mod_spec(1), mod_spec(0), vec_spec(d),
        pl.BlockSpec((OFF_DT, d), lambda i: (0, 0), pipeline_mode=pl.Buffered(1)),
        pl.BlockSpec((LANE, d), lambda i: (0, 0)),
        vec_spec(HEAD_DIM), vec_spec(HEAD_DIM),
    ]
    args = [x2d, mod4, mod4, pre_g, w_in_b, w_dt_b, qg, kg]
    if use_rope:
        cos_t, sin_t = rope_tabs
        tiles_per_seq = cos_t.shape[0] // tm
        tab_spec = pl.BlockSpec((tm, HEAD_DIM), lambda i: (i % tiles_per_seq, 0))
        in_specs += [tab_spec, tab_spec]
        args += [cos_t, sin_t]
    widths = (ATT_W, KV_W, KV_W, RET_QK_W, RET_QK_W, RET_W, RET_W, SSD_W, CONV_CH)
    out_specs = [row_spec(w) for w in widths] + [row_spec(LANE)]
    out_shape = [jax.ShapeDtypeStruct((rows, w), BF16) for w in widths]
    out_shape.append(jax.ShapeDtypeStruct((rows, LANE), F32))
    outs = pl.pallas_call(
        functools.partial(_in_proj_kernel, use_rope),
        grid=(nt,),
        in_specs=in_specs,
        out_specs=out_specs,
        out_shape=out_shape,
        compiler_params=_cparams(("parallel",)),
        name="in_proj_rope" if use_rope else "in_proj",
    )(*args)
    names = ("aq", "ak", "av", "rq", "rk", "rv", "rg", "z", "xbc", "dt")
    return dict(zip(names, outs))


def _attn_kernel(n_kv, tq, n_sub, *refs):
    q_ref = refs[0]
    kv_refs = refs[1:1 + 2 * n_kv]
    o_ref = refs[1 + 2 * n_kv]
    ts = tq // n_sub
    all_scores = []
    for c in range(n_sub):
        q = q_ref[c * ts:(c + 1) * ts, :]
        qs = jnp.concatenate([q[:, r * HEAD_DIM:(r + 1) * HEAD_DIM] for r in range(ATT_REP)], axis=0)
        all_scores.append([_dot_nt(qs, kv_refs[2 * s][...]) for s in range(n_kv)])
    v_ones = []
    for s in range(n_kv):
        v = kv_refs[2 * s + 1][...]
        v_ones.append(jnp.concatenate([v, jnp.ones_like(v)], axis=1))
    for c in range(n_sub):
        scores = all_scores[c]
        m = functools.reduce(jnp.maximum, [jnp.max(s, axis=-1, keepdims=True) for s in scores])
        acc = functools.reduce(jnp.add, [_dot(jnp.exp2(s - m).astype(BF16), v_ones[i]) for i, s in enumerate(scores)])
        out = acc[:, :HEAD_DIM] / acc[:, HEAD_DIM:]
        for r in range(ATT_REP):
            o_ref[c * ts:(c + 1) * ts, r * HEAD_DIM:(r + 1) * HEAD_DIM] = out[r * ts:(r + 1) * ts].astype(BF16)


def _attention(q2d, kv_list, batch, tq, n_sub):
    rows = q2d.shape[0]
    lq = rows // batch
    nq = lq // tq
    gw = ATT_REP * HEAD_DIM
    in_specs = [pl.BlockSpec((tq, gw), lambda b, g, i: (b * nq + i, g))]
    args = [q2d]
    for k2d, v2d in kv_list:
        lk = k2d.shape[0] // batch
        spec = pl.BlockSpec((lk, HEAD_DIM), lambda b, g, i: (b, g))
        in_specs += [spec, spec]
        args += [k2d, v2d]
    return pl.pallas_call(
        functools.partial(_attn_kernel, len(kv_list), tq, n_sub),
        grid=(batch, ATT_KV_HEADS, nq),
        in_specs=in_specs,
        out_specs=pl.BlockSpec((tq, gw), lambda b, g, i: (b * nq + i, g)),
        out_shape=jax.ShapeDtypeStruct((rows, ATT_W), BF16),
        compiler_params=_cparams(("parallel", "parallel", "parallel")),
        name="attention",
    )(*args)


_RT_INTRA, _RT_INTER_F, _RT_STATE_F, _RT_DEC_F, _RT_INTER_B, _RT_STATE_B, _RT_DEC_B = range(7)


def _ret_kernel(n_ctx_chunks, n_lat_chunks, ctx_out, *refs):
    (qc_ref, kc_ref, vc_ref, gc_ref, ql_ref, kl_ref, vl_ref, gl_ref, df_ref, db_ref) = refs[:10]
    if ctx_out:
        oc_ref, ol_ref, yc_s, yl_s, sf_s, sb_s, tab_s, ds_s = refs[10:]
    else:
        ol_ref, yc_s, yl_s, sf_s, sb_s, tab_s, ds_s = refs[10:]
        oc_ref = None

    ii = lax.broadcasted_iota(jnp.int32, (CHUNK, CHUNK), 0)
    jj = lax.broadcasted_iota(jnp.int32, (CHUNK, CHUNK), 1)
    dist = (ii - jj).astype(F32)
    rowi = ii.astype(F32)
    ones = jnp.ones((CHUNK, CHUNK), F32)
    for hd in range(RET_HEADS):
        lam_f = jnp.log1p(-jnp.exp2(df_ref[hd]))
        lam_b = jnp.log1p(-jnp.exp2(db_ref[hd]))
        tab_s[hd, _RT_INTRA] = (jnp.where(jj <= ii, jnp.exp(dist * lam_f), 0.0)
                                + jnp.where(jj >= ii, jnp.exp(-dist * lam_b), 0.0))
        tab_s[hd, _RT_INTER_F] = jnp.exp((rowi + 1.0) * lam_f)
        tab_s[hd, _RT_STATE_F] = jnp.exp((CHUNK - 1.0 - rowi) * lam_f)
        tab_s[hd, _RT_DEC_F] = ones * jnp.exp(CHUNK * lam_f)
        tab_s[hd, _RT_INTER_B] = jnp.exp((CHUNK - rowi) * lam_b)
        tab_s[hd, _RT_STATE_B] = jnp.exp(rowi * lam_b)
        tab_s[hd, _RT_DEC_B] = ones * jnp.exp(CHUNK * lam_b)

    sf_s[...] = jnp.zeros_like(sf_s)
    sb_s[...] = jnp.zeros_like(sb_s)

    def fwd_chunk(q_ref, k_ref, v_ref, y_ref, r0, idx):
        heads = range(RET_HEADS)
        cols = [slice(hd * RET_DK, (hd + 1) * RET_DK) for hd in heads]
        q = [q_ref[pl.ds(r0, CHUNK), cols[hd]] for hd in heads]
        k = [k_ref[pl.ds(r0, CHUNK), cols[hd]] for hd in heads]
        v = [v_ref[pl.ds(r0, CHUNK), cols[hd]] for hd in heads]
        s = [_dot_nt(q[hd], k[hd]) for hd in heads]
        ds = []
        for hd in heads:
            vf = v[hd].astype(F32)
            vw = jnp.concatenate([(vf * tab_s[hd, _RT_STATE_F]).astype(BF16),
                                  (vf * tab_s[hd, _RT_STATE_B]).astype(BF16)], axis=1)
            ds.append(_dot_tn(k[hd], vw))
        for hd in heads:
            sf = sf_s[hd]
            q_scaled = (q[hd].astype(F32) * tab_s[hd, _RT_INTER_F]).astype(BF16)
            lhs = jnp.concatenate([(s[hd] * tab_s[hd, _RT_INTRA]).astype(BF16), q_scaled], axis=1)
            rhs = jnp.concatenate([v[hd], sf.astype(BF16)], axis=0)
            y_ref[pl.ds(r0, CHUNK), cols[hd]] = _dot(lhs, rhs)
            sf_s[hd] = sf * tab_s[hd, _RT_DEC_F] + ds[hd][:, :RET_DV]
            ds_s[idx, hd] = ds[hd][:, RET_DV:]

    def bwd_chunk(q_ref, g_ref, y_ref, o_ref, r0, idx):
        for hd in range(RET_HEADS):
            cols = slice(hd * RET_DK, (hd + 1) * RET_DK)
            sb = sb_s[hd]
            if o_ref is not None:
                q = q_ref[pl.ds(r0, CHUNK), cols]
                y = y_ref[pl.ds(r0, CHUNK), cols] + tab_s[hd, _RT_INTER_B] * _dot(q, sb.astype(BF16))
                gate = _silu(g_ref[pl.ds(r0, CHUNK), cols].astype(F32))
                o_ref[pl.ds(r0, CHUNK), cols] = (_rms(y) * gate).astype(BF16)
            sb_s[hd] = sb * tab_s[hd, _RT_DEC_B] + ds_s[idx, hd]

    for c in range(n_ctx_chunks):
        fwd_chunk(qc_ref, kc_ref, vc_ref, yc_s, c * CHUNK, c)

    def lat_fwd(c, carry):
        fwd_chunk(ql_ref, kl_ref, vl_ref, yl_s, pl.multiple_of(c * CHUNK, CHUNK), n_ctx_chunks + c)
        return carry

    lax.fori_loop(0, n_lat_chunks, lat_fwd, 0, unroll=SCAN_UNROLL)

    for c in reversed(range(n_ctx_chunks)):
        bwd_chunk(qc_ref, gc_ref, yc_s, oc_ref, c * CHUNK, c)

    def lat_bwd(t, carry):
        c = n_lat_chunks - 1 - t
        bwd_chunk(ql_ref, gl_ref, yl_s, ol_ref, pl.multiple_of(c * CHUNK, CHUNK), n_ctx_chunks + c)
        return carry

    lax.fori_loop(0, n_lat_chunks, lat_bwd, 0, unroll=SCAN_UNROLL)


def _retention(tc, tl, decay_f, decay_b, batch, ctx_out):
    lc = tc["rq"].shape[0] // batch
    ll = tl["rq"].shape[0] // batch
    cspec = pl.BlockSpec((lc, RET_W), lambda b: (b, 0))
    lspec = pl.BlockSpec((ll, RET_W), lambda b: (b, 0))
    pspec = pl.BlockSpec((RET_HEADS, 1, LANE), lambda b: (0, 0, 0))
    out_specs = [lspec]
    out_shape = [jax.ShapeDtypeStruct((batch * ll, RET_W), BF16)]
    if ctx_out:
        out_specs = [cspec] + out_specs
        out_shape = [jax.ShapeDtypeStruct((batch * lc, RET_W), BF16)] + out_shape
    bcast = lambda p: jnp.broadcast_to(p.astype(F32)[:, None, None], (RET_HEADS, 1, LANE))
    outs = pl.pallas_call(
        functools.partial(_ret_kernel, lc // CHUNK, ll // CHUNK, ctx_out),
        grid=(batch,),
        in_specs=[cspec] * 4 + [lspec] * 4 + [pspec, pspec],
        out_specs=out_specs,
        out_shape=out_shape,
        scratch_shapes=[
            pltpu.VMEM((lc, RET_W), F32), pltpu.VMEM((ll, RET_W), F32),
            pltpu.VMEM((RET_HEADS, RET_DK, RET_DV), F32), pltpu.VMEM((RET_HEADS, RET_DK, RET_DV), F32),
            pltpu.VMEM((RET_HEADS, 7, CHUNK, CHUNK), F32),
            pltpu.VMEM(((lc + ll) // CHUNK, RET_HEADS, RET_DK, RET_DV), F32),
        ],
        compiler_params=_cparams(("parallel",)),
        name="retention",
    )(tc["rq"], tc["rk"], tc["rv"], tc["rg"], tl["rq"], tl["rk"], tl["rv"], tl["rg"],
      bcast(decay_f), bcast(decay_b))
    return (outs[0], outs[1]) if ctx_out else (None, outs[0])


def _expand_heads(v, e_ref):
    hi = v.astype(BF16)
    lo = (v - hi.astype(F32)).astype(BF16)
    e = e_ref[...]
    return _dot(hi, e) + _dot(lo, e)


def _cumsum_rows(v):
    rows = lax.broadcasted_iota(jnp.int32, v.shape, 0)
    s = 1
    while s < v.shape[0]:
        v = v + jnp.where(rows >= s, pltpu.roll(v, s, 0), 0.0)
        s *= 2
    return v


def _ssd_kernel(n_ctx_chunks, n_lat_chunks, ctx_out, *refs):
    (xc_ref, dtc_ref, zc_ref, xl_ref, dtl_ref, zl_ref,
     cw_ref, cb_ref, bias_ref, alog_ref, skip_ref, ng_ref, ef_ref, eb_ref, shift_ref) = refs[:15]
    if ctx_out:
        oc_ref, ol_ref, tok_s, y_s, sf_s, sb_s = refs[15:]
    else:
        ol_ref, tok_s, y_s, sf_s, sb_s = refs[15:]
        oc_ref = None
    lc = n_ctx_chunks * CHUNK

    ii = lax.broadcasted_iota(jnp.int32, (CHUNK, CHUNK), 0)
    jj = lax.broadcasted_iota(jnp.int32, (CHUNK, CHUNK), 1)
    lower = jj < ii
    upper = jj > ii
    neg_a = -jnp.exp(alog_ref[...])
    halo = CONV_HALO

    def conv_silu(x_ref, c, n_chunks):
        if isinstance(c, int):
            r0 = c * CHUNK
            prev0 = max(r0 - halo, 0)
            next0 = min(r0 + CHUNK, (n_chunks - 1) * CHUNK)
        else:
            r0 = pl.multiple_of(c * CHUNK, CHUNK)
            prev0 = pl.multiple_of(jnp.maximum(r0 - halo, 0), halo)
            next0 = pl.multiple_of(jnp.minimum(r0 + CHUNK, (n_chunks - 1) * CHUNK), halo)
        first = c == 0
        last = c == n_chunks - 1
        centre = x_ref[pl.ds(r0, CHUNK), :]
        prev = x_ref[pl.ds(prev0, halo), :]
        nxt = x_ref[pl.ds(next0, halo), :]
        prev = jnp.where(first, jnp.zeros_like(prev), prev)
        nxt = jnp.where(last, jnp.zeros_like(nxt), nxt)
        pad = jnp.zeros((CONV_EXT - CHUNK - 2 * halo, CONV_CH), BF16)
        ext = jnp.concatenate([centre, prev, nxt, pad], axis=0)
        acc = cb_ref[...] + cw_ref[SSD_CONV // 2:SSD_CONV // 2 + 1, :] * centre.astype(F32)
        taps = [k for k in range(SSD_CONV) if k != SSD_CONV // 2]
        for n, k in enumerate(taps):
            acc = acc + cw_ref[k:k + 1, :] * _dot(shift_ref[n], ext)
        return _silu(acc)

    def decays(dt_ref, r0):
        dt = jax.nn.softplus(dt_ref[pl.ds(r0, CHUNK), :] + bias_ref[...])
        la = dt * neg_a
        a_inc = _cumsum_rows(la)
        total = a_inc[CHUNK - 1:CHUNK, :]
        return dt, la, a_inc, total

    def decay_row(total, e_ref):
        return _expand_heads(jnp.broadcast_to(jnp.exp(total), (8, LANE)), e_ref)[0:1, :]

    def fwd_chunk(x_ref, dt_ref, c, n_chunks, base):
        if isinstance(c, int):
            r0 = c * CHUNK
            t0 = base + r0
        else:
            r0 = pl.multiple_of(c * CHUNK, CHUNK)
            t0 = pl.multiple_of(base + r0, CHUNK)
        tok = conv_silu(x_ref, c, n_chunks)
        tok_b = tok.astype(BF16)
        tok_s[pl.ds(t0, CHUNK), :] = tok_b
        xs = tok_b[:, :SSD_W]
        dt, la, a_inc, total = decays(dt_ref, r0)
        a_exc = a_inc - la
        log2e = math.log2(math.e)
        ldt = jnp.log2(dt)
        col_term = a_inc * log2e
        col_term_b = a_exc * log2e
        row_t = (jnp.where(lax.broadcasted_iota(jnp.int32, (CHUNK, LANE), 1) < SSD_HEADS,
                           ldt - col_term, ldt + col_term_b)).T
        dt_t = dt.T
        groups = range(SSD_GROUPS)
        bs = [tok_b[:, SSD_W + g * SSD_STATE:SSD_W + (g + 1) * SSD_STATE] for g in groups]
        cs = [tok_b[:, SSD_W + SSD_BC + g * SSD_STATE:SSD_W + SSD_BC + (g + 1) * SSD_STATE] for g in groups]
        gsl = [slice(g * GROUP_W, (g + 1) * GROUP_W) for g in groups]
        gmat = [_dot_nt(cs[g], bs[g]) for g in groups]
        inter = [_dot(cs[g], sf_s[:, gsl[g]].astype(BF16)) for g in groups]
        inter_scale = _expand_heads(jnp.exp(a_inc), ef_ref)
        state_scale = _expand_heads(jnp.exp(total - a_inc) * dt, ef_ref)
        dec = decay_row(total, ef_ref)
        xw = (xs.astype(F32) * state_scale).astype(BF16)
        dstate = [_dot_tn(bs[g], xw[:, gsl[g]]) for g in groups]
        ms = []
        for hd in range(SSD_HEADS):
            hb = SSD_HEADS + hd
            diag = jnp.log2(dt_t[hd:hd + 1, :] + dt_t[hb:hb + 1, :])
            expo = jnp.where(lower, col_term[:, hd:hd + 1] + row_t[hd:hd + 1, :],
                             jnp.where(upper, row_t[hb:hb + 1, :] - col_term_b[:, hb:hb + 1], diag))
            ms.append((gmat[hd // HEADS_PER_GROUP] * jnp.exp2(expo)).astype(BF16))
        ys = []
        for pair in range(SSD_HEADS // 2):
            col = 2 * pair * SSD_HEAD_DIM
            xpair = xs[:, col:col + 2 * SSD_HEAD_DIM]
            lane = lax.broadcasted_iota(jnp.int32, xpair.shape, 1)
            zero = jnp.zeros_like(xpair)
            rhs = jnp.concatenate([jnp.where(lane < SSD_HEAD_DIM, xpair, zero),
                                   jnp.where(lane < SSD_HEAD_DIM, zero, xpair)], axis=0)
            ys.append(_dot(jnp.concatenate(ms[2 * pair:2 * pair + 2], axis=1), rhs))
        y_s[pl.ds(t0, CHUNK), :] = jnp.concatenate(ys, axis=1) + inter_scale * jnp.concatenate(inter, axis=1)
        for g in groups:
            sf_s[:, gsl[g]] = sf_s[:, gsl[g]] * dec[:, gsl[g]] + dstate[g]

    def bwd_chunk(dt_ref, z_ref, o_ref, c, base):
        if isinstance(c, int):
            r0 = c * CHUNK
            t0 = base + r0
        else:
            r0 = pl.multiple_of(c * CHUNK, CHUNK)
            t0 = pl.multiple_of(base + r0, CHUNK)
        tok_b = tok_s[pl.ds(t0, CHUNK), :]
        xs = tok_b[:, :SSD_W].astype(F32)
        dt, la, a_inc, total = decays(dt_ref, r0)
        a_exc = a_inc - la
        groups = range(SSD_GROUPS)
        gsl = [slice(g * GROUP_W, (g + 1) * GROUP_W) for g in groups]
        bs = [tok_b[:, SSD_W + g * SSD_STATE:SSD_W + (g + 1) * SSD_STATE] for g in groups]
        xw = (xs * _expand_heads(jnp.exp(a_exc) * dt, eb_ref)).astype(BF16)
        dstate = [_dot_tn(bs[g], xw[:, gsl[g]]) for g in groups]
        dec = decay_row(total, eb_ref)
        if o_ref is not None:
            cs = [tok_b[:, SSD_W + SSD_BC + g * SSD_STATE:SSD_W + SSD_BC + (g + 1) * SSD_STATE] for g in groups]
            inter = [_dot(cs[g], sb_s[:, gsl[g]].astype(BF16)) for g in groups]
            y = (y_s[pl.ds(t0, CHUNK), :]
                 + _expand_heads(jnp.exp(total - a_exc), eb_ref) * jnp.concatenate(inter, axis=1))
            y = (y + skip_ref[...] * xs) * _silu(z_ref[pl.ds(r0, CHUNK), :].astype(F32))
            o_ref[pl.ds(r0, CHUNK), :] = (_rms(y) * ng_ref[...]).astype(BF16)
        for g in groups:
            sb_s[:, gsl[g]] = sb_s[:, gsl[g]] * dec[:, gsl[g]] + dstate[g]

    sf_s[...] = jnp.zeros_like(sf_s)
    sb_s[...] = jnp.zeros_like(sb_s)

    for c in range(n_ctx_chunks):
        fwd_chunk(xc_ref, dtc_ref, c, n_ctx_chunks, 0)

    def lat_fwd(c, carry):
        fwd_chunk(xl_ref, dtl_ref, c, n_lat_chunks, lc)
        return carry

    lax.fori_loop(0, n_lat_chunks, lat_fwd, 0, unroll=SCAN_UNROLL)

    for c in reversed(range(n_ctx_chunks)):
        bwd_chunk(dtc_ref, zc_ref, oc_ref, c, 0)

    def lat_bwd(t, carry):
        bwd_chunk(dtl_ref, zl_ref, ol_ref, n_lat_chunks - 1 - t, lc)
        return carry

    lax.fori_loop(0, n_lat_chunks, lat_bwd, 0, unroll=SCAN_UNROLL)


def _conv_shift_matrices():
    taps = [k for k in range(SSD_CONV) if k != SSD_CONV // 2]
    mats = np.zeros((len(taps), CHUNK, CONV_EXT), np.float32)
    for n, k in enumerate(taps):
        for i in range(CHUNK):
            src = i + k - SSD_CONV // 2
            if src < 0:
                col = CHUNK + CONV_HALO + src
            elif src >= CHUNK:
                col = CHUNK + CONV_HALO + (src - CHUNK)
            else:
                col = src
            mats[n, i, col] = 1.0
    return mats


def _head_lane_vec(f, b):
    v = jnp.zeros((1, LANE), F32)
    v = v.at[0, :SSD_HEADS].set(f.astype(F32))
    return v.at[0, SSD_HEADS:2 * SSD_HEADS].set(b.astype(F32))


def _ssd(tc, tl, p, batch, ctx_out):
    lc = tc["xbc"].shape[0] // batch
    ll = tl["xbc"].shape[0] // batch
    rows = lambda n, w: pl.BlockSpec((n, w), lambda b: (b, 0))
    vec = lambda w: pl.BlockSpec((1, w), lambda b: (0, 0))
    whole = lambda a: pl.BlockSpec(a.shape, lambda b: (0,) * a.ndim)
    head_of_col = np.arange(SSD_W) // SSD_HEAD_DIM
    ef = jnp.asarray(np.arange(LANE)[:, None] == head_of_col[None, :], BF16)
    eb = jnp.asarray(np.arange(LANE)[:, None] == head_of_col[None, :] + SSD_HEADS, BF16)
    shift = jnp.asarray(_conv_shift_matrices(), BF16)
    conv_w = p["conv_w"].astype(F32)
    conv_b = p["conv_b"].astype(F32).reshape(1, CONV_CH)
    bias = _head_lane_vec(p["dt_bias_f"], p["dt_bias_b"])
    alog = _head_lane_vec(p["a_log_f"], p["a_log_b"])
    skip = jnp.repeat(p["d_skip"].astype(F32), SSD_HEAD_DIM).reshape(1, SSD_W)
    norm_g = p["ssd_norm_g"].astype(F32).reshape(1, SSD_W)
    out_specs = [rows(ll, SSD_W)]
    out_shape = [jax.ShapeDtypeStruct((batch * ll, SSD_W), BF16)]
    if ctx_out:
        out_specs = [rows(lc, SSD_W)] + out_specs
        out_shape = [jax.ShapeDtypeStruct((batch * lc, SSD_W), BF16)] + out_shape
    outs = pl.pallas_call(
        functools.partial(_ssd_kernel, lc // CHUNK, ll // CHUNK, ctx_out),
        grid=(batch,),
        in_specs=[rows(lc, CONV_CH), rows(lc, LANE), rows(lc, SSD_W),
                  rows(ll, CONV_CH), rows(ll, LANE), rows(ll, SSD_W),
                  whole(conv_w), vec(CONV_CH), vec(LANE), vec(LANE), vec(SSD_W), vec(SSD_W),
                  whole(ef), whole(eb), whole(shift)],
        out_specs=out_specs,
        out_shape=out_shape,
        scratch_shapes=[
            pltpu.VMEM((lc + ll, CONV_CH), BF16), pltpu.VMEM((lc + ll, SSD_W), F32),
            pltpu.VMEM((SSD_STATE, SSD_W), F32), pltpu.VMEM((SSD_STATE, SSD_W), F32),
        ],
        compiler_params=_cparams(("parallel",)),
        name="ssd_scan",
    )(tc["xbc"], tc["dt"], tc["z"], tl["xbc"], tl["dt"], tl["z"],
      conv_w, conv_b, bias, alog, skip, norm_g, ef, eb, shift)
    return (outs[0], outs[1]) if ctx_out else (None, outs[0])


def _out_proj_kernel(n_sub, att_ref, ret_ref, ssd_ref, w_ref, x_ref, g1_ref, pmg_ref, x1_ref):
    ts = x_ref.shape[0] // n_sub
    for c in range(n_sub):
        rows = slice(c * ts, (c + 1) * ts)
        m = (_dot(att_ref[rows, :], w_ref[0:ATT_W, :])
             + _dot(ret_ref[rows, :], w_ref[ATT_W:ATT_W + RET_W, :])
             + _dot(ssd_ref[rows, :], w_ref[ATT_W + RET_W:, :]))
        x1_ref[rows, :] = x_ref[rows, :] + g1_ref[...] * (_rms(m) * pmg_ref[...])


def _out_proj(att, ret, ssd, w_out_b, x2d, mod4, group_of_tile, post_mix_g, tm):
    rows, d = x2d.shape
    row_spec = lambda w: pl.BlockSpec((tm, w), lambda i: (i, 0))
    mod_spec = lambda k: pl.BlockSpec((None, None, 1, d), lambda i: (group_of_tile(i), k, 0, 0))
    vec_spec = pl.BlockSpec((1, d), lambda i: (0, 0))
    n_sub = max(tm // OUT_PROJ_SUB_ROWS, 1)
    return pl.pallas_call(
        functools.partial(_out_proj_kernel, n_sub),
        grid=(rows // tm,),
        in_specs=[row_spec(ATT_W), row_spec(RET_W), row_spec(SSD_W),
                  pl.BlockSpec(w_out_b.shape, lambda i: (0, 0), pipeline_mode=pl.Buffered(1)),
                  row_spec(d), mod_spec(2), vec_spec],
        out_specs=row_spec(d),
        out_shape=jax.ShapeDtypeStruct((rows, d), F32),
        compiler_params=_cparams(("parallel",)),
        name="out_proj",
    )(att, ret, ssd, w_out_b, x2d, mod4, post_mix_g)


def _ffn_kernel(n_cast, *refs):
    x1_ref, sc_ref, sh_ref, pfg_ref, wg_ref, wu_ref, wd_ref, g2_ref, pg_ref = refs[:9]
    cast_src = refs[9:9 + n_cast]
    o_ref = refs[9 + n_cast]
    cast_dst = refs[10 + n_cast:10 + 2 * n_cast]
    hf_s = refs[10 + 2 * n_cast]
    for src, dst in zip(cast_src, cast_dst):
        dst[...] = src[...].astype(BF16)

    j = pl.program_id(1)
    last_j = pl.num_programs(1) - 1
    tm, n_out = o_ref.shape
    ts = _pick_tile(tm, FFN_SUB_ROWS)
    tr = _pick_tile(ts, FFN_EPILOGUE_ROWS)
    tn = _pick_tile(n_out, FFN_DOWN_COLS)

    def hidden(rows):
        hf = hf_s[rows, :]
        return (_silu(_dot(hf, wg_ref[...])) * _dot(hf, wu_ref[...])).astype(BF16)

    @pl.when(j == 0)
    def _():
        gain = pfg_ref[...] * (1.0 + sc_ref[...])
        for c in range(tm // ts):
            for r in range(ts // tr):
                rows = slice(c * ts + r * tr, c * ts + (r + 1) * tr)
                hf_s[rows, :] = (_rms(x1_ref[rows, :]) * gain + sh_ref[...]).astype(BF16)
            rows = slice(c * ts, (c + 1) * ts)
            o_ref[rows, :] = _dot(hidden(rows), wd_ref[...])

    @pl.when(jnp.logical_and(j > 0, j < last_j))
    def _():
        a = hidden(slice(0, tm))
        for n in range(n_out // tn):
            o_ref[:, n * tn:(n + 1) * tn] += _dot(a, wd_ref[:, n * tn:(n + 1) * tn])

    @pl.when(j == last_j)
    def _():
        gain = g2_ref[...] * pg_ref[...]
        for c in range(tm // ts):
            rows = slice(c * ts, (c + 1) * ts)
            f = o_ref[rows, :] + _dot(hidden(rows), wd_ref[...])
            o_ref[rows, :] = x1_ref[rows, :] + _rms(f) * gain


def _ffn_cast_plan(n_i, tf, layer, w_gate, w_up, w_down):
    _, d, dff = w_gate.shape
    if d % n_i or (d // n_i) % LANE:
        return None
    rd = d // n_i
    return [
        (w_gate, pl.BlockSpec((None, rd, tf), lambda i, j: (layer, i, j)), pl.BlockSpec((rd, tf), lambda i, j: (i, j)),
         (d, dff)),
        (w_up, pl.BlockSpec((None, rd, tf), lambda i, j: (layer, i, j)), pl.BlockSpec((rd, tf), lambda i, j: (i, j)),
         (d, dff)),
        (w_down, pl.BlockSpec((None, tf, rd), lambda i, j: (layer, j, i)), pl.BlockSpec((tf, rd), lambda i, j: (j, i)),
         (dff, d)),
    ]


def _ffn(x1, wg, wu, wd, mod4, group_of_tile, pre_ffn_g, post_ffn_g, tm, tf, cast_plan=None):
    rows, d = x1.shape
    dff = wg.shape[1]
    assert dff // tf >= 2
    mod_spec = lambda k: pl.BlockSpec((None, None, 1, d), lambda i, j: (group_of_tile(i), k, 0, 0))
    vec_spec = pl.BlockSpec((1, d), lambda i, j: (0, 0))
    cast_plan = cast_plan or []
    outs = pl.pallas_call(
        functools.partial(_ffn_kernel, len(cast_plan)),
        grid=(rows // tm, dff // tf),
        in_specs=[
            pl.BlockSpec((tm, d), lambda i, j: (i, 0)),
            mod_spec(4), mod_spec(3), vec_spec,
            pl.BlockSpec((d, tf), lambda i, j: (0, j)),
            pl.BlockSpec((d, tf), lambda i, j: (0, j)),
            pl.BlockSpec((tf, d), lambda i, j: (j, 0)),
            mod_spec(5), vec_spec,
        ] + [c[1] for c in cast_plan],
        out_specs=[pl.BlockSpec((tm, d), lambda i, j: (i, 0))] + [c[2] for c in cast_plan],
        out_shape=[jax.ShapeDtypeStruct((rows, d), F32)] + [jax.ShapeDtypeStruct(c[3], BF16) for c in cast_plan],
        scratch_shapes=[pltpu.VMEM((tm, d), BF16)],
        compiler_params=_cparams(("arbitrary", "arbitrary")),
        name="swiglu_ffn_cast" if cast_plan else "swiglu_ffn",
    )(x1, mod4, mod4, pre_ffn_g, wg, wu, wd, mod4, post_ffn_g, *[c[0] for c in cast_plan])
    return outs[0], tuple(outs[1:])


def _cast_kernel(w_ref, o_ref):
    o_ref[...] = w_ref[...].astype(BF16)


def _cast_layer(w_stack, layer, n_rows=None):
    _, rows, cols = w_stack.shape
    n_rows = rows if n_rows is None else n_rows
    tr = _pick_tile(n_rows, 256)
    return pl.pallas_call(
        _cast_kernel,
        grid=(n_rows // tr,),
        in_specs=[pl.BlockSpec((None, tr, cols), lambda i: (layer, i, 0))],
        out_specs=pl.BlockSpec((tr, cols), lambda i: (i, 0)),
        out_shape=jax.ShapeDtypeStruct((n_rows, cols), BF16),
        compiler_params=_cparams(("parallel",)),
        name="cast_weight",
    )(w_stack)


def _rope_tables(seq_len):
    rows = seq_len // GRID_W
    row = jnp.repeat(jnp.arange(rows, dtype=F32), GRID_W)
    col = jnp.tile(jnp.arange(GRID_W, dtype=F32), rows)
    n_freq = HEAD_DIM // 4
    inv = ROPE_THETA ** (-jnp.arange(n_freq, dtype=F32) / n_freq)
    ang = jnp.concatenate([row[:, None] * inv, col[:, None] * inv], axis=-1)
    cos, sin = jnp.cos(ang), jnp.sin(ang)
    cos_t = jnp.repeat(cos, 2, axis=-1)
    sin_t = jnp.stack([-sin, sin], axis=-1).reshape(seq_len, HEAD_DIM)
    return cos_t, sin_t


def _pick_tile(n, target):
    t = min(n, target)
    while n % t:
        t //= 2
    return t


def _cast_tail_kernel(n_valid, w_ref, o_ref):
    row = lax.broadcasted_iota(jnp.int32, w_ref.shape, 0)
    o_ref[...] = jnp.where(row < n_valid, w_ref[...], 0.0).astype(BF16)


def _cast_layer_tail(w_stack, layer, row0):
    _, rows, cols = w_stack.shape
    assert row0 % LANE == 0 and 0 < rows - row0 <= LANE
    return pl.pallas_call(
        functools.partial(_cast_tail_kernel, rows - row0),
        grid=(1,),
        in_specs=[pl.BlockSpec((None, LANE, cols), lambda i: (layer, row0 // LANE, 0))],
        out_specs=pl.BlockSpec((LANE, cols), lambda i: (0, 0)),
        out_shape=jax.ShapeDtypeStruct((LANE, cols), BF16),
        compiler_params=_cparams(("arbitrary",)),
        name="cast_weight_tail",
    )(w_stack)


def _in_proj_weights(l, w_in):
    w_in_t = jnp.swapaxes(w_in, 1, 2)
    return _cast_layer(w_in_t, l, OFF_DT), _cast_layer_tail(w_in_t, l, OFF_DT)


def kernel(x, c, ctx, c_ctx, w_mod, b_mod, pre_mix_g, post_mix_g, pre_ffn_g, post_ffn_g, w_in, q_norm_g, k_norm_g, ret_decay_f, ret_decay_b, conv_w, conv_b, dt_bias_f, dt_bias_b, a_log_f, a_log_b, d_skip, ssd_norm_g, w_out, w_gate, w_up, w_down):
    batch, seq, d = x.shape
    lc = ctx.shape[1]
    depth = w_mod.shape[0]
    assert batch < N_MOD_ROWS and seq % CHUNK == 0 and lc % CHUNK == 0 and seq % GRID_W == 0

    c16 = jnp.zeros((N_MOD_ROWS, d), F32).at[:batch].set(c.astype(F32)).at[batch].set(c_ctx.astype(F32))
    mod_all = _modulation(c16, w_mod.astype(F32), b_mod.astype(F32))
    rope_tabs = _rope_tables(seq)

    tm_lat = _pick_tile(seq, 512)
    tm_ctx = _pick_tile(batch * lc, 512)
    tm_ffn_lat = _pick_tile(seq, 1024)
    tm_ffn_ctx = _pick_tile(batch * lc, 1024)
    lat_group = lambda tm: (lambda i: i // (seq // tm))
    ctx_group = lambda tm: (lambda i: batch)
    tq_lat = _pick_tile(seq, 1024)
    tq_ctx = _pick_tile(lc, 256)
    sub_lat = max(tq_lat // 128, 1)
    sub_ctx = 2 if tq_ctx % 32 == 0 else 1
    tf = _pick_tile(w_gate.shape[2], 512)

    xl = x.reshape(batch * seq, d).astype(F32)
    xc = ctx.reshape(batch * lc, d).astype(F32)
    vec = lambda v: v.astype(F32).reshape(1, d)

    ffn_w = tuple(_cast_layer(w, 0) for w in (w_gate, w_up, w_down))
    for l in range(depth):
        last = l == depth - 1
        w_in_b, w_dt_b = _in_proj_weights(l, w_in)
        w_out_b = _cast_layer(w_out, l)
        wg, wu, wd = ffn_w
        qg = q_norm_g[l].astype(F32).reshape(1, HEAD_DIM)
        kg = k_norm_g[l].astype(F32).reshape(1, HEAD_DIM)
        mod4 = mod_all[l].reshape(N_MOD_ROWS, 6, 1, d)
        p = dict(conv_w=conv_w[l], conv_b=conv_b[l], dt_bias_f=dt_bias_f[l], dt_bias_b=dt_bias_b[l],
                 a_log_f=a_log_f[l], a_log_b=a_log_b[l], d_skip=d_skip[l], ssd_norm_g=ssd_norm_g[l])

        tl = _in_proj(xl, mod4, lat_group(tm_lat), vec(pre_mix_g[l]), w_in_b, w_dt_b, qg, kg, rope_tabs, tm_lat)
        tc = _in_proj(xc, mod4, ctx_group(tm_ctx), vec(pre_mix_g[l]), w_in_b, w_dt_b, qg, kg, None, tm_ctx)

        att_l = _attention(tl["aq"], [(tc["ak"], tc["av"]), (tl["ak"], tl["av"])], batch, tq_lat, sub_lat)
        ret_c, ret_l = _retention(tc, tl, ret_decay_f[l], ret_decay_b[l], batch, not last)
        ssd_c, ssd_l = _ssd(tc, tl, p, batch, not last)

        x1 = _out_proj(att_l, ret_l, ssd_l, w_out_b, xl, mod4, lat_group(tm_ffn_lat), vec(post_mix_g[l]), tm_ffn_lat)
        plan = None if last else _ffn_cast_plan(batch * seq // tm_ffn_lat, tf, l + 1, w_gate, w_up, w_down)
        xl, next_w = _ffn(x1, wg, wu, wd, mod4, lat_group(tm_ffn_lat), vec(pre_ffn_g[l]), vec(post_ffn_g[l]),
                          tm_ffn_lat, tf, plan)

        if not last:
            att_c = _attention(tc["aq"], [(tc["ak"], tc["av"])], batch, tq_ctx, sub_ctx)
            x1c = _out_proj(att_c, ret_c, ssd_c, w_out_b, xc, mod4, ctx_group(tm_ffn_ctx), vec(post_mix_g[l]),
                            tm_ffn_ctx)
            xc, _ = _ffn(x1c, wg, wu, wd, mod4, ctx_group(tm_ffn_ctx), vec(pre_ffn_g[l]), vec(post_ffn_g[l]),
                         tm_ffn_ctx, tf)
            ffn_w = next_w if plan else tuple(_cast_layer(w, l + 1) for w in (w_gate, w_up, w_down))

    return xl.reshape(batch, seq, d).astype(x.dtype)
```

```python
import functools
import math

import numpy as np
import jax
import jax.numpy as jnp
from jax import lax
from jax.experimental import pallas as pl
from jax.experimental.pallas import tpu as pltpu

F32 = jnp.float32
BF16 = jnp.bfloat16

GRID_W = 64
HEAD_DIM = 128
ATT_HEADS = 6
ATT_KV_HEADS = 2
ATT_REP = ATT_HEADS // ATT_KV_HEADS
RET_HEADS = 4
RET_DK = 128
RET_DV = 128
SSD_HEADS = 12
SSD_HEAD_DIM = 64
SSD_GROUPS = 2
SSD_STATE = 128
SSD_CONV = 5
ATT_W = ATT_HEADS * HEAD_DIM
KV_W = ATT_KV_HEADS * HEAD_DIM
RET_QK_W = RET_HEADS * RET_DK
RET_W = RET_HEADS * RET_DV
SSD_W = SSD_HEADS * SSD_HEAD_DIM
SSD_BC = SSD_GROUPS * SSD_STATE
CONV_CH = SSD_W + 2 * SSD_BC
GROUP_W = SSD_W // SSD_GROUPS
HEADS_PER_GROUP = SSD_HEADS // SSD_GROUPS
CHUNK = 128
ROPE_THETA = 10000.0
EPS = 1e-6
LANE = 128
FFN_DOWN_COLS = 512
FFN_EPILOGUE_ROWS = 128
OUT_PROJ_SUB_ROWS = 256
FFN_SUB_ROWS = 256
SCAN_UNROLL = 4
CONV_HALO = 16
CONV_EXT = 256

OFF_AQ = 0
OFF_AK = OFF_AQ + ATT_W
OFF_AV = OFF_AK + KV_W
OFF_RQ = OFF_AV + KV_W
OFF_RK = OFF_RQ + RET_QK_W
OFF_RV = OFF_RK + RET_QK_W
OFF_RG = OFF_RV + RET_W
OFF_Z = OFF_RG + RET_W
OFF_XBC = OFF_Z + SSD_W
OFF_DT = OFF_XBC + CONV_CH
D_IN = OFF_DT + 2 * SSD_HEADS
D_IN_PAD = OFF_DT + LANE

VMEM_LIMIT = 60 * 1024 * 1024
N_MOD_ROWS = 16


def _cparams(sem):
    return pltpu.CompilerParams(dimension_semantics=sem, vmem_limit_bytes=VMEM_LIMIT)


def _silu(v):
    return v * jax.nn.sigmoid(v)


def _rms(v):
    return v * lax.rsqrt(jnp.mean(v * v, axis=-1, keepdims=True) + EPS)


def _dot(a, b):
    return jnp.dot(a, b, preferred_element_type=F32)


def _dot_nt(a, b):
    return lax.dot_general(a, b, (((1,), (1,)), ((), ())), preferred_element_type=F32)


def _dot_tn(a, b):
    return lax.dot_general(a, b, (((0,), (0,)), ((), ())), preferred_element_type=F32)


def _mod_kernel(c_ref, w_ref, b_ref, o_ref):
    a = _silu(c_ref[...]).astype(BF16)
    o_ref[...] = _dot(a, w_ref[...].astype(BF16)) + b_ref[...]


def _modulation(c16, w_mod, b_mod):
    depth, d, n = w_mod.shape
    tn = _pick_tile(n, 1024)
    return pl.pallas_call(
        _mod_kernel,
        grid=(depth, n // tn),
        in_specs=[
            pl.BlockSpec((N_MOD_ROWS, d), lambda l, j: (0, 0)),
            pl.BlockSpec((None, d, tn), lambda l, j: (l, 0, j)),
            pl.BlockSpec((None, 1, tn), lambda l, j: (l, 0, j)),
        ],
        out_specs=pl.BlockSpec((None, N_MOD_ROWS, tn), lambda l, j: (l, 0, j)),
        out_shape=jax.ShapeDtypeStruct((depth, N_MOD_ROWS, n), F32),
        compiler_params=_cparams(("parallel", "parallel")),
        name="adaln_modulation",
    )(c16, w_mod, b_mod.reshape(depth, 1, n))


def _in_proj_kernel(use_rope, *refs):
    if use_rope:
        (x_ref, sc_ref, sh_ref, g_ref, w_ref, wdt_ref, qg_ref, kg_ref, cos_ref, sin_ref,
         aq_ref, ak_ref, av_ref, rq_ref, rk_ref, rv_ref, rg_ref, z_ref, xbc_ref, dt_ref) = refs
        cos = cos_ref[...]
        sin = sin_ref[...]
        even_lane = (lax.broadcasted_iota(jnp.int32, cos.shape, 1) & 1) == 0
    else:
        (x_ref, sc_ref, sh_ref, g_ref, w_ref, wdt_ref, qg_ref, kg_ref,
         aq_ref, ak_ref, av_ref, rq_ref, rk_ref, rv_ref, rg_ref, z_ref, xbc_ref, dt_ref) = refs

    x = x_ref[...]
    h = (_rms(x) * g_ref[...] * (1.0 + sc_ref[...]) + sh_ref[...]).astype(BF16)

    def proj(lo, width):
        return _dot_nt(h, w_ref[lo:lo + width, :])

    def rope(t):
        if not use_rope:
            return t
        partner = jnp.where(even_lane, pltpu.roll(t, HEAD_DIM - 1, 1), pltpu.roll(t, 1, 1))
        return t * cos + partner * sin

    att_scale = HEAD_DIM ** -0.5 * math.log2(math.e)
    ret_scale = RET_DK ** -0.5

    acc = proj(OFF_AQ, ATT_W)
    for hd in range(ATT_HEADS):
        t = acc[:, hd * HEAD_DIM:(hd + 1) * HEAD_DIM]
        t = rope(_rms(t) * qg_ref[...]) * att_scale
        aq_ref[:, hd * HEAD_DIM:(hd + 1) * HEAD_DIM] = t.astype(BF16)

    acc = proj(OFF_AK, KV_W)
    for hd in range(ATT_KV_HEADS):
        t = acc[:, hd * HEAD_DIM:(hd + 1) * HEAD_DIM]
        t = rope(_rms(t) * kg_ref[...])
        ak_ref[:, hd * HEAD_DIM:(hd + 1) * HEAD_DIM] = t.astype(BF16)

    av_ref[...] = proj(OFF_AV, KV_W).astype(BF16)

    acc = proj(OFF_RQ, RET_QK_W)
    for hd in range(RET_HEADS):
        t = rope(acc[:, hd * RET_DK:(hd + 1) * RET_DK])
        rq_ref[:, hd * RET_DK:(hd + 1) * RET_DK] = t.astype(BF16)

    acc = proj(OFF_RK, RET_QK_W)
    for hd in range(RET_HEADS):
        t = rope(acc[:, hd * RET_DK:(hd + 1) * RET_DK] * ret_scale)
        rk_ref[:, hd * RET_DK:(hd + 1) * RET_DK] = t.astype(BF16)

    rv_ref[...] = proj(OFF_RV, RET_W).astype(BF16)
    rg_ref[...] = proj(OFF_RG, RET_W).astype(BF16)
    z_ref[...] = proj(OFF_Z, SSD_W).astype(BF16)
    xbc_ref[...] = proj(OFF_XBC, CONV_CH).astype(BF16)
    dt_ref[...] = _dot_nt(h, wdt_ref[...])


def _in_proj(x2d, mod4, group_of_tile, pre_g, w_in_b, w_dt_b, qg, kg, rope_tabs, tm):
    rows, d = x2d.shape
    nt = rows // tm
    use_rope = rope_tabs is not None
    row_spec = lambda w: pl.BlockSpec((tm, w), lambda i: (i, 0))
    mod_spec = lambda k: pl.BlockSpec((None, None, 1, d), lambda i: (group_of_tile(i), k, 0, 0))
    vec_spec = lambda w: pl.BlockSpec((1, w), lambda i: (0, 0))
    in_specs = [
        row_spec(d), mod_spec(1), mod_spec(0), vec_spec(d),
        pl.BlockSpec((OFF_DT, d), lambda i: (0, 0), pipeline_mode=pl.Buffered(1)),
        pl.BlockSpec((LANE, d), lambda i: (0, 0)),
        vec_spec(HEAD_DIM), vec_spec(HEAD_DIM),
    ]
    args = [x2d, mod4, mod4, pre_g, w_in_b, w_dt_b, qg, kg]
    if use_rope:
        cos_t, sin_t = rope_tabs
        tiles_per_seq = cos_t.shape[0] // tm
        tab_spec = pl.BlockSpec((tm, HEAD_DIM), lambda i: (i % tiles_per_seq, 0))
        in_specs += [tab_spec, tab_spec]
        args += [cos_t, sin_t]
    widths = (ATT_W, KV_W, KV_W, RET_QK_W, RET_QK_W, RET_W, RET_W, SSD_W, CONV_CH)
    out_specs = [row_spec(w) for w in widths] + [row_spec(LANE)]
    out_shape = [jax.ShapeDtypeStruct((rows, w), BF16) for w in widths]
    out_shape.append(jax.ShapeDtypeStruct((rows, LANE), F32))
    outs = pl.pallas_call(
        functools.partial(_in_proj_kernel, use_rope),
        grid=(nt,),
        in_specs=in_specs,
        out_specs=out_specs,
        out_shape=out_shape,
        compiler_params=_cparams(("parallel",)),
        name="in_proj_rope" if use_rope else "in_proj",
    )(*args)
    names = ("aq", "ak", "av", "rq", "rk", "rv", "rg", "z", "xbc", "dt")
    return dict(zip(names, outs))


def _attn_kernel(n_kv, tq, n_sub, n_cast, *refs):
    q_ref = refs[0]
    kv_refs = refs[1:1 + 2 * n_kv]
    cast_src = refs[1 + 2 * n_kv:1 + 2 * n_kv + n_cast]
    o_ref = refs[1 + 2 * n_kv + n_cast]
    cast_dst = refs[2 + 2 * n_kv + n_cast:]
    for src, dst in zip(cast_src, cast_dst):
        dst[...] = src[...].astype(BF16)
    ts = tq // n_sub
    all_scores = []
    for c in range(n_sub):
        q = q_ref[c * ts:(c + 1) * ts, :]
        qs = jnp.concatenate([q[:, r * HEAD_DIM:(r + 1) * HEAD_DIM] for r in range(ATT_REP)], axis=0)
        all_scores.append([_dot_nt(qs, kv_refs[2 * s][...]) for s in range(n_kv)])
    v_ones = []
    for s in range(n_kv):
        v = kv_refs[2 * s + 1][...]
        v_ones.append(jnp.concatenate([v, jnp.ones_like(v)], axis=1))
    for c in range(n_sub):
        scores = all_scores[c]
        m = functools.reduce(jnp.maximum, [jnp.max(s, axis=-1, keepdims=True) for s in scores])
        acc = functools.reduce(jnp.add, [_dot(jnp.exp2(s - m).astype(BF16), v_ones[i]) for i, s in enumerate(scores)])
        out = acc[:, :HEAD_DIM] / acc[:, HEAD_DIM:]
        for r in range(ATT_REP):
            o_ref[c * ts:(c + 1) * ts, r * HEAD_DIM:(r + 1) * HEAD_DIM] = out[r * ts:(r + 1) * ts].astype(BF16)


def _row_cast_plan(weights, layer, n_steps, step_of):
    plan = []
    for w in weights:
        _, rows, cols = w.shape
        if rows % n_steps or (rows // n_steps) % 16:
            return None
        rb = rows // n_steps
        plan.append((w, pl.BlockSpec((None, rb, cols), lambda *g: (layer, step_of(*g), 0)),
                     pl.BlockSpec((rb, cols), lambda *g: (step_of(*g), 0)), (rows, cols)))
    return plan


def _attention(q2d, kv_list, batch, tq, n_sub, cast_weights=None, cast_layer=0):
    rows = q2d.shape[0]
    lq = rows // batch
    nq = lq // tq
    gw = ATT_REP * HEAD_DIM
    in_specs = [pl.BlockSpec((tq, gw), lambda b, g, i: (b * nq + i, g))]
    args = [q2d]
    for k2d, v2d in kv_list:
        lk = k2d.shape[0] // batch
        spec = pl.BlockSpec((lk, HEAD_DIM), lambda b, g, i: (b, g))
        in_specs += [spec, spec]
        args += [k2d, v2d]
    n_steps = batch * ATT_KV_HEADS * nq
    plan = (_row_cast_plan(cast_weights, cast_layer, n_steps, lambda b, g, i: (b * ATT_KV_HEADS + g) * nq + i)
            if cast_weights else None) or []
    outs = pl.pallas_call(
        functools.partial(_attn_kernel, len(kv_list), tq, n_sub, len(plan)),
        grid=(batch, ATT_KV_HEADS, nq),
        in_specs=in_specs + [c[1] for c in plan],
        out_specs=[pl.BlockSpec((tq, gw), lambda b, g, i: (b * nq + i, g))] + [c[2] for c in plan],
        out_shape=[jax.ShapeDtypeStruct((rows, ATT_W), BF16)] + [jax.ShapeDtypeStruct(c[3], BF16) for c in plan],
        compiler_params=_cparams(("arbitrary", "arbitrary", "arbitrary")),
        name="attention_cast" if plan else "attention",
    )(*args, *[c[0] for c in plan])
    return outs[0], tuple(outs[1:])


_RT_INTRA, _RT_INTER_F, _RT_STATE_F, _RT_DEC_F, _RT_INTER_B, _RT_STATE_B, _RT_DEC_B = range(7)


def _ret_kernel(n_ctx_chunks, n_lat_chunks, ctx_out, *refs):
    (qc_ref, kc_ref, vc_ref, gc_ref, ql_ref, kl_ref, vl_ref, gl_ref, df_ref, db_ref) = refs[:10]
    if ctx_out:
        oc_ref, ol_ref, yc_s, yl_s, sf_s, sb_s, tab_s, ds_s = refs[10:]
    else:
        ol_ref, yc_s, yl_s, sf_s, sb_s, tab_s, ds_s = refs[10:]
        oc_ref = None

    ii = lax.broadcasted_iota(jnp.int32, (CHUNK, CHUNK), 0)
    jj = lax.broadcasted_iota(jnp.int32, (CHUNK, CHUNK), 1)
    dist = (ii - jj).astype(F32)
    rowi = ii.astype(F32)
    ones = jnp.ones((CHUNK, CHUNK), F32)
    for hd in range(RET_HEADS):
        lam_f = jnp.log1p(-jnp.exp2(df_ref[hd]))
        lam_b = jnp.log1p(-jnp.exp2(db_ref[hd]))
        tab_s[hd, _RT_INTRA] = (jnp.where(jj <= ii, jnp.exp(dist * lam_f), 0.0)
                                + jnp.where(jj >= ii, jnp.exp(-dist * lam_b), 0.0))
        tab_s[hd, _RT_INTER_F] = jnp.exp((rowi + 1.0) * lam_f)
        tab_s[hd, _RT_STATE_F] = jnp.exp((CHUNK - 1.0 - rowi) * lam_f)
        tab_s[hd, _RT_DEC_F] = ones * jnp.exp(CHUNK * lam_f)
        tab_s[hd, _RT_INTER_B] = jnp.exp((CHUNK - rowi) * lam_b)
        tab_s[hd, _RT_STATE_B] = jnp.exp(rowi * lam_b)
        tab_s[hd, _RT_DEC_B] = ones * jnp.exp(CHUNK * lam_b)

    sf_s[...] = jnp.zeros_like(sf_s)
    sb_s[...] = jnp.zeros_like(sb_s)

    def fwd_chunk(q_ref, k_ref, v_ref, y_ref, r0, idx):
        heads = range(RET_HEADS)
        cols = [slice(hd * RET_DK, (hd + 1) * RET_DK) for hd in heads]
        q = [q_ref[pl.ds(r0, CHUNK), cols[hd]] for hd in heads]
        k = [k_ref[pl.ds(r0, CHUNK), cols[hd]] for hd in heads]
        v = [v_ref[pl.ds(r0, CHUNK), cols[hd]] for hd in heads]
        s = [_dot_nt(q[hd], k[hd]) for hd in heads]
        ds = []
        for hd in heads:
            vf = v[hd].astype(F32)
            vw = jnp.concatenate([(vf * tab_s[hd, _RT_STATE_F]).astype(BF16),
                                  (vf * tab_s[hd, _RT_STATE_B]).astype(BF16)], axis=1)
            ds.append(_dot_tn(k[hd], vw))
        for hd in heads:
            sf = sf_s[hd]
            q_scaled = (q[hd].astype(F32) * tab_s[hd, _RT_INTER_F]).astype(BF16)
            lhs = jnp.concatenate([(s[hd] * tab_s[hd, _RT_INTRA]).astype(BF16), q_scaled], axis=1)
            rhs = jnp.concatenate([v[hd], sf.astype(BF16)], axis=0)
            y_ref[pl.ds(r0, CHUNK), cols[hd]] = _dot(lhs, rhs)
            sf_s[hd] = sf * tab_s[hd, _RT_DEC_F] + ds[hd][:, :RET_DV]
            ds_s[idx, hd] = ds[hd][:, RET_DV:]

    def bwd_chunk(q_ref, g_ref, y_ref, o_ref, r0, idx):
        for hd in range(RET_HEADS):
            cols = slice(hd * RET_DK, (hd + 1) * RET_DK)
            sb = sb_s[hd]
            if o_ref is not None:
                q = q_ref[pl.ds(r0, CHUNK), cols]
                y = y_ref[pl.ds(r0, CHUNK), cols] + tab_s[hd, _RT_INTER_B] * _dot(q, sb.astype(BF16))
                gate = _silu(g_ref[pl.ds(r0, CHUNK), cols].astype(F32))
                o_ref[pl.ds(r0, CHUNK), cols] = (_rms(y) * gate).astype(BF16)
            sb_s[hd] = sb * tab_s[hd, _RT_DEC_B] + ds_s[idx, hd]

    for c in range(n_ctx_chunks):
        fwd_chunk(qc_ref, kc_ref, vc_ref, yc_s, c * CHUNK, c)

    def lat_fwd(c, carry):
        fwd_chunk(ql_ref, kl_ref, vl_ref, yl_s, pl.multiple_of(c * CHUNK, CHUNK), n_ctx_chunks + c)
        return carry

    lax.fori_loop(0, n_lat_chunks, lat_fwd, 0, unroll=SCAN_UNROLL)

    for c in reversed(range(n_ctx_chunks)):
        bwd_chunk(qc_ref, gc_ref, yc_s, oc_ref, c * CHUNK, c)

    def lat_bwd(t, carry):
        c = n_lat_chunks - 1 - t
        bwd_chunk(ql_ref, gl_ref, yl_s, ol_ref, pl.multiple_of(c * CHUNK, CHUNK), n_ctx_chunks + c)
        return carry

    lax.fori_loop(0, n_lat_chunks, lat_bwd, 0, unroll=SCAN_UNROLL)


def _retention(tc, tl, decay_f, decay_b, batch, ctx_out):
    lc = tc["rq"].shape[0] // batch
    ll = tl["rq"].shape[0] // batch
    cspec = pl.BlockSpec((lc, RET_W), lambda b: (b, 0))
    lspec = pl.BlockSpec((ll, RET_W), lambda b: (b, 0))
    pspec = pl.BlockSpec((RET_HEADS, 1, LANE), lambda b: (0, 0, 0))
    out_specs = [lspec]
    out_shape = [jax.ShapeDtypeStruct((batch * ll, RET_W), BF16)]
    if ctx_out:
        out_specs = [cspec] + out_specs
        out_shape = [jax.ShapeDtypeStruct((batch * lc, RET_W), BF16)] + out_shape
    bcast = lambda p: jnp.broadcast_to(p.astype(F32)[:, None, None], (RET_HEADS, 1, LANE))
    outs = pl.pallas_call(
        functools.partial(_ret_kernel, lc // CHUNK, ll // CHUNK, ctx_out),
        grid=(batch,),
        in_specs=[cspec] * 4 + [lspec] * 4 + [pspec, pspec],
        out_specs=out_specs,
        out_shape=out_shape,
        scratch_shapes=[
            pltpu.VMEM((lc, RET_W), F32), pltpu.VMEM((ll, RET_W), F32),
            pltpu.VMEM((RET_HEADS, RET_DK, RET_DV), F32), pltpu.VMEM((RET_HEADS, RET_DK, RET_DV), F32),
            pltpu.VMEM((RET_HEADS, 7, CHUNK, CHUNK), F32),
            pltpu.VMEM(((lc + ll) // CHUNK, RET_HEADS, RET_DK, RET_DV), F32),
        ],
        compiler_params=_cparams(("parallel",)),
        name="retention",
    )(tc["rq"], tc["rk"], tc["rv"], tc["rg"], tl["rq"], tl["rk"], tl["rv"], tl["rg"],
      bcast(decay_f), bcast(decay_b))
    return (outs[0], outs[1]) if ctx_out else (None, outs[0])


def _expand_heads(v, e_ref):
    hi = v.astype(BF16)
    lo = (v - hi.astype(F32)).astype(BF16)
    e = e_ref[...]
    return _dot(hi, e) + _dot(lo, e)


def _cumsum_rows(v):
    rows = lax.broadcasted_iota(jnp.int32, v.shape, 0)
    s = 1
    while s < v.shape[0]:
        v = v + jnp.where(rows >= s, pltpu.roll(v, s, 0), 0.0)
        s *= 2
    return v


def _ssd_kernel(n_ctx_chunks, n_lat_chunks, ctx_out, *refs):
    (xc_ref, dtc_ref, zc_ref, xl_ref, dtl_ref, zl_ref,
     cw_ref, cb_ref, bias_ref, alog_ref, skip_ref, ng_ref, ef_ref, eb_ref, shift_ref) = refs[:15]
    if ctx_out:
        oc_ref, ol_ref, tok_s, y_s, sf_s, sb_s = refs[15:]
    else:
        ol_ref, tok_s, y_s, sf_s, sb_s = refs[15:]
        oc_ref = None
    lc = n_ctx_chunks * CHUNK

    ii = lax.broadcasted_iota(jnp.int32, (CHUNK, CHUNK), 0)
    jj = lax.broadcasted_iota(jnp.int32, (CHUNK, CHUNK), 1)
    lower = jj < ii
    upper = jj > ii
    neg_a = -jnp.exp(alog_ref[...])
    halo = CONV_HALO

    def conv_silu(x_ref, c, n_chunks):
        if isinstance(c, int):
            r0 = c * CHUNK
            prev0 = max(r0 - halo, 0)
            next0 = min(r0 + CHUNK, (n_chunks - 1) * CHUNK)
        else:
            r0 = pl.multiple_of(c * CHUNK, CHUNK)
            prev0 = pl.multiple_of(jnp.maximum(r0 - halo, 0), halo)
            next0 = pl.multiple_of(jnp.minimum(r0 + CHUNK, (n_chunks - 1) * CHUNK), halo)
        first = c == 0
        last = c == n_chunks - 1
        centre = x_ref[pl.ds(r0, CHUNK), :]
        prev = x_ref[pl.ds(prev0, halo), :]
        nxt = x_ref[pl.ds(next0, halo), :]
        prev = jnp.where(first, jnp.zeros_like(prev), prev)
        nxt = jnp.where(last, jnp.zeros_like(nxt), nxt)
        pad = jnp.zeros((CONV_EXT - CHUNK - 2 * halo, CONV_CH), BF16)
        ext = jnp.concatenate([centre, prev, nxt, pad], axis=0)
        acc = cb_ref[...] + cw_ref[SSD_CONV // 2:SSD_CONV // 2 + 1, :] * centre.astype(F32)
        taps = [k for k in range(SSD_CONV) if k != SSD_CONV // 2]
        for n, k in enumerate(taps):
            acc = acc + cw_ref[k:k + 1, :] * _dot(shift_ref[n], ext)
        return _silu(acc)

    def decays(dt_ref, r0):
        dt = jax.nn.softplus(dt_ref[pl.ds(r0, CHUNK), :] + bias_ref[...])
        la = dt * neg_a
        a_inc = _cumsum_rows(la)
        total = a_inc[CHUNK - 1:CHUNK, :]
        return dt, la, a_inc, total

    def decay_row(total, e_ref):
        return _expand_heads(jnp.broadcast_to(jnp.exp(total), (8, LANE)), e_ref)[0:1, :]

    def fwd_chunk(x_ref, dt_ref, c, n_chunks, base):
        if isinstance(c, int):
            r0 = c * CHUNK
            t0 = base + r0
        else:
            r0 = pl.multiple_of(c * CHUNK, CHUNK)
            t0 = pl.multiple_of(base + r0, CHUNK)
        tok = conv_silu(x_ref, c, n_chunks)
        tok_b = tok.astype(BF16)
        tok_s[pl.ds(t0, CHUNK), :] = tok_b
        xs = tok_b[:, :SSD_W]
        dt, la, a_inc, total = decays(dt_ref, r0)
        a_exc = a_inc - la
        log2e = math.log2(math.e)
        ldt = jnp.log2(dt)
        col_term = a_inc * log2e
        col_term_b = a_exc * log2e
        row_t = (jnp.where(lax.broadcasted_iota(jnp.int32, (CHUNK, LANE), 1) < SSD_HEADS,
                           ldt - col_term, ldt + col_term_b)).T
        dt_t = dt.T
        groups = range(SSD_GROUPS)
        bs = [tok_b[:, SSD_W + g * SSD_STATE:SSD_W + (g + 1) * SSD_STATE] for g in groups]
        cs = [tok_b[:, SSD_W + SSD_BC + g * SSD_STATE:SSD_W + SSD_BC + (g + 1) * SSD_STATE] for g in groups]
        gsl = [slice(g * GROUP_W, (g + 1) * GROUP_W) for g in groups]
        gmat = [_dot_nt(cs[g], bs[g]) for g in groups]
        inter = [_dot(cs[g], sf_s[:, gsl[g]].astype(BF16)) for g in groups]
        inter_scale = _expand_heads(jnp.exp(a_inc), ef_ref)
        state_scale = _expand_heads(jnp.exp(total - a_inc) * dt, ef_ref)
        dec = decay_row(total, ef_ref)
        xw = (xs.astype(F32) * state_scale).astype(BF16)
        dstate = [_dot_tn(bs[g], xw[:, gsl[g]]) for g in groups]
        ms = []
        for hd in range(SSD_HEADS):
            hb = SSD_HEADS + hd
            diag = jnp.log2(dt_t[hd:hd + 1, :] + dt_t[hb:hb + 1, :])
            expo = jnp.where(lower, col_term[:, hd:hd + 1] + row_t[hd:hd + 1, :],
                             jnp.where(upper, row_t[hb:hb + 1, :] - col_term_b[:, hb:hb + 1], diag))
            ms.append((gmat[hd // HEADS_PER_GROUP] * jnp.exp2(expo)).astype(BF16))
        ys = []
        for pair in range(SSD_HEADS // 2):
            col = 2 * pair * SSD_HEAD_DIM
            xpair = xs[:, col:col + 2 * SSD_HEAD_DIM]
            lane = lax.broadcasted_iota(jnp.int32, xpair.shape, 1)
            zero = jnp.zeros_like(xpair)
            rhs = jnp.concatenate([jnp.where(lane < SSD_HEAD_DIM, xpair, zero),
                                   jnp.where(lane < SSD_HEAD_DIM, zero, xpair)], axis=0)
            ys.append(_dot(jnp.concatenate(ms[2 * pair:2 * pair + 2], axis=1), rhs))
        y_s[pl.ds(t0, CHUNK), :] = jnp.concatenate(ys, axis=1) + inter_scale * jnp.concatenate(inter, axis=1)
        for g in groups:
            sf_s[:, gsl[g]] = sf_s[:, gsl[g]] * dec[:, gsl[g]] + dstate[g]

    def bwd_chunk(dt_ref, z_ref, o_ref, c, base):
        if isinstance(c, int):
            r0 = c * CHUNK
            t0 = base + r0
        else:
            r0 = pl.multiple_of(c * CHUNK, CHUNK)
            t0 = pl.multiple_of(base + r0, CHUNK)
        tok_b = tok_s[pl.ds(t0, CHUNK), :]
        xs = tok_b[:, :SSD_W].astype(F32)
        dt, la, a_inc, total = decays(dt_ref, r0)
        a_exc = a_inc - la
        groups = range(SSD_GROUPS)
        gsl = [slice(g * GROUP_W, (g + 1) * GROUP_W) for g in groups]
        bs = [tok_b[:, SSD_W + g * SSD_STATE:SSD_W + (g + 1) * SSD_STATE] for g in groups]
        xw = (xs * _expand_heads(jnp.exp(a_exc) * dt, eb_ref)).astype(BF16)
        dstate = [_dot_tn(bs[g], xw[:, gsl[g]]) for g in groups]
        dec = decay_row(total, eb_ref)
        if o_ref is not None:
            cs = [tok_b[:, SSD_W + SSD_BC + g * SSD_STATE:SSD_W + SSD_BC + (g + 1) * SSD_STATE] for g in groups]
            inter = [_dot(cs[g], sb_s[:, gsl[g]].astype(BF16)) for g in groups]
            y = (y_s[pl.ds(t0, CHUNK), :]
                 + _expand_heads(jnp.exp(total - a_exc), eb_ref) * jnp.concatenate(inter, axis=1))
            y = (y + skip_ref[...] * xs) * _silu(z_ref[pl.ds(r0, CHUNK), :].astype(F32))
            o_ref[pl.ds(r0, CHUNK), :] = (_rms(y) * ng_ref[...]).astype(BF16)
        for g in groups:
            sb_s[:, gsl[g]] = sb_s[:, gsl[g]] * dec[:, gsl[g]] + dstate[g]

    sf_s[...] = jnp.zeros_like(sf_s)
    sb_s[...] = jnp.zeros_like(sb_s)

    for c in range(n_ctx_chunks):
        fwd_chunk(xc_ref, dtc_ref, c, n_ctx_chunks, 0)

    def lat_fwd(c, carry):
        fwd_chunk(xl_ref, dtl_ref, c, n_lat_chunks, lc)
        return carry

    lax.fori_loop(0, n_lat_chunks, lat_fwd, 0, unroll=SCAN_UNROLL)

    for c in reversed(range(n_ctx_chunks)):
        bwd_chunk(dtc_ref, zc_ref, oc_ref, c, 0)

    def lat_bwd(t, carry):
        bwd_chunk(dtl_ref, zl_ref, ol_ref, n_lat_chunks - 1 - t, lc)
        return carry

    lax.fori_loop(0, n_lat_chunks, lat_bwd, 0, unroll=SCAN_UNROLL)


def _conv_shift_matrices():
    taps = [k for k in range(SSD_CONV) if k != SSD_CONV // 2]
    mats = np.zeros((len(taps), CHUNK, CONV_EXT), np.float32)
    for n, k in enumerate(taps):
        for i in range(CHUNK):
            src = i + k - SSD_CONV // 2
            if src < 0:
                col = CHUNK + CONV_HALO + src
            elif src >= CHUNK:
                col = CHUNK + CONV_HALO + (src - CHUNK)
            else:
                col = src
            mats[n, i, col] = 1.0
    return mats


def _head_lane_vec(f, b):
    v = jnp.zeros((1, LANE), F32)
    v = v.at[0, :SSD_HEADS].set(f.astype(F32))
    return v.at[0, SSD_HEADS:2 * SSD_HEADS].set(b.astype(F32))


def _ssd(tc, tl, p, batch, ctx_out):
    lc = tc["xbc"].shape[0] // batch
    ll = tl["xbc"].shape[0] // batch
    rows = lambda n, w: pl.BlockSpec((n, w), lambda b: (b, 0))
    vec = lambda w: pl.BlockSpec((1, w), lambda b: (0, 0))
    whole = lambda a: pl.BlockSpec(a.shape, lambda b: (0,) * a.ndim)
    head_of_col = np.arange(SSD_W) // SSD_HEAD_DIM
    ef = jnp.asarray(np.arange(LANE)[:, None] == head_of_col[None, :], BF16)
    eb = jnp.asarray(np.arange(LANE)[:, None] == head_of_col[None, :] + SSD_HEADS, BF16)
    shift = jnp.asarray(_conv_shift_matrices(), BF16)
    conv_w = p["conv_w"].astype(F32)
    conv_b = p["conv_b"].astype(F32).reshape(1, CONV_CH)
    bias = _head_lane_vec(p["dt_bias_f"], p["dt_bias_b"])
    alog = _head_lane_vec(p["a_log_f"], p["a_log_b"])
    skip = jnp.repeat(p["d_skip"].astype(F32), SSD_HEAD_DIM).reshape(1, SSD_W)
    norm_g = p["ssd_norm_g"].astype(F32).reshape(1, SSD_W)
    out_specs = [rows(ll, SSD_W)]
    out_shape = [jax.ShapeDtypeStruct((batch * ll, SSD_W), BF16)]
    if ctx_out:
        out_specs = [rows(lc, SSD_W)] + out_specs
        out_shape = [jax.ShapeDtypeStruct((batch * lc, SSD_W), BF16)] + out_shape
    outs = pl.pallas_call(
        functools.partial(_ssd_kernel, lc // CHUNK, ll // CHUNK, ctx_out),
        grid=(batch,),
        in_specs=[rows(lc, CONV_CH), rows(lc, LANE), rows(lc, SSD_W),
                  rows(ll, CONV_CH), rows(ll, LANE), rows(ll, SSD_W),
                  whole(conv_w), vec(CONV_CH), vec(LANE), vec(LANE), vec(SSD_W), vec(SSD_W),
                  whole(ef), whole(eb), whole(shift)],
        out_specs=out_specs,
        out_shape=out_shape,
        scratch_shapes=[
            pltpu.VMEM((lc + ll, CONV_CH), BF16), pltpu.VMEM((lc + ll, SSD_W), F32),
            pltpu.VMEM((SSD_STATE, SSD_W), F32), pltpu.VMEM((SSD_STATE, SSD_W), F32),
        ],
        compiler_params=_cparams(("parallel",)),
        name="ssd_scan",
    )(tc["xbc"], tc["dt"], tc["z"], tl["xbc"], tl["dt"], tl["z"],
      conv_w, conv_b, bias, alog, skip, norm_g, ef, eb, shift)
    return (outs[0], outs[1]) if ctx_out else (None, outs[0])


def _out_proj_kernel(n_sub, att_ref, ret_ref, ssd_ref, w_ref, x_ref, g1_ref, pmg_ref, x1_ref):
    ts = x_ref.shape[0] // n_sub
    for c in range(n_sub):
        rows = slice(c * ts, (c + 1) * ts)
        m = (_dot(att_ref[rows, :], w_ref[0:ATT_W, :])
             + _dot(ret_ref[rows, :], w_ref[ATT_W:ATT_W + RET_W, :])
             + _dot(ssd_ref[rows, :], w_ref[ATT_W + RET_W:, :]))
        x1_ref[rows, :] = x_ref[rows, :] + g1_ref[...] * (_rms(m) * pmg_ref[...])


def _out_proj(att, ret, ssd, w_out_b, x2d, mod4, group_of_tile, post_mix_g, tm):
    rows, d = x2d.shape
    row_spec = lambda w: pl.BlockSpec((tm, w), lambda i: (i, 0))
    mod_spec = lambda k: pl.BlockSpec((None, None, 1, d), lambda i: (group_of_tile(i), k, 0, 0))
    vec_spec = pl.BlockSpec((1, d), lambda i: (0, 0))
    n_sub = max(tm // OUT_PROJ_SUB_ROWS, 1)
    return pl.pallas_call(
        functools.partial(_out_proj_kernel, n_sub),
        grid=(rows // tm,),
        in_specs=[row_spec(ATT_W), row_spec(RET_W), row_spec(SSD_W),
                  pl.BlockSpec(w_out_b.shape, lambda i: (0, 0), pipeline_mode=pl.Buffered(1)),
                  row_spec(d), mod_spec(2), vec_spec],
        out_specs=row_spec(d),
        out_shape=jax.ShapeDtypeStruct((rows, d), F32),
        compiler_params=_cparams(("parallel",)),
        name="out_proj",
    )(att, ret, ssd, w_out_b, x2d, mod4, post_mix_g)


def _ffn_kernel(n_cast, *refs):
    x1_ref, sc_ref, sh_ref, pfg_ref, wg_ref, wu_ref, wd_ref, g2_ref, pg_ref = refs[:9]
    cast_src = refs[9:9 + n_cast]
    o_ref = refs[9 + n_cast]
    cast_dst = refs[10 + n_cast:10 + 2 * n_cast]
    hf_s = refs[10 + 2 * n_cast]
    for src, dst in zip(cast_src, cast_dst):
        dst[...] = src[...].astype(BF16)

    j = pl.program_id(1)
    last_j = pl.num_programs(1) - 1
    tm, n_out = o_ref.shape
    ts = _pick_tile(tm, FFN_SUB_ROWS)
    tr = _pick_tile(ts, FFN_EPILOGUE_ROWS)
    tn = _pick_tile(n_out, FFN_DOWN_COLS)

    def hidden(rows):
        hf = hf_s[rows, :]
        return (_silu(_dot(hf, wg_ref[...])) * _dot(hf, wu_ref[...])).astype(BF16)

    @pl.when(j == 0)
    def _():
        gain = pfg_ref[...] * (1.0 + sc_ref[...])
        for c in range(tm // ts):
            for r in range(ts // tr):
                rows = slice(c * ts + r * tr, c * ts + (r + 1) * tr)
                hf_s[rows, :] = (_rms(x1_ref[rows, :]) * gain + sh_ref[...]).astype(BF16)
            rows = slice(c * ts, (c + 1) * ts)
            o_ref[rows, :] = _dot(hidden(rows), wd_ref[...])

    @pl.when(jnp.logical_and(j > 0, j < last_j))
    def _():
        a = hidden(slice(0, tm))
        for n in range(n_out // tn):
            o_ref[:, n * tn:(n + 1) * tn] += _dot(a, wd_ref[:, n * tn:(n + 1) * tn])

    @pl.when(j == last_j)
    def _():
        gain = g2_ref[...] * pg_ref[...]
        for c in range(tm // ts):
            rows = slice(c * ts, (c + 1) * ts)
            f = o_ref[rows, :] + _dot(hidden(rows), wd_ref[...])
            o_ref[rows, :] = x1_ref[rows, :] + _rms(f) * gain


def _ffn_cast_plan(n_i, tf, layer, w_gate, w_up, w_down):
    _, d, dff = w_gate.shape
    if d % n_i or (d // n_i) % LANE:
        return None
    rd = d // n_i
    return [
        (w_gate, pl.BlockSpec((None, rd, tf), lambda i, j: (layer, i, j)), pl.BlockSpec((rd, tf), lambda i, j: (i, j)),
         (d, dff)),
        (w_up, pl.BlockSpec((None, rd, tf), lambda i, j: (layer, i, j)), pl.BlockSpec((rd, tf), lambda i, j: (i, j)),
         (d, dff)),
        (w_down, pl.BlockSpec((None, tf, rd), lambda i, j: (layer, j, i)), pl.BlockSpec((tf, rd), lambda i, j: (j, i)),
         (dff, d)),
    ]


def _ffn(x1, wg, wu, wd, mod4, group_of_tile, pre_ffn_g, post_ffn_g, tm, tf, cast_plan=None):
    rows, d = x1.shape
    dff = wg.shape[1]
    assert dff // tf >= 2
    mod_spec = lambda k: pl.BlockSpec((None, None, 1, d), lambda i, j: (group_of_tile(i), k, 0, 0))
    vec_spec = pl.BlockSpec((1, d), lambda i, j: (0, 0))
    cast_plan = cast_plan or []
    outs = pl.pallas_call(
        functools.partial(_ffn_kernel, len(cast_plan)),
        grid=(rows // tm, dff // tf),
        in_specs=[
            pl.BlockSpec((tm, d), lambda i, j: (i, 0)),
            mod_spec(4), mod_spec(3), vec_spec,
            pl.BlockSpec((d, tf), lambda i, j: (0, j)),
            pl.BlockSpec((d, tf), lambda i, j: (0, j)),
            pl.BlockSpec((tf, d), lambda i, j: (j, 0)),
            mod_spec(5), vec_spec,
        ] + [c[1] for c in cast_plan],
        out_specs=[pl.BlockSpec((tm, d), lambda i, j: (i, 0))] + [c[2] for c in cast_plan],
        out_shape=[jax.ShapeDtypeStruct((rows, d), F32)] + [jax.ShapeDtypeStruct(c[3], BF16) for c in cast_plan],
        scratch_shapes=[pltpu.VMEM((tm, d), BF16)],
        compiler_params=_cparams(("arbitrary", "arbitrary")),
        name="swiglu_ffn_cast" if cast_plan else "swiglu_ffn",
    )(x1, mod4, mod4, pre_ffn_g, wg, wu, wd, mod4, post_ffn_g, *[c[0] for c in cast_plan])
    return outs[0], tuple(outs[1:])


def _cast_kernel(w_ref, o_ref):
    o_ref[...] = w_ref[...].astype(BF16)


def _cast_layer(w_stack, layer, n_rows=None):
    _, rows, cols = w_stack.shape
    n_rows = rows if n_rows is None else n_rows
    tr = _pick_tile(n_rows, 256)
    return pl.pallas_call(
        _cast_kernel,
        grid=(n_rows // tr,),
        in_specs=[pl.BlockSpec((None, tr, cols), lambda i: (layer, i, 0))],
        out_specs=pl.BlockSpec((tr, cols), lambda i: (i, 0)),
        out_shape=jax.ShapeDtypeStruct((n_rows, cols), BF16),
        compiler_params=_cparams(("parallel",)),
        name="cast_weight",
    )(w_stack)


def _rope_tables(seq_len):
    rows = seq_len // GRID_W
    row = jnp.repeat(jnp.arange(rows, dtype=F32), GRID_W)
    col = jnp.tile(jnp.arange(GRID_W, dtype=F32), rows)
    n_freq = HEAD_DIM // 4
    inv = ROPE_THETA ** (-jnp.arange(n_freq, dtype=F32) / n_freq)
    ang = jnp.concatenate([row[:, None] * inv, col[:, None] * inv], axis=-1)
    cos, sin = jnp.cos(ang), jnp.sin(ang)
    cos_t = jnp.repeat(cos, 2, axis=-1)
    sin_t = jnp.stack([-sin, sin], axis=-1).reshape(seq_len, HEAD_DIM)
    return cos_t, sin_t


def _pick_tile(n, target):
    t = min(n, target)
    while n % t:
        t //= 2
    return t


def _cast_tail_kernel(n_valid, w_ref, o_ref):
    row = lax.broadcasted_iota(jnp.int32, w_ref.shape, 0)
    o_ref[...] = jnp.where(row < n_valid, w_ref[...], 0.0).astype(BF16)


def _cast_layer_tail(w_stack, layer, row0):
    _, rows, cols = w_stack.shape
    assert row0 % LANE == 0 and 0 < rows - row0 <= LANE
    return pl.pallas_call(
        functools.partial(_cast_tail_kernel, rows - row0),
        grid=(1,),
        in_specs=[pl.BlockSpec((None, LANE, cols), lambda i: (layer, row0 // LANE, 0))],
        out_specs=pl.BlockSpec((LANE, cols), lambda i: (0, 0)),
        out_shape=jax.ShapeDtypeStruct((LANE, cols), BF16),
        compiler_params=_cparams(("arbitrary",)),
        name="cast_weight_tail",
    )(w_stack)


def _in_proj_weights(l, w_in):
    w_in_t = jnp.swapaxes(w_in, 1, 2)
    return _cast_layer(w_in_t, l, OFF_DT), _cast_layer_tail(w_in_t, l, OFF_DT)


def kernel(x, c, ctx, c_ctx, w_mod, b_mod, pre_mix_g, post_mix_g, pre_ffn_g, post_ffn_g, w_in, q_norm_g, k_norm_g, ret_decay_f, ret_decay_b, conv_w, conv_b, dt_bias_f, dt_bias_b, a_log_f, a_log_b, d_skip, ssd_norm_g, w_out, w_gate, w_up, w_down):
    batch, seq, d = x.shape
    lc = ctx.shape[1]
    depth = w_mod.shape[0]
    assert batch < N_MOD_ROWS and seq % CHUNK == 0 and lc % CHUNK == 0 and seq % GRID_W == 0

    c16 = jnp.zeros((N_MOD_ROWS, d), F32).at[:batch].set(c.astype(F32)).at[batch].set(c_ctx.astype(F32))
    mod_all = _modulation(c16, w_mod.astype(F32), b_mod.astype(F32))
    rope_tabs = _rope_tables(seq)

    tm_lat = _pick_tile(seq, 512)
    tm_ctx = _pick_tile(batch * lc, 512)
    tm_ffn_lat = _pick_tile(seq, 1024)
    tm_ffn_ctx = _pick_tile(batch * lc, 1024)
    lat_group = lambda tm: (lambda i: i // (seq // tm))
    ctx_group = lambda tm: (lambda i: batch)
    tq_lat = _pick_tile(seq, 1024)
    tq_ctx = _pick_tile(lc, 256)
    sub_lat = max(tq_lat // 128, 1)
    sub_ctx = 2 if tq_ctx % 32 == 0 else 1
    tf = _pick_tile(w_gate.shape[2], 512)

    xl = x.reshape(batch * seq, d).astype(F32)
    xc = ctx.reshape(batch * lc, d).astype(F32)
    vec = lambda v: v.astype(F32).reshape(1, d)

    ffn_w = None
    for l in range(depth):
        last = l == depth - 1
        w_in_b, w_dt_b = _in_proj_weights(l, w_in)
        qg = q_norm_g[l].astype(F32).reshape(1, HEAD_DIM)
        kg = k_norm_g[l].astype(F32).reshape(1, HEAD_DIM)
        mod4 = mod_all[l].reshape(N_MOD_ROWS, 6, 1, d)
        p = dict(conv_w=conv_w[l], conv_b=conv_b[l], dt_bias_f=dt_bias_f[l], dt_bias_b=dt_bias_b[l],
                 a_log_f=a_log_f[l], a_log_b=a_log_b[l], d_skip=d_skip[l], ssd_norm_g=ssd_norm_g[l])

        tl = _in_proj(xl, mod4, lat_group(tm_lat), vec(pre_mix_g[l]), w_in_b, w_dt_b, qg, kg, rope_tabs, tm_lat)
        tc = _in_proj(xc, mod4, ctx_group(tm_ctx), vec(pre_mix_g[l]), w_in_b, w_dt_b, qg, kg, None, tm_ctx)

        ride = (w_out,) if ffn_w else (w_out, w_gate, w_up, w_down)
        att_l, rode = _attention(tl["aq"], [(tc["ak"], tc["av"]), (tl["ak"], tl["av"])], batch, tq_lat, sub_lat,
                                 ride, l)
        rode = rode or tuple(_cast_layer(w, l) for w in ride)
        w_out_b = rode[0]
        wg, wu, wd = ffn_w or rode[1:]
        ret_c, ret_l = _retention(tc, tl, ret_decay_f[l], ret_decay_b[l], batch, not last)
        ssd_c, ssd_l = _ssd(tc, tl, p, batch, not last)

        x1 = _out_proj(att_l, ret_l, ssd_l, w_out_b, xl, mod4, lat_group(tm_ffn_lat), vec(post_mix_g[l]), tm_ffn_lat)
        plan = None if last else _ffn_cast_plan(batch * seq // tm_ffn_lat, tf, l + 1, w_gate, w_up, w_down)
        xl, next_w = _ffn(x1, wg, wu, wd, mod4, lat_group(tm_ffn_lat), vec(pre_ffn_g[l]), vec(post_ffn_g[l]),
                          tm_ffn_lat, tf, plan)

        if not last:
            att_c, _ = _attention(tc["aq"], [(tc["ak"], tc["av"])], batch, tq_ctx, sub_ctx)
            x1c = _out_proj(att_c, ret_c, ssd_c, w_out_b, xc, mod4, ctx_group(tm_ffn_ctx), vec(post_mix_g[l]),
                            tm_ffn_ctx)
            xc, _ = _ffn(x1c, wg, wu, wd, mod4, ctx_group(tm_ffn_ctx), vec(pre_ffn_g[l]), vec(post_ffn_g[l]),
                         tm_ffn_ctx, tf)
            ffn_w = next_w if plan else None

    return xl.reshape(batch, seq, d).astype(x.dtype)
```

```python
import functools
import math

import numpy as np
import jax
import jax.numpy as jnp
from jax import lax
from jax.experimental import pallas as pl
from jax.experimental.pallas import tpu as pltpu

F32 = jnp.float32
BF16 = jnp.bfloat16

GRID_W = 64
HEAD_DIM = 128
ATT_HEADS = 6
ATT_KV_HEADS = 2
ATT_REP = ATT_HEADS // ATT_KV_HEADS
RET_HEADS = 4
RET_DK = 128
RET_DV = 128
SSD_HEADS = 12
SSD_HEAD_DIM = 64
SSD_GROUPS = 2
SSD_STATE = 128
SSD_CONV = 5
ATT_W = ATT_HEADS * HEAD_DIM
KV_W = ATT_KV_HEADS * HEAD_DIM
RET_QK_W = RET_HEADS * RET_DK
RET_W = RET_HEADS * RET_DV
SSD_W = SSD_HEADS * SSD_HEAD_DIM
SSD_BC = SSD_GROUPS * SSD_STATE
CONV_CH = SSD_W + 2 * SSD_BC
GROUP_W = SSD_W // SSD_GROUPS
HEADS_PER_GROUP = SSD_HEADS // SSD_GROUPS
CHUNK = 128
ROPE_THETA = 10000.0
EPS = 1e-6
LANE = 128
FFN_DOWN_COLS = 512
FFN_EPILOGUE_ROWS = 128
OUT_PROJ_SUB_ROWS = 256
FFN_SUB_ROWS = 256
SCAN_UNROLL = 4
CONV_HALO = 16
CONV_EXT = 256

OFF_AQ = 0
OFF_AK = OFF_AQ + ATT_W
OFF_AV = OFF_AK + KV_W
OFF_RQ = OFF_AV + KV_W
OFF_RK = OFF_RQ + RET_QK_W
OFF_RV = OFF_RK + RET_QK_W
OFF_RG = OFF_RV + RET_W
OFF_Z = OFF_RG + RET_W
OFF_XBC = OFF_Z + SSD_W
OFF_DT = OFF_XBC + CONV_CH

VMEM_LIMIT = 60 * 1024 * 1024
N_MOD_ROWS = 16


def _cparams(sem):
    return pltpu.CompilerParams(dimension_semantics=sem, vmem_limit_bytes=VMEM_LIMIT)


def _silu(v):
    return v * jax.nn.sigmoid(v)


def _rms(v):
    return v * lax.rsqrt(jnp.mean(v * v, axis=-1, keepdims=True) + EPS)


def _dot(a, b):
    return jnp.dot(a, b, preferred_element_type=F32)


def _dot_nt(a, b):
    return lax.dot_general(a, b, (((1,), (1,)), ((), ())), preferred_element_type=F32)


def _dot_tn(a, b):
    return lax.dot_general(a, b, (((0,), (0,)), ((), ())), preferred_element_type=F32)


def _mod_kernel(c_ref, w_ref, b_ref, o_ref):
    a = _silu(c_ref[...]).astype(BF16)
    o_ref[...] = _dot(a, w_ref[...].astype(BF16)) + b_ref[...]


def _modulation(c16, w_mod, b_mod):
    depth, d, n = w_mod.shape
    tn = _pick_tile(n, 1024)
    return pl.pallas_call(
        _mod_kernel,
        grid=(depth, n // tn),
        in_specs=[
            pl.BlockSpec((N_MOD_ROWS, d), lambda l, j: (0, 0)),
            pl.BlockSpec((None, d, tn), lambda l, j: (l, 0, j)),
            pl.BlockSpec((None, 1, tn), lambda l, j: (l, 0, j)),
        ],
        out_specs=pl.BlockSpec((None, N_MOD_ROWS, tn), lambda l, j: (l, 0, j)),
        out_shape=jax.ShapeDtypeStruct((depth, N_MOD_ROWS, n), F32),
        compiler_params=_cparams(("parallel", "parallel")),
        name="adaln_modulation",
    )(c16, w_mod, b_mod.reshape(depth, 1, n))


def _in_proj_kernel(use_rope, *refs):
    if use_rope:
        (x_ref, sc_ref, sh_ref, g_ref, w_ref, wdt_ref, qg_ref, kg_ref, cos_ref, sin_ref,
         aq_ref, ak_ref, av_ref, rq_ref, rk_ref, rv_ref, rg_ref, z_ref, xbc_ref, dt_ref) = refs
        cos = cos_ref[...]
        sin = sin_ref[...]
        even_lane = (lax.broadcasted_iota(jnp.int32, cos.shape, 1) & 1) == 0
    else:
        (x_ref, sc_ref, sh_ref, g_ref, w_ref, wdt_ref, qg_ref, kg_ref,
         aq_ref, ak_ref, av_ref, rq_ref, rk_ref, rv_ref, rg_ref, z_ref, xbc_ref, dt_ref) = refs

    x = x_ref[...]
    h = (_rms(x) * g_ref[...] * (1.0 + sc_ref[...]) + sh_ref[...]).astype(BF16)

    def proj(lo, width):
        return _dot_nt(h, w_ref[lo:lo + width, :])

    def rope(t):
        if not use_rope:
            return t
        partner = jnp.where(even_lane, pltpu.roll(t, HEAD_DIM - 1, 1), pltpu.roll(t, 1, 1))
        return t * cos + partner * sin

    att_scale = HEAD_DIM ** -0.5 * math.log2(math.e)
    ret_scale = RET_DK ** -0.5

    acc = proj(OFF_AQ, ATT_W)
    for hd in range(ATT_HEADS):
        t = acc[:, hd * HEAD_DIM:(hd + 1) * HEAD_DIM]
        t = rope(_rms(t) * qg_ref[...]) * att_scale
        aq_ref[:, hd * HEAD_DIM:(hd + 1) * HEAD_DIM] = t.astype(BF16)

    acc = proj(OFF_AK, KV_W)
    for hd in range(ATT_KV_HEADS):
        t = acc[:, hd * HEAD_DIM:(hd + 1) * HEAD_DIM]
        t = rope(_rms(t) * kg_ref[...])
        ak_ref[:, hd * HEAD_DIM:(hd + 1) * HEAD_DIM] = t.astype(BF16)

    av_ref[...] = proj(OFF_AV, KV_W).astype(BF16)

    acc = proj(OFF_RQ, RET_QK_W)
    for hd in range(RET_HEADS):
        t = rope(acc[:, hd * RET_DK:(hd + 1) * RET_DK])
        rq_ref[:, hd * RET_DK:(hd + 1) * RET_DK] = t.astype(BF16)

    acc = proj(OFF_RK, RET_QK_W)
    for hd in range(RET_HEADS):
        t = rope(acc[:, hd * RET_DK:(hd + 1) * RET_DK] * ret_scale)
        rk_ref[:, hd * RET_DK:(hd + 1) * RET_DK] = t.astype(BF16)

    rv_ref[...] = proj(OFF_RV, RET_W).astype(BF16)
    rg_ref[...] = proj(OFF_RG, RET_W).astype(BF16)
    z_ref[...] = proj(OFF_Z, SSD_W).astype(BF16)
    xbc_ref[...] = proj(OFF_XBC, CONV_CH).astype(BF16)
    dt_ref[...] = _dot_nt(h, wdt_ref[...])


def _in_proj(x2d, mod4, group_of_tile, pre_g, w_in_b, w_dt_b, qg, kg, rope_tabs, tm):
    rows, d = x2d.shape
    nt = rows // tm
    use_rope = rope_tabs is not None
    row_spec = lambda w: pl.BlockSpec((tm, w), lambda i: (i, 0))
    mod_spec = lambda k: pl.BlockSpec((None, None, 1, d), lambda i: (group_of_tile(i), k, 0, 0))
    vec_spec = lambda w: pl.BlockSpec((1, w), lambda i: (0, 0))
    in_specs = [
        row_spec(d), mod_spec(1), mod_spec(0), vec_spec(d),
        pl.BlockSpec((OFF_DT, d), lambda i: (0, 0), pipeline_mode=pl.Buffered(1)),
        pl.BlockSpec((LANE, d), lambda i: (0, 0)),
        vec_spec(HEAD_DIM), vec_spec(HEAD_DIM),
    ]
    args = [x2d, mod4, mod4, pre_g, w_in_b, w_dt_b, qg, kg]
    if use_rope:
        cos_t, sin_t = rope_tabs
        tiles_per_seq = cos_t.shape[0] // tm
        tab_spec = pl.BlockSpec((tm, HEAD_DIM), lambda i: (i % tiles_per_seq, 0))
        in_specs += [tab_spec, tab_spec]
        args += [cos_t, sin_t]
    widths = (ATT_W, KV_W, KV_W, RET_QK_W, RET_QK_W, RET_W, RET_W, SSD_W, CONV_CH)
    out_specs = [row_spec(w) for w in widths] + [row_spec(LANE)]
    out_shape = [jax.ShapeDtypeStruct((rows, w), BF16) for w in widths]
    out_shape.append(jax.ShapeDtypeStruct((rows, LANE), F32))
    outs = pl.pallas_call(
        functools.partial(_in_proj_kernel, use_rope),
        grid=(nt,),
        in_specs=in_specs,
        out_specs=out_specs,
        out_shape=out_shape,
        compiler_params=_cparams(("parallel",)),
        name="in_proj_rope" if use_rope else "in_proj",
    )(*args)
    names = ("aq", "ak", "av", "rq", "rk", "rv", "rg", "z", "xbc", "dt")
    return dict(zip(names, outs))


def _attn_kernel(n_kv, tq, n_sub, n_cast, *refs):
    q_ref = refs[0]
    kv_refs = refs[1:1 + 2 * n_kv]
    cast_src = refs[1 + 2 * n_kv:1 + 2 * n_kv + n_cast]
    o_ref = refs[1 + 2 * n_kv + n_cast]
    cast_dst = refs[2 + 2 * n_kv + n_cast:]
    for src, dst in zip(cast_src, cast_dst):
        dst[...] = src[...].astype(BF16)
    ts = tq // n_sub
    all_scores = []
    for c in range(n_sub):
        q = q_ref[c * ts:(c + 1) * ts, :]
        qs = jnp.concatenate([q[:, r * HEAD_DIM:(r + 1) * HEAD_DIM] for r in range(ATT_REP)], axis=0)
        all_scores.append([_dot_nt(qs, kv_refs[2 * s][...]) for s in range(n_kv)])
    v_ones = []
    for s in range(n_kv):
        v = kv_refs[2 * s + 1][...]
        v_ones.append(jnp.concatenate([v, jnp.ones_like(v)], axis=1))
    for c in range(n_sub):
        scores = all_scores[c]
        m = functools.reduce(jnp.maximum, [jnp.max(s, axis=-1, keepdims=True) for s in scores])
        acc = functools.reduce(jnp.add, [_dot(jnp.exp2(s - m).astype(BF16), v_ones[i]) for i, s in enumerate(scores)])
        out = acc[:, :HEAD_DIM] / acc[:, HEAD_DIM:]
        for r in range(ATT_REP):
            o_ref[c * ts:(c + 1) * ts, r * HEAD_DIM:(r + 1) * HEAD_DIM] = out[r * ts:(r + 1) * ts].astype(BF16)


def _row_cast_plan(weights, layer, n_steps, step_of):
    plan = []
    for w in weights:
        _, rows, cols = w.shape
        if rows % n_steps or (rows // n_steps) % 16:
            return None
        rb = rows // n_steps
        plan.append((w, pl.BlockSpec((None, rb, cols), lambda *g: (layer, step_of(*g), 0)),
                     pl.BlockSpec((rb, cols), lambda *g: (step_of(*g), 0)), (rows, cols)))
    return plan


def _attention(q2d, kv_list, batch, tq, n_sub, cast_weights=None, cast_layer=0):
    rows = q2d.shape[0]
    lq = rows // batch
    nq = lq // tq
    gw = ATT_REP * HEAD_DIM
    in_specs = [pl.BlockSpec((tq, gw), lambda b, g, i: (b * nq + i, g))]
    args = [q2d]
    for k2d, v2d in kv_list:
        lk = k2d.shape[0] // batch
        spec = pl.BlockSpec((lk, HEAD_DIM), lambda b, g, i: (b, g))
        in_specs += [spec, spec]
        args += [k2d, v2d]
    n_steps = batch * ATT_KV_HEADS * nq
    plan = (_row_cast_plan(cast_weights, cast_layer, n_steps, lambda b, g, i: (b * ATT_KV_HEADS + g) * nq + i)
            if cast_weights else None) or []
    outs = pl.pallas_call(
        functools.partial(_attn_kernel, len(kv_list), tq, n_sub, len(plan)),
        grid=(batch, ATT_KV_HEADS, nq),
        in_specs=in_specs + [c[1] for c in plan],
        out_specs=[pl.BlockSpec((tq, gw), lambda b, g, i: (b * nq + i, g))] + [c[2] for c in plan],
        out_shape=[jax.ShapeDtypeStruct((rows, ATT_W), BF16)] + [jax.ShapeDtypeStruct(c[3], BF16) for c in plan],
        compiler_params=_cparams(("arbitrary", "arbitrary", "arbitrary")),
        name="attention_cast" if plan else "attention",
    )(*args, *[c[0] for c in plan])
    return outs[0], tuple(outs[1:])


_RT_INTRA, _RT_INTER_F, _RT_STATE_F, _RT_DEC_F, _RT_INTER_B, _RT_STATE_B, _RT_DEC_B = range(7)


def _ret_kernel(n_ctx_chunks, n_lat_chunks, ctx_out, *refs):
    (qc_ref, kc_ref, vc_ref, gc_ref, ql_ref, kl_ref, vl_ref, gl_ref, df_ref, db_ref) = refs[:10]
    if ctx_out:
        oc_ref, ol_ref, yc_s, yl_s, sf_s, sb_s, tab_s, ds_s = refs[10:]
    else:
        ol_ref, yc_s, yl_s, sf_s, sb_s, tab_s, ds_s = refs[10:]
        oc_ref = None

    ii = lax.broadcasted_iota(jnp.int32, (CHUNK, CHUNK), 0)
    jj = lax.broadcasted_iota(jnp.int32, (CHUNK, CHUNK), 1)
    dist = (ii - jj).astype(F32)
    rowi = ii.astype(F32)
    ones = jnp.ones((CHUNK, CHUNK), F32)
    for hd in range(RET_HEADS):
        lam_f = jnp.log1p(-jnp.exp2(df_ref[hd]))
        lam_b = jnp.log1p(-jnp.exp2(db_ref[hd]))
        tab_s[hd, _RT_INTRA] = (jnp.where(jj <= ii, jnp.exp(dist * lam_f), 0.0)
                                + jnp.where(jj >= ii, jnp.exp(-dist * lam_b), 0.0))
        tab_s[hd, _RT_INTER_F] = jnp.exp((rowi + 1.0) * lam_f)
        tab_s[hd, _RT_STATE_F] = jnp.exp((CHUNK - 1.0 - rowi) * lam_f)
        tab_s[hd, _RT_DEC_F] = ones * jnp.exp(CHUNK * lam_f)
        tab_s[hd, _RT_INTER_B] = jnp.exp((CHUNK - rowi) * lam_b)
        tab_s[hd, _RT_STATE_B] = jnp.exp(rowi * lam_b)
        tab_s[hd, _RT_DEC_B] = ones * jnp.exp(CHUNK * lam_b)

    sf_s[...] = jnp.zeros_like(sf_s)
    sb_s[...] = jnp.zeros_like(sb_s)

    def fwd_chunk(q_ref, k_ref, v_ref, y_ref, r0, idx):
        heads = range(RET_HEADS)
        cols = [slice(hd * RET_DK, (hd + 1) * RET_DK) for hd in heads]
        q = [q_ref[pl.ds(r0, CHUNK), cols[hd]] for hd in heads]
        k = [k_ref[pl.ds(r0, CHUNK), cols[hd]] for hd in heads]
        v = [v_ref[pl.ds(r0, CHUNK), cols[hd]] for hd in heads]
        s = [_dot_nt(q[hd], k[hd]) for hd in heads]
        ds = []
        for hd in heads:
            vf = v[hd].astype(F32)
            vw = jnp.concatenate([(vf * tab_s[hd, _RT_STATE_F]).astype(BF16),
                                  (vf * tab_s[hd, _RT_STATE_B]).astype(BF16)], axis=1)
            ds.append(_dot_tn(k[hd], vw))
        for hd in heads:
            sf = sf_s[hd]
            q_scaled = (q[hd].astype(F32) * tab_s[hd, _RT_INTER_F]).astype(BF16)
            lhs = jnp.concatenate([(s[hd] * tab_s[hd, _RT_INTRA]).astype(BF16), q_scaled], axis=1)
            rhs = jnp.concatenate([v[hd], sf.astype(BF16)], axis=0)
            y_ref[pl.ds(r0, CHUNK), cols[hd]] = _dot(lhs, rhs)
            sf_s[hd] = sf * tab_s[hd, _RT_DEC_F] + ds[hd][:, :RET_DV]
            ds_s[idx, hd] = ds[hd][:, RET_DV:]

    def bwd_chunk(q_ref, g_ref, y_ref, o_ref, r0, idx):
        for hd in range(RET_HEADS):
            cols = slice(hd * RET_DK, (hd + 1) * RET_DK)
            sb = sb_s[hd]
            if o_ref is not None:
                q = q_ref[pl.ds(r0, CHUNK), cols]
                y = y_ref[pl.ds(r0, CHUNK), cols] + tab_s[hd, _RT_INTER_B] * _dot(q, sb.astype(BF16))
                gate = _silu(g_ref[pl.ds(r0, CHUNK), cols].astype(F32))
                o_ref[pl.ds(r0, CHUNK), cols] = (_rms(y) * gate).astype(BF16)
            sb_s[hd] = sb * tab_s[hd, _RT_DEC_B] + ds_s[idx, hd]

    for c in range(n_ctx_chunks):
        fwd_chunk(qc_ref, kc_ref, vc_ref, yc_s, c * CHUNK, c)

    def lat_fwd(c, carry):
        fwd_chunk(ql_ref, kl_ref, vl_ref, yl_s, pl.multiple_of(c * CHUNK, CHUNK), n_ctx_chunks + c)
        return carry

    lax.fori_loop(0, n_lat_chunks, lat_fwd, 0, unroll=SCAN_UNROLL)

    for c in reversed(range(n_ctx_chunks)):
        bwd_chunk(qc_ref, gc_ref, yc_s, oc_ref, c * CHUNK, c)

    def lat_bwd(t, carry):
        c = n_lat_chunks - 1 - t
        bwd_chunk(ql_ref, gl_ref, yl_s, ol_ref, pl.multiple_of(c * CHUNK, CHUNK), n_ctx_chunks + c)
        return carry

    lax.fori_loop(0, n_lat_chunks, lat_bwd, 0, unroll=SCAN_UNROLL)


def _retention(tc, tl, decay_f, decay_b, batch, ctx_out):
    lc = tc["rq"].shape[0] // batch
    ll = tl["rq"].shape[0] // batch
    cspec = pl.BlockSpec((lc, RET_W), lambda b: (b, 0))
    lspec = pl.BlockSpec((ll, RET_W), lambda b: (b, 0))
    pspec = pl.BlockSpec((RET_HEADS, 1, LANE), lambda b: (0, 0, 0))
    out_specs = [lspec]
    out_shape = [jax.ShapeDtypeStruct((batch * ll, RET_W), BF16)]
    if ctx_out:
        out_specs = [cspec] + out_specs
        out_shape = [jax.ShapeDtypeStruct((batch * lc, RET_W), BF16)] + out_shape
    bcast = lambda p: jnp.broadcast_to(p.astype(F32)[:, None, None], (RET_HEADS, 1, LANE))
    outs = pl.pallas_call(
        functools.partial(_ret_kernel, lc // CHUNK, ll // CHUNK, ctx_out),
        grid=(batch,),
        in_specs=[cspec] * 4 + [lspec] * 4 + [pspec, pspec],
        out_specs=out_specs,
        out_shape=out_shape,
        scratch_shapes=[
            pltpu.VMEM((lc, RET_W), F32), pltpu.VMEM((ll, RET_W), F32),
            pltpu.VMEM((RET_HEADS, RET_DK, RET_DV), F32), pltpu.VMEM((RET_HEADS, RET_DK, RET_DV), F32),
            pltpu.VMEM((RET_HEADS, 7, CHUNK, CHUNK), F32),
            pltpu.VMEM(((lc + ll) // CHUNK, RET_HEADS, RET_DK, RET_DV), F32),
        ],
        compiler_params=_cparams(("parallel",)),
        name="retention",
    )(tc["rq"], tc["rk"], tc["rv"], tc["rg"], tl["rq"], tl["rk"], tl["rv"], tl["rg"],
      bcast(decay_f), bcast(decay_b))
    return (outs[0], outs[1]) if ctx_out else (None, outs[0])


def _expand_heads(v, e_ref):
    hi = v.astype(BF16)
    lo = (v - hi.astype(F32)).astype(BF16)
    e = e_ref[...]
    return _dot(hi, e) + _dot(lo, e)


def _cumsum_rows(v):
    rows = lax.broadcasted_iota(jnp.int32, v.shape, 0)
    s = 1
    while s < v.shape[0]:
        v = v + jnp.where(rows >= s, pltpu.roll(v, s, 0), 0.0)
        s *= 2
    return v


def _ssd_kernel(n_ctx_chunks, n_lat_chunks, ctx_out, *refs):
    (xc_ref, dtc_ref, zc_ref, xl_ref, dtl_ref, zl_ref,
     cw_ref, cb_ref, bias_ref, alog_ref, skip_ref, ng_ref, ef_ref, eb_ref, shift_ref) = refs[:15]
    if ctx_out:
        oc_ref, ol_ref, tok_s, y_s, sf_s, sb_s = refs[15:]
    else:
        ol_ref, tok_s, y_s, sf_s, sb_s = refs[15:]
        oc_ref = None
    lc = n_ctx_chunks * CHUNK

    ii = lax.broadcasted_iota(jnp.int32, (CHUNK, CHUNK), 0)
    jj = lax.broadcasted_iota(jnp.int32, (CHUNK, CHUNK), 1)
    lower = jj < ii
    upper = jj > ii
    neg_a = -jnp.exp(alog_ref[...])
    halo = CONV_HALO

    def conv_silu(x_ref, c, n_chunks):
        if isinstance(c, int):
            r0 = c * CHUNK
            prev0 = max(r0 - halo, 0)
            next0 = min(r0 + CHUNK, (n_chunks - 1) * CHUNK)
        else:
            r0 = pl.multiple_of(c * CHUNK, CHUNK)
            prev0 = pl.multiple_of(jnp.maximum(r0 - halo, 0), halo)
            next0 = pl.multiple_of(jnp.minimum(r0 + CHUNK, (n_chunks - 1) * CHUNK), halo)
        first = c == 0
        last = c == n_chunks - 1
        centre = x_ref[pl.ds(r0, CHUNK), :]
        prev = x_ref[pl.ds(prev0, halo), :]
        nxt = x_ref[pl.ds(next0, halo), :]
        prev = jnp.where(first, jnp.zeros_like(prev), prev)
        nxt = jnp.where(last, jnp.zeros_like(nxt), nxt)
        pad = jnp.zeros((CONV_EXT - CHUNK - 2 * halo, CONV_CH), BF16)
        ext = jnp.concatenate([centre, prev, nxt, pad], axis=0)
        acc = cb_ref[...] + cw_ref[SSD_CONV // 2:SSD_CONV // 2 + 1, :] * centre.astype(F32)
        taps = [k for k in range(SSD_CONV) if k != SSD_CONV // 2]
        for n, k in enumerate(taps):
            acc = acc + cw_ref[k:k + 1, :] * _dot(shift_ref[n], ext)
        return _silu(acc)

    def decays(dt_ref, r0):
        dt = jax.nn.softplus(dt_ref[pl.ds(r0, CHUNK), :] + bias_ref[...])
        la = dt * neg_a
        a_inc = _cumsum_rows(la)
        total = a_inc[CHUNK - 1:CHUNK, :]
        return dt, la, a_inc, total

    def decay_row(total, e_ref):
        return _expand_heads(jnp.broadcast_to(jnp.exp(total), (8, LANE)), e_ref)[0:1, :]

    def fwd_chunk(x_ref, dt_ref, c, n_chunks, base):
        if isinstance(c, int):
            r0 = c * CHUNK
            t0 = base + r0
        else:
            r0 = pl.multiple_of(c * CHUNK, CHUNK)
            t0 = pl.multiple_of(base + r0, CHUNK)
        tok = conv_silu(x_ref, c, n_chunks)
        tok_b = tok.astype(BF16)
        tok_s[pl.ds(t0, CHUNK), :] = tok_b
        xs = tok_b[:, :SSD_W]
        dt, la, a_inc, total = decays(dt_ref, r0)
        a_exc = a_inc - la
        log2e = math.log2(math.e)
        ldt = jnp.log2(dt)
        col_term = a_inc * log2e
        col_term_b = a_exc * log2e
        row_t = (jnp.where(lax.broadcasted_iota(jnp.int32, (CHUNK, LANE), 1) < SSD_HEADS,
                           ldt - col_term, ldt + col_term_b)).T
        dt_t = dt.T
        groups = range(SSD_GROUPS)
        bs = [tok_b[:, SSD_W + g * SSD_STATE:SSD_W + (g + 1) * SSD_STATE] for g in groups]
        cs = [tok_b[:, SSD_W + SSD_BC + g * SSD_STATE:SSD_W + SSD_BC + (g + 1) * SSD_STATE] for g in groups]
        gsl = [slice(g * GROUP_W, (g + 1) * GROUP_W) for g in groups]
        gmat = [_dot_nt(cs[g], bs[g]) for g in groups]
        inter = [_dot(cs[g], sf_s[:, gsl[g]].astype(BF16)) for g in groups]
        inter_scale = _expand_heads(jnp.exp(a_inc), ef_ref)
        state_scale = _expand_heads(jnp.exp(total - a_inc) * dt, ef_ref)
        dec = decay_row(total, ef_ref)
        xw = (xs.astype(F32) * state_scale).astype(BF16)
        dstate = [_dot_tn(bs[g], xw[:, gsl[g]]) for g in groups]
        ms = []
        for hd in range(SSD_HEADS):
            hb = SSD_HEADS + hd
            diag = jnp.log2(dt_t[hd:hd + 1, :] + dt_t[hb:hb + 1, :])
            expo = jnp.where(lower, col_term[:, hd:hd + 1] + row_t[hd:hd + 1, :],
                             jnp.where(upper, row_t[hb:hb + 1, :] - col_term_b[:, hb:hb + 1], diag))
            ms.append((gmat[hd // HEADS_PER_GROUP] * jnp.exp2(expo)).astype(BF16))
        ys = []
        for pair in range(SSD_HEADS // 2):
            col = 2 * pair * SSD_HEAD_DIM
            xpair = xs[:, col:col + 2 * SSD_HEAD_DIM]
            lane = lax.broadcasted_iota(jnp.int32, xpair.shape, 1)
            zero = jnp.zeros_like(xpair)
            rhs = jnp.concatenate([jnp.where(lane < SSD_HEAD_DIM, xpair, zero),
                                   jnp.where(lane < SSD_HEAD_DIM, zero, xpair)], axis=0)
            ys.append(_dot(jnp.concatenate(ms[2 * pair:2 * pair + 2], axis=1), rhs))
        y_s[pl.ds(t0, CHUNK), :] = jnp.concatenate(ys, axis=1) + inter_scale * jnp.concatenate(inter, axis=1)
        for g in groups:
            sf_s[:, gsl[g]] = sf_s[:, gsl[g]] * dec[:, gsl[g]] + dstate[g]

    def bwd_chunk(dt_ref, z_ref, o_ref, c, base):
        if isinstance(c, int):
            r0 = c * CHUNK
            t0 = base + r0
        else:
            r0 = pl.multiple_of(c * CHUNK, CHUNK)
            t0 = pl.multiple_of(base + r0, CHUNK)
        tok_b = tok_s[pl.ds(t0, CHUNK), :]
        xs = tok_b[:, :SSD_W].astype(F32)
        dt, la, a_inc, total = decays(dt_ref, r0)
        a_exc = a_inc - la
        groups = range(SSD_GROUPS)
        gsl = [slice(g * GROUP_W, (g + 1) * GROUP_W) for g in groups]
        bs = [tok_b[:, SSD_W + g * SSD_STATE:SSD_W + (g + 1) * SSD_STATE] for g in groups]
        xw = (xs * _expand_heads(jnp.exp(a_exc) * dt, eb_ref)).astype(BF16)
        dstate = [_dot_tn(bs[g], xw[:, gsl[g]]) for g in groups]
        dec = decay_row(total, eb_ref)
        if o_ref is not None:
            cs = [tok_b[:, SSD_W + SSD_BC + g * SSD_STATE:SSD_W + SSD_BC + (g + 1) * SSD_STATE] for g in groups]
            inter = [_dot(cs[g], sb_s[:, gsl[g]].astype(BF16)) for g in groups]
            y = (y_s[pl.ds(t0, CHUNK), :]
                 + _expand_heads(jnp.exp(total - a_exc), eb_ref) * jnp.concatenate(inter, axis=1))
            y = (y + skip_ref[...] * xs) * _silu(z_ref[pl.ds(r0, CHUNK), :].astype(F32))
            o_ref[pl.ds(r0, CHUNK), :] = (_rms(y) * ng_ref[...]).astype(BF16)
        for g in groups:
            sb_s[:, gsl[g]] = sb_s[:, gsl[g]] * dec[:, gsl[g]] + dstate[g]

    sf_s[...] = jnp.zeros_like(sf_s)
    sb_s[...] = jnp.zeros_like(sb_s)

    for c in range(n_ctx_chunks):
        fwd_chunk(xc_ref, dtc_ref, c, n_ctx_chunks, 0)

    def lat_fwd(c, carry):
        fwd_chunk(xl_ref, dtl_ref, c, n_lat_chunks, lc)
        return carry

    lax.fori_loop(0, n_lat_chunks, lat_fwd, 0, unroll=SCAN_UNROLL)

    for c in reversed(range(n_ctx_chunks)):
        bwd_chunk(dtc_ref, zc_ref, oc_ref, c, 0)

    def lat_bwd(t, carry):
        bwd_chunk(dtl_ref, zl_ref, ol_ref, n_lat_chunks - 1 - t, lc)
        return carry

    lax.fori_loop(0, n_lat_chunks, lat_bwd, 0, unroll=SCAN_UNROLL)


def _conv_shift_matrices():
    taps = [k for k in range(SSD_CONV) if k != SSD_CONV // 2]
    mats = np.zeros((len(taps), CHUNK, CONV_EXT), np.float32)
    for n, k in enumerate(taps):
        for i in range(CHUNK):
            src = i + k - SSD_CONV // 2
            if src < 0:
                col = CHUNK + CONV_HALO + src
            elif src >= CHUNK:
                col = CHUNK + CONV_HALO + (src - CHUNK)
            else:
                col = src
            mats[n, i, col] = 1.0
    return mats


def _head_lane_vec(f, b):
    v = jnp.zeros((1, LANE), F32)
    v = v.at[0, :SSD_HEADS].set(f.astype(F32))
    return v.at[0, SSD_HEADS:2 * SSD_HEADS].set(b.astype(F32))


def _ssd(tc, tl, p, batch, ctx_out):
    lc = tc["xbc"].shape[0] // batch
    ll = tl["xbc"].shape[0] // batch
    rows = lambda n, w: pl.BlockSpec((n, w), lambda b: (b, 0))
    vec = lambda w: pl.BlockSpec((1, w), lambda b: (0, 0))
    whole = lambda a: pl.BlockSpec(a.shape, lambda b: (0,) * a.ndim)
    head_of_col = np.arange(SSD_W) // SSD_HEAD_DIM
    ef = jnp.asarray(np.arange(LANE)[:, None] == head_of_col[None, :], BF16)
    eb = jnp.asarray(np.arange(LANE)[:, None] == head_of_col[None, :] + SSD_HEADS, BF16)
    shift = jnp.asarray(_conv_shift_matrices(), BF16)
    conv_w = p["conv_w"].astype(F32)
    conv_b = p["conv_b"].astype(F32).reshape(1, CONV_CH)
    bias = _head_lane_vec(p["dt_bias_f"], p["dt_bias_b"])
    alog = _head_lane_vec(p["a_log_f"], p["a_log_b"])
    skip = jnp.repeat(p["d_skip"].astype(F32), SSD_HEAD_DIM).reshape(1, SSD_W)
    norm_g = p["ssd_norm_g"].astype(F32).reshape(1, SSD_W)
    out_specs = [rows(ll, SSD_W)]
    out_shape = [jax.ShapeDtypeStruct((batch * ll, SSD_W), BF16)]
    if ctx_out:
        out_specs = [rows(lc, SSD_W)] + out_specs
        out_shape = [jax.ShapeDtypeStruct((batch * lc, SSD_W), BF16)] + out_shape
    outs = pl.pallas_call(
        functools.partial(_ssd_kernel, lc // CHUNK, ll // CHUNK, ctx_out),
        grid=(batch,),
        in_specs=[rows(lc, CONV_CH), rows(lc, LANE), rows(lc, SSD_W),
                  rows(ll, CONV_CH), rows(ll, LANE), rows(ll, SSD_W),
                  whole(conv_w), vec(CONV_CH), vec(LANE), vec(LANE), vec(SSD_W), vec(SSD_W),
                  whole(ef), whole(eb), whole(shift)],
        out_specs=out_specs,
        out_shape=out_shape,
        scratch_shapes=[
            pltpu.VMEM((lc + ll, CONV_CH), BF16), pltpu.VMEM((lc + ll, SSD_W), F32),
            pltpu.VMEM((SSD_STATE, SSD_W), F32), pltpu.VMEM((SSD_STATE, SSD_W), F32),
        ],
        compiler_params=_cparams(("parallel",)),
        name="ssd_scan",
    )(tc["xbc"], tc["dt"], tc["z"], tl["xbc"], tl["dt"], tl["z"],
      conv_w, conv_b, bias, alog, skip, norm_g, ef, eb, shift)
    return (outs[0], outs[1]) if ctx_out else (None, outs[0])


def _out_proj_kernel(n_sub, att_ref, ret_ref, ssd_ref, w_ref, x_ref, g1_ref, pmg_ref, x1_ref):
    ts = x_ref.shape[0] // n_sub
    for c in range(n_sub):
        rows = slice(c * ts, (c + 1) * ts)
        m = (_dot(att_ref[rows, :], w_ref[0:ATT_W, :])
             + _dot(ret_ref[rows, :], w_ref[ATT_W:ATT_W + RET_W, :])
             + _dot(ssd_ref[rows, :], w_ref[ATT_W + RET_W:, :]))
        x1_ref[rows, :] = x_ref[rows, :] + g1_ref[...] * (_rms(m) * pmg_ref[...])


def _out_proj(att, ret, ssd, w_out_b, x2d, mod4, group_of_tile, post_mix_g, tm):
    rows, d = x2d.shape
    row_spec = lambda w: pl.BlockSpec((tm, w), lambda i: (i, 0))
    mod_spec = lambda k: pl.BlockSpec((None, None, 1, d), lambda i: (group_of_tile(i), k, 0, 0))
    vec_spec = pl.BlockSpec((1, d), lambda i: (0, 0))
    n_sub = max(tm // OUT_PROJ_SUB_ROWS, 1)
    return pl.pallas_call(
        functools.partial(_out_proj_kernel, n_sub),
        grid=(rows // tm,),
        in_specs=[row_spec(ATT_W), row_spec(RET_W), row_spec(SSD_W),
                  pl.BlockSpec(w_out_b.shape, lambda i: (0, 0), pipeline_mode=pl.Buffered(1)),
                  row_spec(d), mod_spec(2), vec_spec],
        out_specs=row_spec(d),
        out_shape=jax.ShapeDtypeStruct((rows, d), F32),
        compiler_params=_cparams(("parallel",)),
        name="out_proj",
    )(att, ret, ssd, w_out_b, x2d, mod4, post_mix_g)


def _ffn_kernel(n_cast, *refs):
    x1_ref, sc_ref, sh_ref, pfg_ref, wg_ref, wu_ref, wd_ref, g2_ref, pg_ref = refs[:9]
    cast_src = refs[9:9 + n_cast]
    o_ref = refs[9 + n_cast]
    cast_dst = refs[10 + n_cast:10 + 2 * n_cast]
    hf_s = refs[10 + 2 * n_cast]
    for src, dst in zip(cast_src, cast_dst):
        dst[...] = src[...].astype(BF16)

    j = pl.program_id(1)
    last_j = pl.num_programs(1) - 1
    tm, n_out = o_ref.shape
    ts = _pick_tile(tm, FFN_SUB_ROWS)
    tr = _pick_tile(ts, FFN_EPILOGUE_ROWS)
    tn = _pick_tile(n_out, FFN_DOWN_COLS)

    def hidden(rows):
        hf = hf_s[rows, :]
        return (_silu(_dot(hf, wg_ref[...])) * _dot(hf, wu_ref[...])).astype(BF16)

    @pl.when(j == 0)
    def _():
        gain = pfg_ref[...] * (1.0 + sc_ref[...])
        for c in range(tm // ts):
            for r in range(ts // tr):
                rows = slice(c * ts + r * tr, c * ts + (r + 1) * tr)
                hf_s[rows, :] = (_rms(x1_ref[rows, :]) * gain + sh_ref[...]).astype(BF16)
            rows = slice(c * ts, (c + 1) * ts)
            o_ref[rows, :] = _dot(hidden(rows), wd_ref[...])

    @pl.when(jnp.logical_and(j > 0, j < last_j))
    def _():
        a = hidden(slice(0, tm))
        for n in range(n_out // tn):
            o_ref[:, n * tn:(n + 1) * tn] += _dot(a, wd_ref[:, n * tn:(n + 1) * tn])

    @pl.when(j == last_j)
    def _():
        gain = g2_ref[...] * pg_ref[...]
        for c in range(tm // ts):
            rows = slice(c * ts, (c + 1) * ts)
            f = o_ref[rows, :] + _dot(hidden(rows), wd_ref[...])
            o_ref[rows, :] = x1_ref[rows, :] + _rms(f) * gain


def _ffn_cast_plan(n_i, tf, layer, w_gate, w_up, w_down):
    _, d, dff = w_gate.shape
    if d % n_i or (d // n_i) % LANE:
        return None
    rd = d // n_i
    return [
        (w_gate, pl.BlockSpec((None, rd, tf), lambda i, j: (layer, i, j)), pl.BlockSpec((rd, tf), lambda i, j: (i, j)),
         (d, dff)),
        (w_up, pl.BlockSpec((None, rd, tf), lambda i, j: (layer, i, j)), pl.BlockSpec((rd, tf), lambda i, j: (i, j)),
         (d, dff)),
        (w_down, pl.BlockSpec((None, tf, rd), lambda i, j: (layer, j, i)), pl.BlockSpec((tf, rd), lambda i, j: (j, i)),
         (dff, d)),
    ]


def _ffn(x1, wg, wu, wd, mod4, group_of_tile, pre_ffn_g, post_ffn_g, tm, tf, cast_plan=None):
    rows, d = x1.shape
    dff = wg.shape[1]
    assert dff // tf >= 2
    mod_spec = lambda k: pl.BlockSpec((None, None, 1, d), lambda i, j: (group_of_tile(i), k, 0, 0))
    vec_spec = pl.BlockSpec((1, d), lambda i, j: (0, 0))
    cast_plan = cast_plan or []
    outs = pl.pallas_call(
        functools.partial(_ffn_kernel, len(cast_plan)),
        grid=(rows // tm, dff // tf),
        in_specs=[
            pl.BlockSpec((tm, d), lambda i, j: (i, 0)),
            mod_spec(4), mod_spec(3), vec_spec,
            pl.BlockSpec((d, tf), lambda i, j: (0, j)),
            pl.BlockSpec((d, tf), lambda i, j: (0, j)),
            pl.BlockSpec((tf, d), lambda i, j: (j, 0)),
            mod_spec(5), vec_spec,
        ] + [c[1] for c in cast_plan],
        out_specs=[pl.BlockSpec((tm, d), lambda i, j: (i, 0))] + [c[2] for c in cast_plan],
        out_shape=[jax.ShapeDtypeStruct((rows, d), F32)] + [jax.ShapeDtypeStruct(c[3], BF16) for c in cast_plan],
        scratch_shapes=[pltpu.VMEM((tm, d), BF16)],
        compiler_params=_cparams(("arbitrary", "arbitrary")),
        name="swiglu_ffn_cast" if cast_plan else "swiglu_ffn",
    )(x1, mod4, mod4, pre_ffn_g, wg, wu, wd, mod4, post_ffn_g, *[c[0] for c in cast_plan])
    return outs[0], tuple(outs[1:])


def _cast_kernel(w_ref, o_ref):
    o_ref[...] = w_ref[...].astype(BF16)


def _cast_layer(w_stack, layer, n_rows=None):
    _, rows, cols = w_stack.shape
    n_rows = rows if n_rows is None else n_rows
    tr = _pick_tile(n_rows, 256)
    return pl.pallas_call(
        _cast_kernel,
        grid=(n_rows // tr,),
        in_specs=[pl.BlockSpec((None, tr, cols), lambda i: (layer, i, 0))],
        out_specs=pl.BlockSpec((tr, cols), lambda i: (i, 0)),
        out_shape=jax.ShapeDtypeStruct((n_rows, cols), BF16),
        compiler_params=_cparams(("parallel",)),
        name="cast_weight",
    )(w_stack)


def _rope_tables(seq_len):
    rows = seq_len // GRID_W
    row = jnp.repeat(jnp.arange(rows, dtype=F32), GRID_W)
    col = jnp.tile(jnp.arange(GRID_W, dtype=F32), rows)
    n_freq = HEAD_DIM // 4
    inv = ROPE_THETA ** (-jnp.arange(n_freq, dtype=F32) / n_freq)
    ang = jnp.concatenate([row[:, None] * inv, col[:, None] * inv], axis=-1)
    cos, sin = jnp.cos(ang), jnp.sin(ang)
    cos_t = jnp.repeat(cos, 2, axis=-1)
    sin_t = jnp.stack([-sin, sin], axis=-1).reshape(seq_len, HEAD_DIM)
    return cos_t, sin_t


def _pick_tile(n, target):
    t = min(n, target)
    while n % t:
        t //= 2
    return t


def _cast_tail_kernel(n_valid, w_ref, o_ref):
    row = lax.broadcasted_iota(jnp.int32, w_ref.shape, 0)
    o_ref[...] = jnp.where(row < n_valid, w_ref[...], 0.0).astype(BF16)


def _cast_layer_tail(w_stack, layer, row0):
    _, rows, cols = w_stack.shape
    assert row0 % LANE == 0 and 0 < rows - row0 <= LANE
    return pl.pallas_call(
        functools.partial(_cast_tail_kernel, rows - row0),
        grid=(1,),
        in_specs=[pl.BlockSpec((None, LANE, cols), lambda i: (layer, row0 // LANE, 0))],
        out_specs=pl.BlockSpec((LANE, cols), lambda i: (0, 0)),
        out_shape=jax.ShapeDtypeStruct((LANE, cols), BF16),
        compiler_params=_cparams(("arbitrary",)),
        name="cast_weight_tail",
    )(w_stack)


def _in_proj_weights(l, w_in):
    w_in_t = jnp.swapaxes(w_in, 1, 2)
    return _cast_layer(w_in_t, l, OFF_DT), _cast_layer_tail(w_in_t, l, OFF_DT)


def kernel(x, c, ctx, c_ctx, w_mod, b_mod, pre_mix_g, post_mix_g, pre_ffn_g, post_ffn_g, w_in, q_norm_g, k_norm_g, ret_decay_f, ret_decay_b, conv_w, conv_b, dt_bias_f, dt_bias_b, a_log_f, a_log_b, d_skip, ssd_norm_g, w_out, w_gate, w_up, w_down):
    batch, seq, d = x.shape
    lc = ctx.shape[1]
    depth = w_mod.shape[0]
    assert batch < N_MOD_ROWS and seq % CHUNK == 0 and lc % CHUNK == 0 and seq % GRID_W == 0

    c16 = jnp.zeros((N_MOD_ROWS, d), F32).at[:batch].set(c.astype(F32)).at[batch].set(c_ctx.astype(F32))
    mod_all = _modulation(c16, w_mod.astype(F32), b_mod.astype(F32))
    rope_tabs = _rope_tables(seq)

    tm_lat = _pick_tile(seq, 512)
    tm_ctx = _pick_tile(batch * lc, 512)
    tm_ffn_lat = _pick_tile(seq, 1024)
    tm_ffn_ctx = _pick_tile(batch * lc, 1024)
    lat_group = lambda tm: (lambda i: i // (seq // tm))
    ctx_group = lambda tm: (lambda i: batch)
    tq_lat = _pick_tile(seq, 1024)
    tq_ctx = _pick_tile(lc, 256)
    sub_lat = max(tq_lat // 128, 1)
    sub_ctx = 2 if tq_ctx % 32 == 0 else 1
    tf = _pick_tile(w_gate.shape[2], 512)

    xl = x.reshape(batch * seq, d).astype(F32)
    xc = ctx.reshape(batch * lc, d).astype(F32)
    vec = lambda v: v.astype(F32).reshape(1, d)

    ffn_w = None
    for l in range(depth):
        last = l == depth - 1
        w_in_b, w_dt_b = _in_proj_weights(l, w_in)
        qg = q_norm_g[l].astype(F32).reshape(1, HEAD_DIM)
        kg = k_norm_g[l].astype(F32).reshape(1, HEAD_DIM)
        mod4 = mod_all[l].reshape(N_MOD_ROWS, 6, 1, d)
        p = dict(conv_w=conv_w[l], conv_b=conv_b[l], dt_bias_f=dt_bias_f[l], dt_bias_b=dt_bias_b[l],
                 a_log_f=a_log_f[l], a_log_b=a_log_b[l], d_skip=d_skip[l], ssd_norm_g=ssd_norm_g[l])

        tl = _in_proj(xl, mod4, lat_group(tm_lat), vec(pre_mix_g[l]), w_in_b, w_dt_b, qg, kg, rope_tabs, tm_lat)
        tc = _in_proj(xc, mod4, ctx_group(tm_ctx), vec(pre_mix_g[l]), w_in_b, w_dt_b, qg, kg, None, tm_ctx)

        ride = (w_out,) if ffn_w else (w_out, w_gate, w_up, w_down)
        att_l, rode = _attention(tl["aq"], [(tc["ak"], tc["av"]), (tl["ak"], tl["av"])], batch, tq_lat, sub_lat,
                                 ride, l)
        rode = rode or tuple(_cast_layer(w, l) for w in ride)
        w_out_b = rode[0]
        wg, wu, wd = ffn_w or rode[1:]
        ret_c, ret_l = _retention(tc, tl, ret_decay_f[l], ret_decay_b[l], batch, not last)
        ssd_c, ssd_l = _ssd(tc, tl, p, batch, not last)

        x1 = _out_proj(att_l, ret_l, ssd_l, w_out_b, xl, mod4, lat_group(tm_ffn_lat), vec(post_mix_g[l]), tm_ffn_lat)
        plan = None if last else _ffn_cast_plan(batch * seq // tm_ffn_lat, tf, l + 1, w_gate, w_up, w_down)
        xl, next_w = _ffn(x1, wg, wu, wd, mod4, lat_group(tm_ffn_lat), vec(pre_ffn_g[l]), vec(post_ffn_g[l]),
                          tm_ffn_lat, tf, plan)

        if not last:
            att_c, _ = _attention(tc["aq"], [(tc["ak"], tc["av"])], batch, tq_ctx, sub_ctx)
            x1c = _out_proj(att_c, ret_c, ssd_c, w_out_b, xc, mod4, ctx_group(tm_ffn_ctx), vec(post_mix_g[l]),
                            tm_ffn_ctx)
            xc, _ = _ffn(x1c, wg, wu, wd, mod4, ctx_group(tm_ffn_ctx), vec(pre_ffn_g[l]), vec(post_ffn_g[l]),
                         tm_ffn_ctx, tf)
            ffn_w = next_w if plan else None

    return xl.reshape(batch, seq, d).astype(x.dtype)
```

```python
import functools
import math

import numpy as np
import jax
import jax.numpy as jnp
from jax import lax
from jax.experimental import pallas as pl
from jax.experimental.pallas import tpu as pltpu

F32 = jnp.float32
BF16 = jnp.bfloat16

GRID_W = 64
HEAD_DIM = 128
ATT_HEADS = 6
ATT_KV_HEADS = 2
ATT_REP = ATT_HEADS // ATT_KV_HEADS
RET_HEADS = 4
RET_DK = 128
RET_DV = 128
SSD_HEADS = 12
SSD_HEAD_DIM = 64
SSD_GROUPS = 2
SSD_STATE = 128
SSD_CONV = 5
ATT_W = ATT_HEADS * HEAD_DIM
KV_W = ATT_KV_HEADS * HEAD_DIM
RET_QK_W = RET_HEADS * RET_DK
RET_W = RET_HEADS * RET_DV
SSD_W = SSD_HEADS * SSD_HEAD_DIM
SSD_BC = SSD_GROUPS * SSD_STATE
CONV_CH = SSD_W + 2 * SSD_BC
GROUP_W = SSD_W // SSD_GROUPS
HEADS_PER_GROUP = SSD_HEADS // SSD_GROUPS
CHUNK = 128
ROPE_THETA = 10000.0
EPS = 1e-6
LANE = 128
FFN_DOWN_COLS = 512
FFN_EPILOGUE_ROWS = 128
OUT_PROJ_SUB_ROWS = 256
FFN_SUB_ROWS = 256
SCAN_UNROLL = 4
CONV_HALO = 16
CONV_EXT = 256

OFF_AQ = 0
OFF_AK = OFF_AQ + ATT_W
OFF_AV = OFF_AK + KV_W
OFF_RQ = OFF_AV + KV_W
OFF_RK = OFF_RQ + RET_QK_W
OFF_RV = OFF_RK + RET_QK_W
OFF_RG = OFF_RV + RET_W
OFF_Z = OFF_RG + RET_W
OFF_XBC = OFF_Z + SSD_W
OFF_DT = OFF_XBC + CONV_CH

VMEM_LIMIT = 60 * 1024 * 1024
N_MOD_ROWS = 16


def _cparams(sem):
    return pltpu.CompilerParams(dimension_semantics=sem, vmem_limit_bytes=VMEM_LIMIT)


def _silu(v):
    return v * jax.nn.sigmoid(v)


def _rms(v):
    return v * lax.rsqrt(jnp.mean(v * v, axis=-1, keepdims=True) + EPS)


def _dot(a, b):
    return jnp.dot(a, b, preferred_element_type=F32)


def _dot_nt(a, b):
    return lax.dot_general(a, b, (((1,), (1,)), ((), ())), preferred_element_type=F32)


def _dot_tn(a, b):
    return lax.dot_general(a, b, (((0,), (0,)), ((), ())), preferred_element_type=F32)


def _mod_kernel(c_ref, w_ref, b_ref, o_ref):
    a = _silu(c_ref[...]).astype(BF16)
    o_ref[...] = _dot(a, w_ref[...].astype(BF16)) + b_ref[...]


def _modulation(c16, w_mod, b_mod):
    depth, d, n = w_mod.shape
    tn = _pick_tile(n, 1024)
    return pl.pallas_call(
        _mod_kernel,
        grid=(depth, n // tn),
        in_specs=[
            pl.BlockSpec((N_MOD_ROWS, d), lambda l, j: (0, 0)),
            pl.BlockSpec((None, d, tn), lambda l, j: (l, 0, j)),
            pl.BlockSpec((None, 1, tn), lambda l, j: (l, 0, j)),
        ],
        out_specs=pl.BlockSpec((None, N_MOD_ROWS, tn), lambda l, j: (l, 0, j)),
        out_shape=jax.ShapeDtypeStruct((depth, N_MOD_ROWS, n), F32),
        compiler_params=_cparams(("parallel", "parallel")),
        name="adaln_modulation",
    )(c16, w_mod, b_mod.reshape(depth, 1, n))


def _in_proj_kernel(use_rope, *refs):
    if use_rope:
        (x_ref, sc_ref, sh_ref, g_ref, w_ref, wdt_ref, qg_ref, kg_ref, cos_ref, sin_ref,
         aq_ref, ak_ref, av_ref, rq_ref, rk_ref, rv_ref, rg_ref, z_ref, xbc_ref, dt_ref) = refs
        cos = cos_ref[...]
        sin = sin_ref[...]
        even_lane = (lax.broadcasted_iota(jnp.int32, cos.shape, 1) & 1) == 0
    else:
        (x_ref, sc_ref, sh_ref, g_ref, w_ref, wdt_ref, qg_ref, kg_ref,
         aq_ref, ak_ref, av_ref, rq_ref, rk_ref, rv_ref, rg_ref, z_ref, xbc_ref, dt_ref) = refs

    x = x_ref[...]
    h = (_rms(x) * g_ref[...] * (1.0 + sc_ref[...]) + sh_ref[...]).astype(BF16)

    def proj(lo, width):
        return _dot_nt(h, w_ref[lo:lo + width, :])

    def rope(t):
        if not use_rope:
            return t
        partner = jnp.where(even_lane, pltpu.roll(t, HEAD_DIM - 1, 1), pltpu.roll(t, 1, 1))
        return t * cos + partner * sin

    att_scale = HEAD_DIM ** -0.5 * math.log2(math.e)
    ret_scale = RET_DK ** -0.5

    acc = proj(OFF_AQ, ATT_W)
    for hd in range(ATT_HEADS):
        t = acc[:, hd * HEAD_DIM:(hd + 1) * HEAD_DIM]
        t = rope(_rms(t) * qg_ref[...]) * att_scale
        aq_ref[:, hd * HEAD_DIM:(hd + 1) * HEAD_DIM] = t.astype(BF16)

    acc = proj(OFF_AK, KV_W)
    for hd in range(ATT_KV_HEADS):
        t = acc[:, hd * HEAD_DIM:(hd + 1) * HEAD_DIM]
        t = rope(_rms(t) * kg_ref[...])
        ak_ref[:, hd * HEAD_DIM:(hd + 1) * HEAD_DIM] = t.astype(BF16)

    av_ref[...] = proj(OFF_AV, KV_W).astype(BF16)

    acc = proj(OFF_RQ, RET_QK_W)
    for hd in range(RET_HEADS):
        t = rope(acc[:, hd * RET_DK:(hd + 1) * RET_DK])
        rq_ref[:, hd * RET_DK:(hd + 1) * RET_DK] = t.astype(BF16)

    acc = proj(OFF_RK, RET_QK_W)
    for hd in range(RET_HEADS):
        t = rope(acc[:, hd * RET_DK:(hd + 1) * RET_DK] * ret_scale)
        rk_ref[:, hd * RET_DK:(hd + 1) * RET_DK] = t.astype(BF16)

    rv_ref[...] = proj(OFF_RV, RET_W).astype(BF16)
    rg_ref[...] = proj(OFF_RG, RET_W).astype(BF16)
    z_ref[...] = proj(OFF_Z, SSD_W).astype(BF16)
    xbc_ref[...] = proj(OFF_XBC, CONV_CH).astype(BF16)
    dt_ref[...] = _dot_nt(h, wdt_ref[...])


def _in_proj(x2d, mod4, group_of_tile, pre_g, w_in_b, w_dt_b, qg, kg, rope_tabs, tm):
    rows, d = x2d.shape
    nt = rows // tm
    use_rope = rope_tabs is not None
    row_spec = lambda w: pl.BlockSpec((tm, w), lambda i: (i, 0))
    mod_spec = lambda k: pl.BlockSpec((None, None, 1, d), lambda i: (group_of_tile(i), k, 0, 0))
    vec_spec = lambda w: pl.BlockSpec((1, w), lambda i: (0, 0))
    in_specs = [
        row_spec(d), mod_spec(1), mod_spec(0), vec_spec(d),
        pl.BlockSpec((OFF_DT, d), lambda i: (0, 0), pipeline_mode=pl.Buffered(1)),
        pl.BlockSpec((LANE, d), lambda i: (0, 0)),
        vec_spec(HEAD_DIM), vec_spec(HEAD_DIM),
    ]
    args = [x2d, mod4, mod4, pre_g, w_in_b, w_dt_b, qg, kg]
    if use_rope:
        cos_t, sin_t = rope_tabs
        tiles_per_seq = cos_t.shape[0] // tm
        tab_spec = pl.BlockSpec((tm, HEAD_DIM), lambda i: (i % tiles_per_seq, 0))
        in_specs += [tab_spec, tab_spec]
        args += [cos_t, sin_t]
    widths = (ATT_W, KV_W, KV_W, RET_QK_W, RET_QK_W, RET_W, RET_W, SSD_W, CONV_CH)
    out_specs = [row_spec(w) for w in widths] + [row_spec(LANE)]
    out_shape = [jax.ShapeDtypeStruct((rows, w), BF16) for w in widths]
    out_shape.append(jax.ShapeDtypeStruct((rows, LANE), F32))
    outs = pl.pallas_call(
        functools.partial(_in_proj_kernel, use_rope),
        grid=(nt,),
        in_specs=in_specs,
        out_specs=out_specs,
        out_shape=out_shape,
        compiler_params=_cparams(("parallel",)),
        name="in_proj_rope" if use_rope else "in_proj",
    )(*args)
    names = ("aq", "ak", "av", "rq", "rk", "rv", "rg", "z", "xbc", "dt")
    return dict(zip(names, outs))


def _attn_kernel(n_kv, tq, n_sub, n_cast, *refs):
    q_ref = refs[0]
    kv_refs = refs[1:1 + 2 * n_kv]
    cast_src = refs[1 + 2 * n_kv:1 + 2 * n_kv + n_cast]
    o_ref = refs[1 + 2 * n_kv + n_cast]
    cast_dst = refs[2 + 2 * n_kv + n_cast:]
    for src, dst in zip(cast_src, cast_dst):
        dst[...] = src[...].astype(BF16)
    ts = tq // n_sub
    all_scores = []
    for c in range(n_sub):
        q = q_ref[c * ts:(c + 1) * ts, :]
        qs = jnp.concatenate([q[:, r * HEAD_DIM:(r + 1) * HEAD_DIM] for r in range(ATT_REP)], axis=0)
        all_scores.append([_dot_nt(qs, kv_refs[2 * s][...]) for s in range(n_kv)])
    v_ones = []
    for s in range(n_kv):
        v = kv_refs[2 * s + 1][...]
        v_ones.append(jnp.concatenate([v, jnp.ones_like(v)], axis=1))
    for c in range(n_sub):
        scores = all_scores[c]
        m = functools.reduce(jnp.maximum, [jnp.max(s, axis=-1, keepdims=True) for s in scores])
        acc = functools.reduce(jnp.add, [_dot(jnp.exp2(s - m).astype(BF16), v_ones[i]) for i, s in enumerate(scores)])
        out = acc[:, :HEAD_DIM] / acc[:, HEAD_DIM:]
        for r in range(ATT_REP):
            o_ref[c * ts:(c + 1) * ts, r * HEAD_DIM:(r + 1) * HEAD_DIM] = out[r * ts:(r + 1) * ts].astype(BF16)


def _row_cast_plan(weights, layer, n_steps, step_of):
    plan = []
    for w in weights:
        _, rows, cols = w.shape
        if rows % n_steps or (rows // n_steps) % 16:
            return None
        rb = rows // n_steps
        plan.append((w, pl.BlockSpec((None, rb, cols), lambda *g: (layer, step_of(*g), 0)),
                     pl.BlockSpec((rb, cols), lambda *g: (step_of(*g), 0)), (rows, cols)))
    return plan


def _attention(q2d, kv_list, batch, tq, n_sub, cast_weights=None, cast_layer=0):
    rows = q2d.shape[0]
    lq = rows // batch
    nq = lq // tq
    gw = ATT_REP * HEAD_DIM
    in_specs = [pl.BlockSpec((tq, gw), lambda b, g, i: (b * nq + i, g))]
    args = [q2d]
    for k2d, v2d in kv_list:
        lk = k2d.shape[0] // batch
        spec = pl.BlockSpec((lk, HEAD_DIM), lambda b, g, i: (b, g))
        in_specs += [spec, spec]
        args += [k2d, v2d]
    n_steps = batch * ATT_KV_HEADS * nq
    plan = (_row_cast_plan(cast_weights, cast_layer, n_steps, lambda b, g, i: (b * ATT_KV_HEADS + g) * nq + i)
            if cast_weights else None) or []
    outs = pl.pallas_call(
        functools.partial(_attn_kernel, len(kv_list), tq, n_sub, len(plan)),
        grid=(batch, ATT_KV_HEADS, nq),
        in_specs=in_specs + [c[1] for c in plan],
        out_specs=[pl.BlockSpec((tq, gw), lambda b, g, i: (b * nq + i, g))] + [c[2] for c in plan],
        out_shape=[jax.ShapeDtypeStruct((rows, ATT_W), BF16)] + [jax.ShapeDtypeStruct(c[3], BF16) for c in plan],
        compiler_params=_cparams(("arbitrary", "arbitrary", "arbitrary")),
        name="attention_cast" if plan else "attention",
    )(*args, *[c[0] for c in plan])
    return outs[0], tuple(outs[1:])


_RT_INTRA, _RT_INTER_F, _RT_STATE_F, _RT_DEC_F, _RT_INTER_B, _RT_STATE_B, _RT_DEC_B = range(7)


def _ret_kernel(n_ctx_chunks, n_lat_chunks, ctx_out, *refs):
    (qc_ref, kc_ref, vc_ref, gc_ref, ql_ref, kl_ref, vl_ref, gl_ref, df_ref, db_ref) = refs[:10]
    if ctx_out:
        oc_ref, ol_ref, yc_s, yl_s, sf_s, sb_s, tab_s, ds_s = refs[10:]
    else:
        ol_ref, yc_s, yl_s, sf_s, sb_s, tab_s, ds_s = refs[10:]
        oc_ref = None

    ii = lax.broadcasted_iota(jnp.int32, (CHUNK, CHUNK), 0)
    jj = lax.broadcasted_iota(jnp.int32, (CHUNK, CHUNK), 1)
    dist = (ii - jj).astype(F32)
    rowi = ii.astype(F32)
    ones = jnp.ones((CHUNK, CHUNK), F32)
    for hd in range(RET_HEADS):
        lam_f = jnp.log1p(-jnp.exp2(df_ref[hd]))
        lam_b = jnp.log1p(-jnp.exp2(db_ref[hd]))
        tab_s[hd, _RT_INTRA] = (jnp.where(jj <= ii, jnp.exp(dist * lam_f), 0.0)
                                + jnp.where(jj >= ii, jnp.exp(-dist * lam_b), 0.0))
        tab_s[hd, _RT_INTER_F] = jnp.exp((rowi + 1.0) * lam_f)
        tab_s[hd, _RT_STATE_F] = jnp.exp((CHUNK - 1.0 - rowi) * lam_f)
        tab_s[hd, _RT_DEC_F] = ones * jnp.exp(CHUNK * lam_f)
        tab_s[hd, _RT_INTER_B] = jnp.exp((CHUNK - rowi) * lam_b)
        tab_s[hd, _RT_STATE_B] = jnp.exp(rowi * lam_b)
        tab_s[hd, _RT_DEC_B] = ones * jnp.exp(CHUNK * lam_b)

    sf_s[...] = jnp.zeros_like(sf_s)
    sb_s[...] = jnp.zeros_like(sb_s)

    def fwd_chunk(q_ref, k_ref, v_ref, y_ref, r0, idx):
        heads = range(RET_HEADS)
        cols = [slice(hd * RET_DK, (hd + 1) * RET_DK) for hd in heads]
        q = [q_ref[pl.ds(r0, CHUNK), cols[hd]] for hd in heads]
        k = [k_ref[pl.ds(r0, CHUNK), cols[hd]] for hd in heads]
        v = [v_ref[pl.ds(r0, CHUNK), cols[hd]] for hd in heads]
        s = [_dot_nt(q[hd], k[hd]) for hd in heads]
        ds = []
        for hd in heads:
            vf = v[hd].astype(F32)
            vw = jnp.concatenate([(vf * tab_s[hd, _RT_STATE_F]).astype(BF16),
                                  (vf * tab_s[hd, _RT_STATE_B]).astype(BF16)], axis=1)
            ds.append(_dot_tn(k[hd], vw))
        for hd in heads:
            sf = sf_s[hd]
            q_scaled = (q[hd].astype(F32) * tab_s[hd, _RT_INTER_F]).astype(BF16)
            lhs = jnp.concatenate([(s[hd] * tab_s[hd, _RT_INTRA]).astype(BF16), q_scaled], axis=1)
            rhs = jnp.concatenate([v[hd], sf.astype(BF16)], axis=0)
            y_ref[pl.ds(r0, CHUNK), cols[hd]] = _dot(lhs, rhs)
            sf_s[hd] = sf * tab_s[hd, _RT_DEC_F] + ds[hd][:, :RET_DV]
            ds_s[idx, hd] = ds[hd][:, RET_DV:]

    def bwd_chunk(q_ref, g_ref, y_ref, o_ref, r0, idx):
        for hd in range(RET_HEADS):
            cols = slice(hd * RET_DK, (hd + 1) * RET_DK)
            sb = sb_s[hd]
            if o_ref is not None:
                q = q_ref[pl.ds(r0, CHUNK), cols]
                y = y_ref[pl.ds(r0, CHUNK), cols] + tab_s[hd, _RT_INTER_B] * _dot(q, sb.astype(BF16))
                gate = _silu(g_ref[pl.ds(r0, CHUNK), cols].astype(F32))
                o_ref[pl.ds(r0, CHUNK), cols] = (_rms(y) * gate).astype(BF16)
            sb_s[hd] = sb * tab_s[hd, _RT_DEC_B] + ds_s[idx, hd]

    for c in range(n_ctx_chunks):
        fwd_chunk(qc_ref, kc_ref, vc_ref, yc_s, c * CHUNK, c)

    def lat_fwd(c, carry):
        fwd_chunk(ql_ref, kl_ref, vl_ref, yl_s, pl.multiple_of(c * CHUNK, CHUNK), n_ctx_chunks + c)
        return carry

    lax.fori_loop(0, n_lat_chunks, lat_fwd, 0, unroll=SCAN_UNROLL)

    for c in reversed(range(n_ctx_chunks)):
        bwd_chunk(qc_ref, gc_ref, yc_s, oc_ref, c * CHUNK, c)

    def lat_bwd(t, carry):
        c = n_lat_chunks - 1 - t
        bwd_chunk(ql_ref, gl_ref, yl_s, ol_ref, pl.multiple_of(c * CHUNK, CHUNK), n_ctx_chunks + c)
        return carry

    lax.fori_loop(0, n_lat_chunks, lat_bwd, 0, unroll=SCAN_UNROLL)


def _retention(tc, tl, decay_f, decay_b, batch, ctx_out):
    lc = tc["rq"].shape[0] // batch
    ll = tl["rq"].shape[0] // batch
    cspec = pl.BlockSpec((lc, RET_W), lambda b: (b, 0))
    lspec = pl.BlockSpec((ll, RET_W), lambda b: (b, 0))
    pspec = pl.BlockSpec((RET_HEADS, 1, LANE), lambda b: (0, 0, 0))
    out_specs = [lspec]
    out_shape = [jax.ShapeDtypeStruct((batch * ll, RET_W), BF16)]
    if ctx_out:
        out_specs = [cspec] + out_specs
        out_shape = [jax.ShapeDtypeStruct((batch * lc, RET_W), BF16)] + out_shape
    bcast = lambda p: jnp.broadcast_to(p.astype(F32)[:, None, None], (RET_HEADS, 1, LANE))
    outs = pl.pallas_call(
        functools.partial(_ret_kernel, lc // CHUNK, ll // CHUNK, ctx_out),
        grid=(batch,),
        in_specs=[cspec] * 4 + [lspec] * 4 + [pspec, pspec],
        out_specs=out_specs,
        out_shape=out_shape,
        scratch_shapes=[
            pltpu.VMEM((lc, RET_W), F32), pltpu.VMEM((ll, RET_W), F32),
            pltpu.VMEM((RET_HEADS, RET_DK, RET_DV), F32), pltpu.VMEM((RET_HEADS, RET_DK, RET_DV), F32),
            pltpu.VMEM((RET_HEADS, 7, CHUNK, CHUNK), F32),
            pltpu.VMEM(((lc + ll) // CHUNK, RET_HEADS, RET_DK, RET_DV), F32),
        ],
        compiler_params=_cparams(("parallel",)),
        name="retention",
    )(tc["rq"], tc["rk"], tc["rv"], tc["rg"], tl["rq"], tl["rk"], tl["rv"], tl["rg"],
      bcast(decay_f), bcast(decay_b))
    return (outs[0], outs[1]) if ctx_out else (None, outs[0])


def _expand_heads(v, e_ref):
    hi = v.astype(BF16)
    lo = (v - hi.astype(F32)).astype(BF16)
    e = e_ref[...]
    return _dot(hi, e) + _dot(lo, e)


def _cumsum_rows(v):
    rows = lax.broadcasted_iota(jnp.int32, v.shape, 0)
    s = 1
    while s < v.shape[0]:
        v = v + jnp.where(rows >= s, pltpu.roll(v, s, 0), 0.0)
        s *= 2
    return v


def _ssd_kernel(n_ctx_chunks, n_lat_chunks, ctx_out, *refs):
    (xc_ref, dtc_ref, zc_ref, xl_ref, dtl_ref, zl_ref,
     cw_ref, cb_ref, bias_ref, alog_ref, skip_ref, ng_ref, ef_ref, eb_ref, shift_ref) = refs[:15]
    if ctx_out:
        oc_ref, ol_ref, tok_s, y_s, sf_s, sb_s = refs[15:]
    else:
        ol_ref, tok_s, y_s, sf_s, sb_s = refs[15:]
        oc_ref = None
    lc = n_ctx_chunks * CHUNK

    ii = lax.broadcasted_iota(jnp.int32, (CHUNK, CHUNK), 0)
    jj = lax.broadcasted_iota(jnp.int32, (CHUNK, CHUNK), 1)
    lower = jj < ii
    upper = jj > ii
    neg_a = -jnp.exp(alog_ref[...])
    halo = CONV_HALO

    def conv_silu(x_ref, c, n_chunks):
        if isinstance(c, int):
            r0 = c * CHUNK
            prev0 = max(r0 - halo, 0)
            next0 = min(r0 + CHUNK, (n_chunks - 1) * CHUNK)
        else:
            r0 = pl.multiple_of(c * CHUNK, CHUNK)
            prev0 = pl.multiple_of(jnp.maximum(r0 - halo, 0), halo)
            next0 = pl.multiple_of(jnp.minimum(r0 + CHUNK, (n_chunks - 1) * CHUNK), halo)
        first = c == 0
        last = c == n_chunks - 1
        centre = x_ref[pl.ds(r0, CHUNK), :]
        prev = x_ref[pl.ds(prev0, halo), :]
        nxt = x_ref[pl.ds(next0, halo), :]
        prev = jnp.where(first, jnp.zeros_like(prev), prev)
        nxt = jnp.where(last, jnp.zeros_like(nxt), nxt)
        pad = jnp.zeros((CONV_EXT - CHUNK - 2 * halo, CONV_CH), BF16)
        ext = jnp.concatenate([centre, prev, nxt, pad], axis=0)
        acc = cb_ref[...] + cw_ref[SSD_CONV // 2:SSD_CONV // 2 + 1, :] * centre.astype(F32)
        taps = [k for k in range(SSD_CONV) if k != SSD_CONV // 2]
        for n, k in enumerate(taps):
            acc = acc + cw_ref[k:k + 1, :] * _dot(shift_ref[n], ext)
        return _silu(acc)

    def decays(dt_ref, r0):
        dt = jax.nn.softplus(dt_ref[pl.ds(r0, CHUNK), :] + bias_ref[...])
        la = dt * neg_a
        a_inc = _cumsum_rows(la)
        total = a_inc[CHUNK - 1:CHUNK, :]
        return dt, la, a_inc, total

    def decay_row(total, e_ref):
        return _expand_heads(jnp.broadcast_to(jnp.exp(total), (8, LANE)), e_ref)[0:1, :]

    def fwd_chunk(dt_ref, c, base):
        if isinstance(c, int):
            r0 = c * CHUNK
            t0 = base + r0
        else:
            r0 = pl.multiple_of(c * CHUNK, CHUNK)
            t0 = pl.multiple_of(base + r0, CHUNK)
        tok_b = tok_s[pl.ds(t0, CHUNK), :]
        xs = tok_b[:, :SSD_W]
        dt, la, a_inc, total = decays(dt_ref, r0)
        a_exc = a_inc - la
        log2e = math.log2(math.e)
        ldt = jnp.log2(dt)
        col_term = a_inc * log2e
        col_term_b = a_exc * log2e
        row_t = (jnp.where(lax.broadcasted_iota(jnp.int32, (CHUNK, LANE), 1) < SSD_HEADS,
                           ldt - col_term, ldt + col_term_b)).T
        dt_t = dt.T
        groups = range(SSD_GROUPS)
        bs = [tok_b[:, SSD_W + g * SSD_STATE:SSD_W + (g + 1) * SSD_STATE] for g in groups]
        cs = [tok_b[:, SSD_W + SSD_BC + g * SSD_STATE:SSD_W + SSD_BC + (g + 1) * SSD_STATE] for g in groups]
        gsl = [slice(g * GROUP_W, (g + 1) * GROUP_W) for g in groups]
        gmat = [_dot_nt(cs[g], bs[g]) for g in groups]
        inter = [_dot(cs[g], sf_s[:, gsl[g]].astype(BF16)) for g in groups]
        inter_scale = _expand_heads(jnp.exp(a_inc), ef_ref)
        state_scale = _expand_heads(jnp.exp(total - a_inc) * dt, ef_ref)
        dec = decay_row(total, ef_ref)
        xw = (xs.astype(F32) * state_scale).astype(BF16)
        dstate = [_dot_tn(bs[g], xw[:, gsl[g]]) for g in groups]
        ms = []
        for hd in range(SSD_HEADS):
            hb = SSD_HEADS + hd
            diag = jnp.log2(dt_t[hd:hd + 1, :] + dt_t[hb:hb + 1, :])
            expo = jnp.where(lower, col_term[:, hd:hd + 1] + row_t[hd:hd + 1, :],
                             jnp.where(upper, row_t[hb:hb + 1, :] - col_term_b[:, hb:hb + 1], diag))
            ms.append((gmat[hd // HEADS_PER_GROUP] * jnp.exp2(expo)).astype(BF16))
        ys = []
        for pair in range(SSD_HEADS // 2):
            col = 2 * pair * SSD_HEAD_DIM
            xpair = xs[:, col:col + 2 * SSD_HEAD_DIM]
            lane = lax.broadcasted_iota(jnp.int32, xpair.shape, 1)
            zero = jnp.zeros_like(xpair)
            rhs = jnp.concatenate([jnp.where(lane < SSD_HEAD_DIM, xpair, zero),
                                   jnp.where(lane < SSD_HEAD_DIM, zero, xpair)], axis=0)
            ys.append(_dot(jnp.concatenate(ms[2 * pair:2 * pair + 2], axis=1), rhs))
        y_s[pl.ds(t0, CHUNK), :] = jnp.concatenate(ys, axis=1) + inter_scale * jnp.concatenate(inter, axis=1)
        for g in groups:
            sf_s[:, gsl[g]] = sf_s[:, gsl[g]] * dec[:, gsl[g]] + dstate[g]

    def bwd_chunk(dt_ref, z_ref, o_ref, c, base):
        if isinstance(c, int):
            r0 = c * CHUNK
            t0 = base + r0
        else:
            r0 = pl.multiple_of(c * CHUNK, CHUNK)
            t0 = pl.multiple_of(base + r0, CHUNK)
        tok_b = tok_s[pl.ds(t0, CHUNK), :]
        xs = tok_b[:, :SSD_W].astype(F32)
        dt, la, a_inc, total = decays(dt_ref, r0)
        a_exc = a_inc - la
        groups = range(SSD_GROUPS)
        gsl = [slice(g * GROUP_W, (g + 1) * GROUP_W) for g in groups]
        bs = [tok_b[:, SSD_W + g * SSD_STATE:SSD_W + (g + 1) * SSD_STATE] for g in groups]
        xw = (xs * _expand_heads(jnp.exp(a_exc) * dt, eb_ref)).astype(BF16)
        dstate = [_dot_tn(bs[g], xw[:, gsl[g]]) for g in groups]
        dec = decay_row(total, eb_ref)
        if o_ref is not None:
            cs = [tok_b[:, SSD_W + SSD_BC + g * SSD_STATE:SSD_W + SSD_BC + (g + 1) * SSD_STATE] for g in groups]
            inter = [_dot(cs[g], sb_s[:, gsl[g]].astype(BF16)) for g in groups]
            y = (y_s[pl.ds(t0, CHUNK), :]
                 + _expand_heads(jnp.exp(total - a_exc), eb_ref) * jnp.concatenate(inter, axis=1))
            y = (y + skip_ref[...] * xs) * _silu(z_ref[pl.ds(r0, CHUNK), :].astype(F32))
            o_ref[pl.ds(r0, CHUNK), :] = (_rms(y) * ng_ref[...]).astype(BF16)
        for g in groups:
            sb_s[:, gsl[g]] = sb_s[:, gsl[g]] * dec[:, gsl[g]] + dstate[g]

    sf_s[...] = jnp.zeros_like(sf_s)
    sb_s[...] = jnp.zeros_like(sb_s)

    for c in range(n_ctx_chunks):
        tok_s[pl.ds(c * CHUNK, CHUNK), :] = conv_silu(xc_ref, c, n_ctx_chunks).astype(BF16)

    def lat_conv(c, carry):
        t0 = pl.multiple_of(lc + c * CHUNK, CHUNK)
        tok_s[pl.ds(t0, CHUNK), :] = conv_silu(xl_ref, c, n_lat_chunks).astype(BF16)
        return carry

    lax.fori_loop(0, n_lat_chunks, lat_conv, 0, unroll=SCAN_UNROLL)

    for c in range(n_ctx_chunks):
        fwd_chunk(dtc_ref, c, 0)

    def lat_fwd(c, carry):
        fwd_chunk(dtl_ref, c, lc)
        return carry

    lax.fori_loop(0, n_lat_chunks, lat_fwd, 0, unroll=SCAN_UNROLL)

    for c in reversed(range(n_ctx_chunks)):
        bwd_chunk(dtc_ref, zc_ref, oc_ref, c, 0)

    def lat_bwd(t, carry):
        bwd_chunk(dtl_ref, zl_ref, ol_ref, n_lat_chunks - 1 - t, lc)
        return carry

    lax.fori_loop(0, n_lat_chunks, lat_bwd, 0, unroll=SCAN_UNROLL)


def _conv_shift_matrices():
    taps = [k for k in range(SSD_CONV) if k != SSD_CONV // 2]
    mats = np.zeros((len(taps), CHUNK, CONV_EXT), np.float32)
    for n, k in enumerate(taps):
        for i in range(CHUNK):
            src = i + k - SSD_CONV // 2
            if src < 0:
                col = CHUNK + CONV_HALO + src
            elif src >= CHUNK:
                col = CHUNK + CONV_HALO + (src - CHUNK)
            else:
                col = src
            mats[n, i, col] = 1.0
    return mats


def _head_lane_vec(f, b):
    v = jnp.zeros((1, LANE), F32)
    v = v.at[0, :SSD_HEADS].set(f.astype(F32))
    return v.at[0, SSD_HEADS:2 * SSD_HEADS].set(b.astype(F32))


def _ssd(tc, tl, p, batch, ctx_out):
    lc = tc["xbc"].shape[0] // batch
    ll = tl["xbc"].shape[0] // batch
    rows = lambda n, w: pl.BlockSpec((n, w), lambda b: (b, 0))
    vec = lambda w: pl.BlockSpec((1, w), lambda b: (0, 0))
    whole = lambda a: pl.BlockSpec(a.shape, lambda b: (0,) * a.ndim)
    head_of_col = np.arange(SSD_W) // SSD_HEAD_DIM
    ef = jnp.asarray(np.arange(LANE)[:, None] == head_of_col[None, :], BF16)
    eb = jnp.asarray(np.arange(LANE)[:, None] == head_of_col[None, :] + SSD_HEADS, BF16)
    shift = jnp.asarray(_conv_shift_matrices(), BF16)
    conv_w = p["conv_w"].astype(F32)
    conv_b = p["conv_b"].astype(F32).reshape(1, CONV_CH)
    bias = _head_lane_vec(p["dt_bias_f"], p["dt_bias_b"])
    alog = _head_lane_vec(p["a_log_f"], p["a_log_b"])
    skip = jnp.repeat(p["d_skip"].astype(F32), SSD_HEAD_DIM).reshape(1, SSD_W)
    norm_g = p["ssd_norm_g"].astype(F32).reshape(1, SSD_W)
    out_specs = [rows(ll, SSD_W)]
    out_shape = [jax.ShapeDtypeStruct((batch * ll, SSD_W), BF16)]
    if ctx_out:
        out_specs = [rows(lc, SSD_W)] + out_specs
        out_shape = [jax.ShapeDtypeStruct((batch * lc, SSD_W), BF16)] + out_shape
    outs = pl.pallas_call(
        functools.partial(_ssd_kernel, lc // CHUNK, ll // CHUNK, ctx_out),
        grid=(batch,),
        in_specs=[rows(lc, CONV_CH), rows(lc, LANE), rows(lc, SSD_W),
                  rows(ll, CONV_CH), rows(ll, LANE), rows(ll, SSD_W),
                  whole(conv_w), vec(CONV_CH), vec(LANE), vec(LANE), vec(SSD_W), vec(SSD_W),
                  whole(ef), whole(eb), whole(shift)],
        out_specs=out_specs,
        out_shape=out_shape,
        scratch_shapes=[
            pltpu.VMEM((lc + ll, CONV_CH), BF16), pltpu.VMEM((lc + ll, SSD_W), F32),
            pltpu.VMEM((SSD_STATE, SSD_W), F32), pltpu.VMEM((SSD_STATE, SSD_W), F32),
        ],
        compiler_params=_cparams(("parallel",)),
        name="ssd_scan",
    )(tc["xbc"], tc["dt"], tc["z"], tl["xbc"], tl["dt"], tl["z"],
      conv_w, conv_b, bias, alog, skip, norm_g, ef, eb, shift)
    return (outs[0], outs[1]) if ctx_out else (None, outs[0])


def _out_proj_kernel(n_sub, att_ref, ret_ref, ssd_ref, w_ref, x_ref, g1_ref, pmg_ref, x1_ref):
    ts = x_ref.shape[0] // n_sub
    for c in range(n_sub):
        rows = slice(c * ts, (c + 1) * ts)
        m = (_dot(att_ref[rows, :], w_ref[0:ATT_W, :])
             + _dot(ret_ref[rows, :], w_ref[ATT_W:ATT_W + RET_W, :])
             + _dot(ssd_ref[rows, :], w_ref[ATT_W + RET_W:, :]))
        x1_ref[rows, :] = x_ref[rows, :] + g1_ref[...] * (_rms(m) * pmg_ref[...])


def _out_proj(att, ret, ssd, w_out_b, x2d, mod4, group_of_tile, post_mix_g, tm):
    rows, d = x2d.shape
    row_spec = lambda w: pl.BlockSpec((tm, w), lambda i: (i, 0))
    mod_spec = lambda k: pl.BlockSpec((None, None, 1, d), lambda i: (group_of_tile(i), k, 0, 0))
    vec_spec = pl.BlockSpec((1, d), lambda i: (0, 0))
    n_sub = max(tm // OUT_PROJ_SUB_ROWS, 1)
    return pl.pallas_call(
        functools.partial(_out_proj_kernel, n_sub),
        grid=(rows // tm,),
        in_specs=[row_spec(ATT_W), row_spec(RET_W), row_spec(SSD_W),
                  pl.BlockSpec(w_out_b.shape, lambda i: (0, 0), pipeline_mode=pl.Buffered(1)),
                  row_spec(d), mod_spec(2), vec_spec],
        out_specs=row_spec(d),
        out_shape=jax.ShapeDtypeStruct((rows, d), F32),
        compiler_params=_cparams(("parallel",)),
        name="out_proj",
    )(att, ret, ssd, w_out_b, x2d, mod4, post_mix_g)


def _ffn_kernel(n_cast, *refs):
    x1_ref, sc_ref, sh_ref, pfg_ref, wg_ref, wu_ref, wd_ref, g2_ref, pg_ref = refs[:9]
    cast_src = refs[9:9 + n_cast]
    o_ref = refs[9 + n_cast]
    cast_dst = refs[10 + n_cast:10 + 2 * n_cast]
    hf_s = refs[10 + 2 * n_cast]
    for src, dst in zip(cast_src, cast_dst):
        dst[...] = src[...].astype(BF16)

    j = pl.program_id(1)
    last_j = pl.num_programs(1) - 1
    tm, n_out = o_ref.shape
    ts = _pick_tile(tm, FFN_SUB_ROWS)
    tr = _pick_tile(ts, FFN_EPILOGUE_ROWS)
    tn = _pick_tile(n_out, FFN_DOWN_COLS)

    def hidden(rows):
        hf = hf_s[rows, :]
        return (_silu(_dot(hf, wg_ref[...])) * _dot(hf, wu_ref[...])).astype(BF16)

    @pl.when(j == 0)
    def _():
        gain = pfg_ref[...] * (1.0 + sc_ref[...])
        for c in range(tm // ts):
            for r in range(ts // tr):
                rows = slice(c * ts + r * tr, c * ts + (r + 1) * tr)
                hf_s[rows, :] = (_rms(x1_ref[rows, :]) * gain + sh_ref[...]).astype(BF16)
            rows = slice(c * ts, (c + 1) * ts)
            o_ref[rows, :] = _dot(hidden(rows), wd_ref[...])

    @pl.when(jnp.logical_and(j > 0, j < last_j))
    def _():
        a = hidden(slice(0, tm))
        for n in range(n_out // tn):
            o_ref[:, n * tn:(n + 1) * tn] += _dot(a, wd_ref[:, n * tn:(n + 1) * tn])

    @pl.when(j == last_j)
    def _():
        gain = g2_ref[...] * pg_ref[...]
        for c in range(tm // ts):
            rows = slice(c * ts, (c + 1) * ts)
            f = o_ref[rows, :] + _dot(hidden(rows), wd_ref[...])
            o_ref[rows, :] = x1_ref[rows, :] + _rms(f) * gain


def _ffn_cast_plan(n_i, tf, layer, w_gate, w_up, w_down):
    _, d, dff = w_gate.shape
    if d % n_i or (d // n_i) % LANE:
        return None
    rd = d // n_i
    return [
        (w_gate, pl.BlockSpec((None, rd, tf), lambda i, j: (layer, i, j)), pl.BlockSpec((rd, tf), lambda i, j: (i, j)),
         (d, dff)),
        (w_up, pl.BlockSpec((None, rd, tf), lambda i, j: (layer, i, j)), pl.BlockSpec((rd, tf), lambda i, j: (i, j)),
         (d, dff)),
        (w_down, pl.BlockSpec((None, tf, rd), lambda i, j: (layer, j, i)), pl.BlockSpec((tf, rd), lambda i, j: (j, i)),
         (dff, d)),
    ]


def _ffn(x1, wg, wu, wd, mod4, group_of_tile, pre_ffn_g, post_ffn_g, tm, tf, cast_plan=None):
    rows, d = x1.shape
    dff = wg.shape[1]
    assert dff // tf >= 2
    mod_spec = lambda k: pl.BlockSpec((None, None, 1, d), lambda i, j: (group_of_tile(i), k, 0, 0))
    vec_spec = pl.BlockSpec((1, d), lambda i, j: (0, 0))
    cast_plan = cast_plan or []
    outs = pl.pallas_call(
        functools.partial(_ffn_kernel, len(cast_plan)),
        grid=(rows // tm, dff // tf),
        in_specs=[
            pl.BlockSpec((tm, d), lambda i, j: (i, 0)),
            mod_spec(4), mod_spec(3), vec_spec,
            pl.BlockSpec((d, tf), lambda i, j: (0, j)),
            pl.BlockSpec((d, tf), lambda i, j: (0, j)),
            pl.BlockSpec((tf, d), lambda i, j: (j, 0)),
            mod_spec(5), vec_spec,
        ] + [c[1] for c in cast_plan],
        out_specs=[pl.BlockSpec((tm, d), lambda i, j: (i, 0))] + [c[2] for c in cast_plan],
        out_shape=[jax.ShapeDtypeStruct((rows, d), F32)] + [jax.ShapeDtypeStruct(c[3], BF16) for c in cast_plan],
        scratch_shapes=[pltpu.VMEM((tm, d), BF16)],
        compiler_params=_cparams(("arbitrary", "arbitrary")),
        name="swiglu_ffn_cast" if cast_plan else "swiglu_ffn",
    )(x1, mod4, mod4, pre_ffn_g, wg, wu, wd, mod4, post_ffn_g, *[c[0] for c in cast_plan])
    return outs[0], tuple(outs[1:])


def _cast_kernel(w_ref, o_ref):
    o_ref[...] = w_ref[...].astype(BF16)


def _cast_layer(w_stack, layer, n_rows=None):
    _, rows, cols = w_stack.shape
    n_rows = rows if n_rows is None else n_rows
    tr = _pick_tile(n_rows, 256)
    return pl.pallas_call(
        _cast_kernel,
        grid=(n_rows // tr,),
        in_specs=[pl.BlockSpec((None, tr, cols), lambda i: (layer, i, 0))],
        out_specs=pl.BlockSpec((tr, cols), lambda i: (i, 0)),
        out_shape=jax.ShapeDtypeStruct((n_rows, cols), BF16),
        compiler_params=_cparams(("parallel",)),
        name="cast_weight",
    )(w_stack)


def _rope_tables(seq_len):
    rows = seq_len // GRID_W
    row = jnp.repeat(jnp.arange(rows, dtype=F32), GRID_W)
    col = jnp.tile(jnp.arange(GRID_W, dtype=F32), rows)
    n_freq = HEAD_DIM // 4
    inv = ROPE_THETA ** (-jnp.arange(n_freq, dtype=F32) / n_freq)
    ang = jnp.concatenate([row[:, None] * inv, col[:, None] * inv], axis=-1)
    cos, sin = jnp.cos(ang), jnp.sin(ang)
    cos_t = jnp.repeat(cos, 2, axis=-1)
    sin_t = jnp.stack([-sin, sin], axis=-1).reshape(seq_len, HEAD_DIM)
    return cos_t, sin_t


def _pick_tile(n, target):
    t = min(n, target)
    while n % t:
        t //= 2
    return t


def _cast_tail_kernel(n_valid, w_ref, o_ref):
    row = lax.broadcasted_iota(jnp.int32, w_ref.shape, 0)
    o_ref[...] = jnp.where(row < n_valid, w_ref[...], 0.0).astype(BF16)


def _cast_layer_tail(w_stack, layer, row0):
    _, rows, cols = w_stack.shape
    assert row0 % LANE == 0 and 0 < rows - row0 <= LANE
    return pl.pallas_call(
        functools.partial(_cast_tail_kernel, rows - row0),
        grid=(1,),
        in_specs=[pl.BlockSpec((None, LANE, cols), lambda i: (layer, row0 // LANE, 0))],
        out_specs=pl.BlockSpec((LANE, cols), lambda i: (0, 0)),
        out_shape=jax.ShapeDtypeStruct((LANE, cols), BF16),
        compiler_params=_cparams(("arbitrary",)),
        name="cast_weight_tail",
    )(w_stack)


def _in_proj_weights(l, w_in):
    w_in_t = jnp.swapaxes(w_in, 1, 2)
    return _cast_layer(w_in_t, l, OFF_DT), _cast_layer_tail(w_in_t, l, OFF_DT)


def kernel(x, c, ctx, c_ctx, w_mod, b_mod, pre_mix_g, post_mix_g, pre_ffn_g, post_ffn_g, w_in, q_norm_g, k_norm_g, ret_decay_f, ret_decay_b, conv_w, conv_b, dt_bias_f, dt_bias_b, a_log_f, a_log_b, d_skip, ssd_norm_g, w_out, w_gate, w_up, w_down):
    batch, seq, d = x.shape
    lc = ctx.shape[1]
    depth = w_mod.shape[0]
    assert batch < N_MOD_ROWS and seq % CHUNK == 0 and lc % CHUNK == 0 and seq % GRID_W == 0

    c16 = jnp.zeros((N_MOD_ROWS, d), F32).at[:batch].set(c.astype(F32)).at[batch].set(c_ctx.astype(F32))
    mod_all = _modulation(c16, w_mod.astype(F32), b_mod.astype(F32))
    rope_tabs = _rope_tables(seq)

    tm_lat = _pick_tile(seq, 512)
    tm_ctx = _pick_tile(batch * lc, 512)
    tm_ffn_lat = _pick_tile(seq, 1024)
    tm_ffn_ctx = _pick_tile(batch * lc, 1024)
    lat_group = lambda tm: (lambda i: i // (seq // tm))
    ctx_group = lambda tm: (lambda i: batch)
    tq_lat = _pick_tile(seq, 1024)
    tq_ctx = _pick_tile(lc, 256)
    sub_lat = max(tq_lat // 128, 1)
    sub_ctx = 2 if tq_ctx % 32 == 0 else 1
    tf = _pick_tile(w_gate.shape[2], 512)

    xl = x.reshape(batch * seq, d).astype(F32)
    xc = ctx.reshape(batch * lc, d).astype(F32)
    vec = lambda v: v.astype(F32).reshape(1, d)

    ffn_w = None
    for l in range(depth):
        last = l == depth - 1
        w_in_b, w_dt_b = _in_proj_weights(l, w_in)
        qg = q_norm_g[l].astype(F32).reshape(1, HEAD_DIM)
        kg = k_norm_g[l].astype(F32).reshape(1, HEAD_DIM)
        mod4 = mod_all[l].reshape(N_MOD_ROWS, 6, 1, d)
        p = dict(conv_w=conv_w[l], conv_b=conv_b[l], dt_bias_f=dt_bias_f[l], dt_bias_b=dt_bias_b[l],
                 a_log_f=a_log_f[l], a_log_b=a_log_b[l], d_skip=d_skip[l], ssd_norm_g=ssd_norm_g[l])

        tl = _in_proj(xl, mod4, lat_group(tm_lat), vec(pre_mix_g[l]), w_in_b, w_dt_b, qg, kg, rope_tabs, tm_lat)
        tc = _in_proj(xc, mod4, ctx_group(tm_ctx), vec(pre_mix_g[l]), w_in_b, w_dt_b, qg, kg, None, tm_ctx)

        ride = (w_out,) if ffn_w else (w_out, w_gate, w_up, w_down)
        att_l, rode = _attention(tl["aq"], [(tc["ak"], tc["av"]), (tl["ak"], tl["av"])], batch, tq_lat, sub_lat,
                                 ride, l)
        rode = rode or tuple(_cast_layer(w, l) for w in ride)
        w_out_b = rode[0]
        wg, wu, wd = ffn_w or rode[1:]
        ret_c, ret_l = _retention(tc, tl, ret_decay_f[l], ret_decay_b[l], batch, not last)
        ssd_c, ssd_l = _ssd(tc, tl, p, batch, not last)

        x1 = _out_proj(att_l, ret_l, ssd_l, w_out_b, xl, mod4, lat_group(tm_ffn_lat), vec(post_mix_g[l]), tm_ffn_lat)
        plan = None if last else _ffn_cast_plan(batch * seq // tm_ffn_lat, tf, l + 1, w_gate, w_up, w_down)
        xl, next_w = _ffn(x1, wg, wu, wd, mod4, lat_group(tm_ffn_lat), vec(pre_ffn_g[l]), vec(post_ffn_g[l]),
                          tm_ffn_lat, tf, plan)

        if not last:
            att_c, _ = _attention(tc["aq"], [(tc["ak"], tc["av"])], batch, tq_ctx, sub_ctx)
            x1c = _out_proj(att_c, ret_c, ssd_c, w_out_b, xc, mod4, ctx_group(tm_ffn_ctx), vec(post_mix_g[l]),
                            tm_ffn_ctx)
            xc, _ = _ffn(x1c, wg, wu, wd, mod4, ctx_group(tm_ffn_ctx), vec(pre_ffn_g[l]), vec(post_ffn_g[l]),
                         tm_ffn_ctx, tf)
            ffn_w = next_w if plan else None

    return xl.reshape(batch, seq, d).astype(x.dtype)
```
